```python
import math
import jax, jax.numpy as jnp
from jax import lax
import numpy as np

D_MODEL = 2048
BATCH = 1
SEQ = 8192
DEPTH = 4
DEC_BATCH = 8
DEC_SEQ = 2048
PAST_LEN = 128

HEAD_DIM = 128
D_HYENA = D_MODEL // 2
HYENA_GROUPS = D_HYENA // HEAD_DIM
D_ATTN = D_MODEL // 4
N_ATTN_HEADS = D_ATTN // HEAD_DIM
D_MEMX = D_MODEL // 4
N_MEM_HEADS = D_MEMX // HEAD_DIM
N_MEM = 256
D_IN = 3 * D_HYENA + 3 * D_ATTN + D_MEMX
DILATED_PATTERNS = ((128, 1), (512, 4), (2048, 16))
ROPE_THETA = 500000.0
ROT_DIM = HEAD_DIM // 4
FILTER_EMB = 33
FILTER_HIDDEN = 64
HYENA_ORDER = 2
N_DIRS = 2
DECAY_FAST = 0.3
DECAY_SLOW = 1.5
DECAY_TARGET = 1e-2
DECAY_SHIFT = 0.05
D_FF = 11 * D_MODEL // 4
CONV_WIDTH = 3
EPS = 1e-6
NEG = -1e30

kernel_name = "hybrid_hyena_dilated_memx_encoder"

F32 = jnp.float32


def rmsnorm(x, g):
    xf = x.astype(F32)
    y = xf * lax.rsqrt(jnp.mean(xf * xf, axis=-1, keepdims=True) + EPS)
    return (y * g.astype(F32)).astype(x.dtype)


def group_rmsnorm(x, g):
    xf = x.astype(F32)
    parts = jnp.split(xf, [D_HYENA, D_HYENA + D_ATTN], axis=-1)
    normed = [p * lax.rsqrt(jnp.mean(p * p, axis=-1, keepdims=True) + EPS) for p in parts]
    return (jnp.concatenate(normed, axis=-1) * g.astype(F32)).astype(x.dtype)


def dwconv3(u, w, b):
    up = jnp.pad(u, ((0, 0), (1, 1), (0, 0)))
    return up[:, :-2] * w[0] + up[:, 1:-1] * w[1] + up[:, 2:] * w[2] + b


def hyena_filters(L, w1, b1, w2, b2, w3, b3, w4, freq):
    t = jnp.linspace(0.0, 1.0, L, dtype=F32)[:, None]
    bands = (FILTER_EMB - 1) // 2
    w = 2.0 * math.pi * jnp.arange(L, dtype=F32) / L
    f = jnp.linspace(1e-4, bands - 1, bands, dtype=F32)
    ang = w[:, None] * f[None, :]
    z = jnp.concatenate([t, jnp.cos(ang), -jnp.sin(ang)], axis=-1)
    fr = freq.astype(F32)
    hdn = jnp.sin(fr * (z @ w1.astype(F32) + b1.astype(F32)))
    hdn = jnp.sin(fr * (hdn @ w2.astype(F32) + b2.astype(F32)))
    hdn = jnp.sin(fr * (hdn @ w3.astype(F32) + b3.astype(F32)))
    k = (hdn @ w4.astype(F32)).reshape(L, HYENA_ORDER, N_DIRS, D_HYENA)
    deltas = jnp.abs(jnp.linspace(math.log(DECAY_TARGET) / DECAY_SLOW,
                                  math.log(DECAY_TARGET) / DECAY_FAST, D_HYENA, dtype=F32))
    decay = jnp.exp(-t * deltas[None, :]) + DECAY_SHIFT
    k = k * decay[:, None, None, :]
    k = k.at[0, :, 1, :].set(0.0)
    k = k / jnp.sum(jnp.abs(k), axis=(0, 2), keepdims=True)
    return k.transpose(1, 2, 0, 3)


def long_conv(u, h_fwd, h_bwd, bias):
    L, C = u.shape[1], u.shape[2]
    kern = jnp.concatenate([h_fwd, jnp.zeros((1, C), F32), h_bwd[1:][::-1]], axis=0)
    K = jnp.fft.rfft(kern, axis=0)
    uf = u.astype(F32)
    U = jnp.fft.rfft(uf, n=2 * L, axis=1)
    y = jnp.fft.irfft(U * K[None], n=2 * L, axis=1)[:, :L]
    return (y + uf * bias.astype(F32)).astype(u.dtype)


def rope_partial(x):
    L = x.shape[1]
    half = ROT_DIM // 2
    inv_freq = jnp.exp(-math.log(ROPE_THETA) * jnp.arange(0, ROT_DIM, 2, dtype=F32) / ROT_DIM)
    ang = jnp.arange(L, dtype=F32)[:, None] * inv_freq[None, :]
    c = jnp.cos(ang)[None, :, None, :]
    s = jnp.sin(ang)[None, :, None, :]
    xf = x.astype(F32)
    x1, x2, rest = xf[..., :half], xf[..., half:ROT_DIM], xf[..., ROT_DIM:]
    out = jnp.concatenate([x1 * c - x2 * s, x2 * c + x1 * s, rest], axis=-1)
    return out.astype(x.dtype)


def dilated_branch(q, k, v, dil, radius):
    B, L, H, hd = q.shape
    n = L // dil
    blk = radius
    nb = -(-n // blk)
    n_pad = nb * blk

    def split(t):
        return t.reshape(B, n, dil, H, hd).transpose(0, 2, 3, 1, 4).astype(F32)

    qs, ks, vs = split(q) * (1.0 / math.sqrt(hd)), split(k), split(v)
    qb = jnp.pad(qs, ((0, 0), (0, 0), (0, 0), (0, n_pad - n), (0, 0))).reshape(B, dil, H, nb, blk, hd)

    def windows(t):
        tp = jnp.pad(t, ((0, 0), (0, 0), (0, 0), (blk, n_pad - n + blk), (0, 0)))
        tp = tp.reshape(B, dil, H, nb + 2, blk, hd)
        return jnp.concatenate([tp[:, :, :, :-2], tp[:, :, :, 1:-1], tp[:, :, :, 2:]], axis=4)

    kw, vw = windows(ks), windows(vs)
    qpos = jnp.arange(nb)[:, None] * blk + jnp.arange(blk)[None, :]
    kpos = (jnp.arange(nb)[:, None] - 1) * blk + jnp.arange(3 * blk)[None, :]
    rel = kpos[:, None, :] - qpos[:, :, None]
    mask = (jnp.abs(rel) <= radius) & (kpos[:, None, :] >= 0) & (kpos[:, None, :] < n)
    s = jnp.einsum('bghnid,bghnjd->bghnij', qb, kw)
    s = jnp.where(mask, s, NEG)
    m = jnp.max(s, axis=-1, keepdims=True)
    p = jnp.exp(s - m)
    l = jnp.sum(p, axis=-1, keepdims=True)
    o = jnp.einsum('bghnij,bghnjd->bghnid', p, vw) / l
    lse = (m + jnp.log(l))[..., 0]
    o = o.reshape(B, dil, H, n_pad, hd)[:, :, :, :n].transpose(0, 3, 1, 2, 4).reshape(B, L, H, hd)
    lse = lse.reshape(B, dil, H, n_pad)[:, :, :, :n].transpose(0, 3, 1, 2).reshape(B, L, H)
    return o, lse


def layer(x, mem, lp):
    (g_pre_mix, w_in, conv_w, conv_b, f_w1, f_b1, f_w2, f_b2, f_w3, f_b3, f_w4, f_freq,
     f_bias, g_mem, w_mem_kv, g_grp, w_out, g_post_mix, g_pre_ffn, w_up, ffn_conv_w,
     ffn_conv_b, w_down, g_post_ffn) = lp
    B, L, _ = x.shape
    dt = x.dtype

    h = rmsnorm(x, g_pre_mix)
    proj = h @ w_in
    hy, qkv_a, q_m = jnp.split(proj, [3 * D_HYENA, 3 * D_HYENA + 3 * D_ATTN], axis=-1)

    hy = dwconv3(hy, conv_w, conv_b)
    v0, x1, x2 = jnp.split(hy, 3, axis=-1)
    filt = hyena_filters(L, f_w1, f_b1, f_w2, f_b2, f_w3, f_b3, f_w4, f_freq)
    z = x1 * long_conv(v0, filt[0, 0], filt[0, 1], f_bias[0])
    y_h = (x2 * long_conv(z, filt[1, 0], filt[1, 1], f_bias[1])).astype(dt)

    q_a, k_a, v_a = [t.reshape(B, L, N_ATTN_HEADS, HEAD_DIM) for t in jnp.split(qkv_a, 3, axis=-1)]
    q_a, k_a = rope_partial(q_a), rope_partial(k_a)
    outs, lses = [], []
    for window, dil in DILATED_PATTERNS:
        o, lse = dilated_branch(q_a, k_a, v_a, dil, window // (2 * dil))
        outs.append(o)
        lses.append(lse)
    wts = jax.nn.softmax(jnp.stack(lses, axis=0), axis=0)
    y_a = jnp.sum(wts[..., None] * jnp.stack(outs, axis=0), axis=0).reshape(B, L, D_ATTN).astype(dt)

    mh = rmsnorm(mem, g_mem)
    kv = (mh @ w_mem_kv).reshape(B, mem.shape[1], 2, N_MEM_HEADS, HEAD_DIM)
    km, vm = kv[:, :, 0].astype(F32), kv[:, :, 1].astype(F32)
    qm = q_m.reshape(B, L, N_MEM_HEADS, HEAD_DIM).astype(F32)
    sm = jnp.einsum('blhd,bmhd->bhlm', qm, km) * (1.0 / math.sqrt(HEAD_DIM))
    pm = jax.nn.softmax(sm, axis=-1)
    y_m = jnp.einsum('bhlm,bmhd->blhd', pm, vm).reshape(B, L, D_MEMX).astype(dt)

    mix = group_rmsnorm(jnp.concatenate([y_h, y_a, y_m], axis=-1), g_grp)
    x = x + rmsnorm(mix @ w_out, g_post_mix).astype(dt)

    hf = rmsnorm(x, g_pre_ffn)
    gate, val = jnp.split(hf @ w_up, 2, axis=-1)
    gate = dwconv3(gate, ffn_conv_w, ffn_conv_b)
    ff = jax.nn.gelu(gate, approximate=True) * val
    x = x + rmsnorm(ff @ w_down, g_post_ffn).astype(dt)
    return x


def setup_inputs(seed: int = 0) -> dict:
    key = jax.random.key(seed)
    ks = jax.random.split(key, 32)

    def nrm(k, shape, scale):
        return jax.random.normal(k, shape, F32) * scale

    def gain(k, n):
        return 1.0 + nrm(k, (DEPTH, n), 0.02)

    D = D_MODEL
    return {
        "x_prompt": nrm(ks[0], (BATCH, SEQ, D), 1.0),
        "x_sample": nrm(ks[1], (DEC_BATCH, DEC_SEQ, D), 1.0),
        "mem_prompt": nrm(ks[2], (BATCH, N_MEM, D), 1.0),
        "mem_sample": nrm(ks[3], (DEC_BATCH, N_MEM, D), 1.0),
        "g_pre_mix": gain(ks[4], D),
        "w_in": nrm(ks[5], (DEPTH, D, D_IN), D ** -0.5),
        "conv_w": nrm(ks[6], (DEPTH, CONV_WIDTH, 3 * D_HYENA), CONV_WIDTH ** -0.5),
        "conv_b": nrm(ks[7], (DEPTH, 3 * D_HYENA), 0.01),
        "f_w1": nrm(ks[8], (DEPTH, FILTER_EMB, FILTER_HIDDEN), FILTER_EMB ** -0.5),
        "f_b1": nrm(ks[9], (DEPTH, FILTER_HIDDEN), 0.1),
        "f_w2": nrm(ks[10], (DEPTH, FILTER_HIDDEN, FILTER_HIDDEN), FILTER_HIDDEN ** -0.5),
        "f_b2": nrm(ks[11], (DEPTH, FILTER_HIDDEN), 0.1),
        "f_w3": nrm(ks[12], (DEPTH, FILTER_HIDDEN, FILTER_HIDDEN), FILTER_HIDDEN ** -0.5),
        "f_b3": nrm(ks[13], (DEPTH, FILTER_HIDDEN), 0.1),
        "f_w4": nrm(ks[14], (DEPTH, FILTER_HIDDEN, HYENA_ORDER * N_DIRS * D_HYENA), FILTER_HIDDEN ** -0.5),
        "f_freq": 1.0 + nrm(ks[15], (DEPTH, FILTER_HIDDEN), 0.02),
        "f_bias": nrm(ks[16], (DEPTH, HYENA_ORDER, D_HYENA), 0.1),
        "g_mem": gain(ks[17], D),
        "w_mem_kv": nrm(ks[18], (DEPTH, D, 2 * D_MEMX), D ** -0.5),
        "g_grp": gain(ks[19], D),
        "w_out": nrm(ks[20], (DEPTH, D, D), D ** -0.5),
        "g_post_mix": gain(ks[21], D),
        "g_pre_ffn": gain(ks[22], D),
        "w_up": nrm(ks[23], (DEPTH, D, 2 * D_FF), D ** -0.5),
        "ffn_conv_w": nrm(ks[24], (DEPTH, CONV_WIDTH, D_FF), CONV_WIDTH ** -0.5),
        "ffn_conv_b": nrm(ks[25], (DEPTH, D_FF), 0.01),
        "w_down": nrm(ks[26], (DEPTH, D_FF, D), D_FF ** -0.5),
        "g_post_ffn": gain(ks[27], D),
    }


def reference(x_prompt, x_sample, mem_prompt, mem_sample, g_pre_mix, w_in, conv_w, conv_b,
              f_w1, f_b1, f_w2, f_b2, f_w3, f_b3, f_w4, f_freq, f_bias, g_mem, w_mem_kv,
              g_grp, w_out, g_post_mix, g_pre_ffn, w_up, ffn_conv_w, ffn_conv_b, w_down,
              g_post_ffn):
    y_prompt = x_prompt
    y_sample = x_sample
    for i in range(DEPTH):
        lp = (g_pre_mix[i], w_in[i], conv_w[i], conv_b[i], f_w1[i], f_b1[i], f_w2[i], f_b2[i],
              f_w3[i], f_b3[i], f_w4[i], f_freq[i], f_bias[i], g_mem[i], w_mem_kv[i], g_grp[i],
              w_out[i], g_post_mix[i], g_pre_ffn[i], w_up[i], ffn_conv_w[i], ffn_conv_b[i],
              w_down[i], g_post_ffn[i])
        y_prompt = layer(y_prompt, mem_prompt, lp)
        y_sample = layer(y_sample, mem_sample, lp)
    return (y_prompt, y_sample)
```

```python
import functools
import math

import numpy as np
import jax
import jax.numpy as jnp
from jax import lax
from jax.experimental import pallas as pl
from jax.experimental.pallas import tpu as pltpu

F32 = jnp.float32
BF16 = jnp.bfloat16

HEAD_DIM = 128
DILATED_PATTERNS = ((128, 1), (512, 4), (2048, 16))
ROPE_THETA = 500000.0
ROT_DIM = HEAD_DIM // 4
FILTER_EMB = 33
FILTER_HIDDEN = 64
DECAY_FAST = 0.3
DECAY_SLOW = 1.5
DECAY_TARGET = 1e-2
DECAY_SHIFT = 0.05
EPS = 1e-6
NEG = -1e30

HALO = 16
PAD_K1 = 8
VMEM_LIMIT = 56 * 1024 * 1024


def _cparams(sem):
    return pltpu.CompilerParams(dimension_semantics=sem, vmem_limit_bytes=VMEM_LIMIT)


def _split(x):
    hi = x.astype(BF16)
    lo = (x - hi.astype(F32)).astype(BF16)
    return hi, lo


def _dot(a, b):
    return jnp.dot(a, b, preferred_element_type=F32)


def _dot3(a_hi, a_lo, b):
    b_hi, b_lo = _split(b)
    return _dot(a_hi, b_hi) + _dot(a_lo, b_hi) + _dot(a_hi, b_lo)


def _rms(v, g):
    return v * lax.rsqrt(jnp.mean(v * v, axis=-1, keepdims=True) + EPS) * g


def _fill_normed(hb_ref, xp_ref, x_ref, xn_ref, g_ref, i, tm, blocks_per_seq):
    g = g_ref[...]
    pos = i % blocks_per_seq
    keep_p = (pos != 0).astype(F32)
    keep_n = (pos != blocks_per_seq - 1).astype(F32)
    hb_ref[0:HALO, :] = (_rms(xp_ref[...], g) * keep_p).astype(BF16)
    hb_ref[HALO:HALO + tm, :] = _rms(x_ref[...], g).astype(BF16)
    hb_ref[HALO + tm:, :] = (_rms(xn_ref[...], g) * keep_n).astype(BF16)


def _conv3(u, cw, cb, tm):
    rows = tm + 2 * HALO
    up = pltpu.roll(u, 1, 0)[HALO:HALO + tm]
    un = pltpu.roll(u, rows - 1, 0)[HALO:HALO + tm]
    uc = u[HALO:HALO + tm]
    return up * cw[0:1] + uc * cw[1:2] + un * cw[2:3] + cb


def _nm_conv_body(xp_ref, x_ref, xn_ref, g_ref, w_ref, cw_ref, cb_ref, o_ref, hb_ref, *, tm, blocks_per_seq):
    i = pl.program_id(0)

    @pl.when(pl.program_id(1) == 0)
    def _():
        _fill_normed(hb_ref, xp_ref, x_ref, xn_ref, g_ref, i, tm, blocks_per_seq)

    u = _dot(hb_ref[...], w_ref[...])
    o_ref[...] = _conv3(u, cw_ref[...], cb_ref[...], tm)


def _nm_body(x_ref, g_ref, w_ref, o_ref, hb_ref):
    @pl.when(pl.program_id(1) == 0)
    def _():
        hb_ref[...] = _rms(x_ref[...], g_ref[...]).astype(BF16)

    o_ref[...] = _dot(hb_ref[...], w_ref[...])


def _halo_specs(tm, d, n_rows):
    hb = tm // HALO
    last = n_rows // HALO - 1
    return [
        pl.BlockSpec((HALO, d), lambda i, j: (jnp.maximum(i * hb - 1, 0), 0)),
        pl.BlockSpec((tm, d), lambda i, j: (i, 0)),
        pl.BlockSpec((HALO, d), lambda i, j: (jnp.minimum((i + 1) * hb, last), 0)),
    ]


def norm_matmul_conv(x, g, w, cw, cb, seq_len, tm, tn):
    t, d = x.shape
    n = w.shape[1]
    body = functools.partial(_nm_conv_body, tm=tm, blocks_per_seq=seq_len // tm)
    return pl.pallas_call(
        body,
        out_shape=jax.ShapeDtypeStruct((t, n), F32),
        grid=(t // tm, n // tn),
        in_specs=_halo_specs(tm, d, t) + [
            pl.BlockSpec((1, d), lambda i, j: (0, 0)),
            pl.BlockSpec((d, tn), lambda i, j: (0, j)),
            pl.BlockSpec((3, tn), lambda i, j: (0, j)),
            pl.BlockSpec((1, tn), lambda i, j: (0, j)),
        ],
        out_specs=pl.BlockSpec((tm, tn), lambda i, j: (i, j)),
        scratch_shapes=[pltpu.VMEM((tm + 2 * HALO, d), BF16)],
        compiler_params=_cparams(("parallel", "arbitrary")),
        name="norm_matmul_conv",
    )(x, x, x, g, w, cw, cb)


def norm_matmul(x, g, w, tm, tn):
    t, d = x.shape
    n = w.shape[1]
    return pl.pallas_call(
        _nm_body,
        out_shape=jax.ShapeDtypeStruct((t, n), F32),
        grid=(t // tm, n // tn),
        in_specs=[
            pl.BlockSpec((tm, d), lambda i, j: (i, 0)),
            pl.BlockSpec((1, d), lambda i, j: (0, 0)),
            pl.BlockSpec((d, tn), lambda i, j: (0, j)),
        ],
        out_specs=pl.BlockSpec((tm, tn), lambda i, j: (i, j)),
        scratch_shapes=[pltpu.VMEM((tm, d), BF16)],
        compiler_params=_cparams(("parallel", "arbitrary")),
        name="norm_matmul",
    )(x, g, w)


def _ffn_body(xp_ref, x_ref, xn_ref, g_ref, wg_ref, wv_ref, cw_ref, cb_ref, wd_ref, gp_ref, o_ref,
              hb_ref, acc_ref, *, tm, blocks_per_seq, nj):
    i = pl.program_id(0)
    j = pl.program_id(1)

    @pl.when(j == 0)
    def _():
        _fill_normed(hb_ref, xp_ref, x_ref, xn_ref, g_ref, i, tm, blocks_per_seq)
        acc_ref[...] = jnp.zeros_like(acc_ref)

    ug = _dot(hb_ref[...], wg_ref[...])
    val = _dot(hb_ref[HALO:HALO + tm, :], wv_ref[...])
    gate = _conv3(ug, cw_ref[...], cb_ref[...], tm)
    ff = jax.nn.gelu(gate, approximate=True) * val
    acc_ref[...] += _dot(ff.astype(BF16), wd_ref[...])

    @pl.when(j == nj - 1)
    def _():
        o_ref[...] = x_ref[...] + _rms(acc_ref[...], gp_ref[...])


def ffn(x, g_pre, w_up, cw, cb, w_down, g_post, seq_len, tm, tf):
    t, d = x.shape
    d_ff = w_down.shape[0]
    nj = d_ff // tf
    body = functools.partial(_ffn_body, tm=tm, blocks_per_seq=seq_len // tm, nj=nj)
    return pl.pallas_call(
        body,
        out_shape=jax.ShapeDtypeStruct((t, d), F32),
        grid=(t // tm, nj),
        in_specs=_halo_specs(tm, d, t) + [
            pl.BlockSpec((1, d), lambda i, j: (0, 0)),
            pl.BlockSpec((d, tf), lambda i, j: (0, j)),
            pl.BlockSpec((d, tf), lambda i, j: (0, j + nj)),
            pl.BlockSpec((3, tf), lambda i, j: (0, j)),
            pl.BlockSpec((1, tf), lambda i, j: (0, j)),
            pl.BlockSpec((tf, d), lambda i, j: (j, 0)),
            pl.BlockSpec((1, d), lambda i, j: (0, 0)),
        ],
        out_specs=pl.BlockSpec((tm, d), lambda i, j: (i, 0)),
        scratch_shapes=[pltpu.VMEM((tm + 2 * HALO, d), BF16), pltpu.VMEM((tm, d), F32)],
        compiler_params=_cparams(("parallel", "arbitrary")),
        name="ffn",
    )(x, x, x, g_pre, w_up, w_up, cw, cb, w_down, g_post)


def _mix_out_body(yh_ref, ya_ref, ym_ref, gg_ref, w_ref, gp_ref, x_ref, o_ref, *, dh, da):
    gg = gg_ref[...]
    acc = _dot(_rms(yh_ref[...], gg[:, :dh]).astype(BF16), w_ref[0:dh, :])
    acc += _dot(_rms(ya_ref[...], gg[:, dh:dh + da]).astype(BF16), w_ref[dh:dh + da, :])
    acc += _dot(_rms(ym_ref[...], gg[:, dh + da:]).astype(BF16), w_ref[dh + da:, :])
    o_ref[...] = x_ref[...] + _rms(acc, gp_ref[...])


def mix_out(yh, ya, ym, g_grp, w_out, g_post, x, tm):
    t, d = x.shape
    dh, da, dm = yh.shape[1], ya.shape[1], ym.shape[1]
    body = functools.partial(_mix_out_body, dh=dh, da=da)
    return pl.pallas_call(
        body,
        out_shape=jax.ShapeDtypeStruct((t, d), F32),
        grid=(t // tm,),
        in_specs=[
            pl.BlockSpec((tm, dh), lambda i: (i, 0)),
            pl.BlockSpec((tm, da), lambda i: (i, 0)),
            pl.BlockSpec((tm, dm), lambda i: (i, 0)),
            pl.BlockSpec((1, d), lambda i: (0, 0)),
            pl.BlockSpec((d, d), lambda i: (0, 0)),
            pl.BlockSpec((1, d), lambda i: (0, 0)),
            pl.BlockSpec((tm, d), lambda i: (i, 0)),
        ],
        out_specs=pl.BlockSpec((tm, d), lambda i: (i, 0)),
        compiler_params=_cparams(("parallel",)),
        name="mix_out",
    )(yh, ya, ym, g_grp, w_out, g_post, x)


def _rope_tables(seq_len):
    half = ROT_DIM // 2
    inv_freq = np.exp(-math.log(ROPE_THETA) * np.arange(0, ROT_DIM, 2, dtype=np.float64) / ROT_DIM)
    ang = np.arange(seq_len, dtype=np.float64)[:, None] * inv_freq.astype(np.float32).astype(np.float64)[None, :]
    c, s = np.cos(ang), np.sin(ang)
    cos_t = np.ones((seq_len, HEAD_DIM), np.float64)
    sin_t = np.zeros((seq_len, HEAD_DIM), np.float64)
    cos_t[:, :half] = c
    cos_t[:, half:ROT_DIM] = c
    sin_t[:, :half] = -s
    sin_t[:, half:ROT_DIM] = s
    return cos_t.astype(np.float32), sin_t.astype(np.float32)


def _dil_attn_body(q_ref, k_ref, v_ref, cos_ref, sin_ref, o_ref, qs_ref, ks_ref, m_ref, l_ref, *, seq_len):
    half = ROT_DIM // 2
    scale = 1.0 / math.sqrt(HEAD_DIM)
    chunk = min(512, seq_len)

    def rope_chunk(c, carry):
        rows = pl.ds(pl.multiple_of(c * chunk, chunk), chunk)
        cs = cos_ref[rows, :]
        sn = sin_ref[rows, :]
        lane = lax.broadcasted_iota(jnp.int32, (chunk, HEAD_DIM), 1)
        for src, dst, mul in ((q_ref, qs_ref, scale), (k_ref, ks_ref, None)):
            x = src[rows, :]
            partner = jnp.where(lane < half, pltpu.roll(x, HEAD_DIM - half, 1), pltpu.roll(x, half, 1))
            y = x * cs + partner * sn
            dst[rows, :] = y if mul is None else y * mul
        return carry

    lax.fori_loop(0, seq_len // chunk, rope_chunk, 0)

    n_br = len(DILATED_PATTERNS)
    for bi, (window, dil) in enumerate(DILATED_PATTERNS):
        radius = window // (2 * dil)
        n = seq_len // dil
        tq = min(128, n)
        kw = min(tq + 2 * radius, n)
        nblk = n // tq

        def block(idx, carry, bi=bi, dil=dil, radius=radius, n=n, tq=tq, kw=kw, nblk=nblk):
            r = idx // nblk
            q0 = (idx % nblk) * tq
            k0 = jnp.clip(q0 - radius, 0, n - kw)
            if dil == 1:
                qsl = pl.ds(pl.multiple_of(q0, tq), tq)
                ksl = pl.ds(pl.multiple_of(k0, 8), kw)
            else:
                qsl = pl.ds(r + q0 * dil, tq, stride=dil)
                ksl = pl.ds(r + k0 * dil, kw, stride=dil)
            qb = qs_ref[qsl, :].astype(BF16)
            kb = ks_ref[ksl, :].astype(BF16)
            vb = v_ref[ksl, :].astype(BF16)
            s = lax.dot_general(qb, kb, (((1,), (1,)), ((), ())), preferred_element_type=F32)
            rel = (k0 + lax.broadcasted_iota(jnp.int32, (tq, kw), 1)) - (q0 + lax.broadcasted_iota(jnp.int32, (tq, kw), 0))
            s = jnp.where(jnp.abs(rel) <= radius, s, NEG)
            m = jnp.max(s, axis=-1, keepdims=True)
            p = jnp.exp(s - m)
            l = jnp.sum(p, axis=-1, keepdims=True)
            acc = _dot(p.astype(BF16), vb)
            m = jnp.broadcast_to(m, (tq, HEAD_DIM))
            l = jnp.broadcast_to(l, (tq, HEAD_DIM))
            if bi > 0:
                m_old = m_ref[qsl, :]
                m_new = jnp.maximum(m_old, m)
                a_old = jnp.exp(m_old - m_new)
                a_new = jnp.exp(m - m_new)
                acc = o_ref[qsl, :] * a_old + acc * a_new
                l = l_ref[qsl, :] * a_old + l * a_new
                m = m_new
            if bi == n_br - 1:
                o_ref[qsl, :] = acc / l
            else:
                o_ref[qsl, :] = acc
                m_ref[qsl, :] = m
                l_ref[qsl, :] = l
            return carry

        lax.fori_loop(0, dil * nblk, block, 0)


def dil_attn(proj, batch, seq_len, q_col, k_col, v_col, n_heads):
    cos_t, sin_t = _rope_tables(seq_len)
    nb = 1 if seq_len * HEAD_DIM * 4 > (2 << 20) else 2

    def col_spec(col):
        return pl.BlockSpec((seq_len, HEAD_DIM), lambda b, h: (b, col + h), pipeline_mode=pl.Buffered(nb))

    tab_spec = pl.BlockSpec((seq_len, HEAD_DIM), lambda b, h: (0, 0), pipeline_mode=pl.Buffered(1))
    body = functools.partial(_dil_attn_body, seq_len=seq_len)
    return pl.pallas_call(
        body,
        out_shape=jax.ShapeDtypeStruct((batch * seq_len, n_heads * HEAD_DIM), F32),
        grid=(batch, n_heads),
        in_specs=[col_spec(q_col), col_spec(k_col), col_spec(v_col), tab_spec, tab_spec],
        out_specs=pl.BlockSpec((seq_len, HEAD_DIM), lambda b, h: (b, h), pipeline_mode=pl.Buffered(nb)),
        scratch_shapes=[pltpu.VMEM((seq_len, HEAD_DIM), F32)] * 4,
        compiler_params=_cparams(("parallel", "parallel")),
        name="dil_attn",
    )(proj, proj, proj, cos_t, sin_t)


def _mem_attn_body(q_ref, k_ref, v_ref, o_ref):
    s = lax.dot_general(q_ref[...].astype(BF16), k_ref[...].astype(BF16), (((1,), (1,)), ((), ())),
                        preferred_element_type=F32) * (1.0 / math.sqrt(HEAD_DIM))
    m = jnp.max(s, axis=-1, keepdims=True)
    p = jnp.exp(s - m)
    l = jnp.sum(p, axis=-1, keepdims=True)
    o_ref[...] = _dot(p.astype(BF16), v_ref[...].astype(BF16)) / l


def mem_attn(proj, kv, batch, seq_len, n_mem, q_col, n_heads, tq):
    nq = seq_len // tq
    return pl.pallas_call(
        _mem_attn_body,
        out_shape=jax.ShapeDtypeStruct((batch * seq_len, n_heads * HEAD_DIM), F32),
        grid=(batch, nq, n_heads),
        in_specs=[
            pl.BlockSpec((tq, HEAD_DIM), lambda b, i, h: (b * nq + i, q_col + h)),
            pl.BlockSpec((n_mem, HEAD_DIM), lambda b, i, h: (b, h)),
            pl.BlockSpec((n_mem, HEAD_DIM), lambda b, i, h: (b, n_heads + h)),
        ],
        out_specs=pl.BlockSpec((tq, HEAD_DIM), lambda b, i, h: (b * nq + i, h)),
        compiler_params=_cparams(("parallel", "parallel", "parallel")),
        name="mem_attn",
    )(proj, kv, kv)


def _fft_split(seq_len):
    n = 2 * seq_len
    n2 = 128
    return n // n2, n2


def _np_split(x):
    x32 = np.asarray(x, np.float32)
    hi = x32.astype(BF16)
    lo = (x32 - hi.astype(np.float32)).astype(BF16)
    return hi, lo


@functools.lru_cache(maxsize=None)
def _fft_consts(seq_len):
    n1, n2 = _fft_split(seq_len)
    n = n1 * n2
    kh = n1 // 2 + 1
    k1p = n1 // 2 + PAD_K1
    a = np.arange(n1)
    k1 = np.arange(k1p)
    live = (k1 < kh).astype(np.float64)
    th = 2.0 * np.pi * ((k1[:, None] * a[None, :]) % n1) / n1
    a_fwd = np.concatenate([np.cos(th) * live[:, None], -np.sin(th) * live[:, None]], axis=0)
    alpha = np.where((k1 == 0) | (k1 == n1 // 2), 1.0, 2.0) * live / n
    a_out = n1 // 2
    th_i = th[:, :a_out].T
    a_inv_c = np.cos(th_i) * alpha[None, :]
    a_inv_s = -np.sin(th_i) * alpha[None, :]
    b = np.arange(n2)
    k2 = np.arange(n2)
    kk = k1[:, None, None] + n1 * k2[None, :, None]
    ph = 2.0 * np.pi * ((kk * b[None, None, :]) % n) / n
    wr, wi = np.cos(ph), -np.sin(ph)
    vr, vi = np.transpose(wr, (0, 2, 1)), -np.transpose(wi, (0, 2, 1))

    def stack(re, im):
        re_h, re_l = _np_split(re)
        im_h, im_l = _np_split(im)
        return np.concatenate([re_h, im_h, re_l, im_l], axis=1)

    return dict(n1=n1, n2=n2, k1p=k1p, a_out=a_out,
                a_fwd=_np_split(a_fwd), a_fwd_half=_np_split(a_fwd[:, :a_out]),
                a_inv_c=_np_split(a_inv_c), a_inv_s=_np_split(a_inv_s),
                w_fwd=stack(wr, wi), w_inv=stack(vr, vi))


def _filter_tables(seq_len, channels):
    n = 2 * seq_len
    bands = (FILTER_EMB - 1) // 2
    pos = np.concatenate([np.arange(seq_len), [0], np.arange(seq_len - 1, 0, -1)])
    t = np.linspace(0.0, 1.0, seq_len)[pos]
    w = 2.0 * np.pi * pos / seq_len
    f = np.linspace(1e-4, bands - 1, bands)
    ang = w[:, None] * f[None, :]
    z = np.zeros((n, FILTER_HIDDEN), np.float64)
    z[:, 0] = t
    z[:, 1:1 + bands] = np.cos(ang)
    z[:, 1 + bands:1 + 2 * bands] = -np.sin(ang)
    deltas = np.abs(np.linspace(math.log(DECAY_TARGET) / DECAY_SLOW, math.log(DECAY_TARGET) / DECAY_FAST, channels))
    return z.astype(np.float32), deltas[None, :].astype(np.float32)


def _filter_body(z_ref, w1_ref, b1_ref, w2_ref, b2_ref, w3_ref, b3_ref, fr_ref, w4_ref, dl_ref,
                 k_ref, s_ref, *, tr, seq_len, channels):
    i = pl.program_id(0)
    fr = fr_ref[...]
    z = z_ref[...]

    def layer(h, w_ref, b_ref):
        w_hi, w_lo = _split(w_ref[...])
        h_hi, h_lo = _split(h)
        pre = _dot(h_hi, w_hi) + _dot(h_lo, w_hi) + _dot(h_hi, w_lo)
        return jnp.sin(fr * (pre + b_ref[...]))

    h = layer(z, w1_ref, b1_ref)
    h = layer(h, w2_ref, b2_ref)
    h = layer(h, w3_ref, b3_ref)
    w_hi, w_lo = _split(w4_ref[...])
    h_hi, h_lo = _split(h)
    k = _dot(h_hi, w_hi) + _dot(h_lo, w_hi) + _dot(h_hi, w_lo)
    decay = jnp.exp(-z[:, 0:1] * dl_ref[...]) + DECAY_SHIFT
    row = i * tr + lax.broadcasted_iota(jnp.int32, (tr, 1), 0)
    decay = jnp.where(row == seq_len, 0.0, decay)
    k = k * jnp.concatenate([decay, decay], axis=1)
    k_ref[...] = k

    @pl.when(i == 0)
    def _():
        s_ref[...] = jnp.zeros_like(s_ref)

    s_ref[...] += jnp.sum(jnp.abs(k).reshape(tr // 8, 8, 2 * channels), axis=0)


def filter_gen(seq_len, channels, w1p, b1, w2, b2, w3, b3, freq, w4d):
    n = 2 * seq_len
    tr = min(512, seq_len)
    z, deltas = _filter_tables(seq_len, channels)
    hid = FILTER_HIDDEN
    full = lambda shape: pl.BlockSpec(shape, lambda i: (0,) * len(shape))
    body = functools.partial(_filter_body, tr=tr, seq_len=seq_len, channels=channels)
    return pl.pallas_call(
        body,
        out_shape=(jax.ShapeDtypeStruct((n, 2 * channels), F32), jax.ShapeDtypeStruct((8, 2 * channels), F32)),
        grid=(n // tr,),
        in_specs=[
            pl.BlockSpec((tr, hid), lambda i: (i, 0)),
            full((hid, hid)), full((1, hid)), full((hid, hid)), full((1, hid)), full((hid, hid)), full((1, hid)),
            full((1, hid)),
            pl.BlockSpec((None, hid, 2 * channels), lambda i: ((i * tr) // seq_len, 0, 0)),
            full((1, channels)),
        ],
        out_specs=(pl.BlockSpec((tr, 2 * channels), lambda i: (i, 0)), full((8, 2 * channels))),
        compiler_params=_cparams(("arbitrary",)),
        name="filter_gen",
    )(z, w1p, b1, w2, b2, w3, b3, freq, w4d, deltas)


def _major_fwd_body(u_ref, ah_ref, al_ref, yr_ref, yi_ref, *, k1p):
    y = _dot3(ah_ref[...], al_ref[...], u_ref[...])
    yr_ref[...] = y[:k1p]
    yi_ref[...] = y[k1p:]


def major_fwd(u, a_hi, a_lo, k1p, tn):
    bsz, a_in, n = u.shape
    body = functools.partial(_major_fwd_body, k1p=k1p)
    out = jax.ShapeDtypeStruct((bsz, k1p, n), F32)
    return pl.pallas_call(
        body,
        out_shape=(out, out),
        grid=(bsz, n // tn),
        in_specs=[
            pl.BlockSpec((None, a_in, tn), lambda b, j: (b, 0, j)),
            pl.BlockSpec((2 * k1p, a_in), lambda b, j: (0, 0)),
            pl.BlockSpec((2 * k1p, a_in), lambda b, j: (0, 0)),
        ],
        out_specs=(pl.BlockSpec((None, k1p, tn), lambda b, j: (b, 0, j)),) * 2,
        compiler_params=_cparams(("parallel", "parallel")),
        name="major_fwd",
    )(u, a_hi, a_lo)


def _major_inv_body(gr_ref, gi_ref, ch_ref, cl_ref, sh_ref, sl_ref, m_ref, o_ref):
    y = _dot3(ch_ref[...], cl_ref[...], gr_ref[...]) + _dot3(sh_ref[...], sl_ref[...], gi_ref[...])
    o_ref[...] = y * m_ref[...]


def major_inv(gr, gi, a_c, a_s, mult, tn):
    bsz, k1p, n = gr.shape
    a_out = mult.shape[1]
    g_spec = pl.BlockSpec((None, k1p, tn), lambda b, j: (b, 0, j))
    a_spec = pl.BlockSpec((a_out, k1p), lambda b, j: (0, 0))
    o_spec = pl.BlockSpec((None, a_out, tn), lambda b, j: (b, 0, j))
    return pl.pallas_call(
        _major_inv_body,
        out_shape=jax.ShapeDtypeStruct((bsz, a_out, n), F32),
        grid=(bsz, n // tn),
        in_specs=[g_spec, g_spec, a_spec, a_spec, a_spec, a_spec, o_spec],
        out_specs=o_spec,
        compiler_params=_cparams(("parallel", "parallel")),
        name="major_inv",
    )(gr, gi, a_c[0], a_c[1], a_s[0], a_s[1], mult)


def _cplx_apply(ws_ref, xr, xi, n2):
    ws = ws_ref[...]

    def one(x):
        x_hi, x_lo = _split(x)
        p = _dot(ws, x_hi)
        q = _dot(ws[:2 * n2], x_lo)
        re = p[0:n2] + p[2 * n2:3 * n2] + q[0:n2]
        im = p[n2:2 * n2] + p[3 * n2:4 * n2] + q[n2:2 * n2]
        return re, im

    rr, ir = one(xr)
    ri, ii = one(xi)
    return rr - ii, ri + ir


def _minor_conv_body(yr_ref, yi_ref, wf_ref, wi_ref, kr_ref, ki_ref, gr_ref, gi_ref, *, n2):
    zr, zi = _cplx_apply(wf_ref, yr_ref[...], yi_ref[...], n2)
    kr = kr_ref[...]
    ki = ki_ref[...]
    pr = zr * kr - zi * ki
    pi = zr * ki + zi * kr
    gr, gi = _cplx_apply(wi_ref, pr, pi, n2)
    gr_ref[...] = gr
    gi_ref[...] = gi


def minor_conv(yr, yi, w_fwd, w_inv, kr, ki, order, ct):
    bsz, k1p, n2, c = yr.shape
    nc = c // ct
    y_spec = pl.BlockSpec((None, None, n2, ct), lambda k, j, b: (b, k, 0, j))
    w_spec = pl.BlockSpec((None, 4 * n2, n2), lambda k, j, b: (k, 0, 0))
    k_spec = pl.BlockSpec((None, n2, ct), lambda k, j, b: (k, 0, order * nc + j))
    body = functools.partial(_minor_conv_body, n2=n2)
    out = jax.ShapeDtypeStruct((bsz, k1p, n2, c), F32)
    return pl.pallas_call(
        body,
        out_shape=(out, out),
        grid=(k1p, nc, bsz),
        in_specs=[y_spec, y_spec, w_spec, w_spec, k_spec, k_spec],
        out_specs=(y_spec, y_spec),
        compiler_params=_cparams(("parallel", "parallel", "parallel")),
        name="minor_conv",
    )(yr, yi, w_fwd, w_inv, kr, ki)


def _minor_filter_body(yr_ref, yi_ref, wf_ref, inv_ref, bias_ref, kr_ref, ki_ref, *, n2):
    zr, zi = _cplx_apply(wf_ref, yr_ref[...], yi_ref[...], n2)
    inv = inv_ref[...]
    kr_ref[...] = zr * inv + bias_ref[...]
    ki_ref[...] = zi * inv


def minor_filter(yr, yi, w_fwd, inv_norm, bias, ct):
    k1p, n2, c2 = yr.shape
    y_spec = pl.BlockSpec((None, n2, ct), lambda k, j: (k, 0, j))
    v_spec = pl.BlockSpec((1, ct), lambda k, j: (0, j))
    body = functools.partial(_minor_filter_body, n2=n2)
    out = jax.ShapeDtypeStruct((k1p, n2, c2), F32)
    return pl.pallas_call(
        body,
        out_shape=(out, out),
        grid=(k1p, c2 // ct),
        in_specs=[y_spec, y_spec, pl.BlockSpec((None, 4 * n2, n2), lambda k, j: (k, 0, 0)), v_spec, v_spec],
        out_specs=(y_spec, y_spec),
        compiler_params=_cparams(("parallel", "parallel")),
        name="minor_filter",
    )(yr, yi, w_fwd, inv_norm, bias)


def hyena_filter_spectra(seq_len, channels, fw, f_bias):
    cst = _fft_consts(seq_len)
    n1, n2, k1p = cst["n1"], cst["n2"], cst["k1p"]
    kern, sums = filter_gen(seq_len, channels, *fw)
    inv_norm = 1.0 / jnp.sum(sums, axis=0, keepdims=True)
    u = kern.reshape(1, n1, n2 * 2 * channels)
    yr, yi = major_fwd(u, *cst["a_fwd"], k1p, tn=4096)
    yr = yr.reshape(k1p, n2, 2 * channels)
    yi = yi.reshape(k1p, n2, 2 * channels)
    return minor_filter(yr, yi, cst["w_fwd"], inv_norm, f_bias.reshape(1, 2 * channels), ct=512)


def long_conv_gated(u_r, mult_r, spectra, order, seq_len, channels):
    cst = _fft_consts(seq_len)
    n2, k1p = cst["n2"], cst["k1p"]
    bsz = u_r.shape[0]
    yr, yi = major_fwd(u_r, *cst["a_fwd_half"], k1p, tn=4096)
    yr = yr.reshape(bsz, k1p, n2, channels)
    yi = yi.reshape(bsz, k1p, n2, channels)
    gr, gi = minor_conv(yr, yi, cst["w_fwd"], cst["w_inv"], spectra[0], spectra[1], order, ct=512)
    gr = gr.reshape(bsz, k1p, n2 * channels)
    gi = gi.reshape(bsz, k1p, n2 * channels)
    return major_inv(gr, gi, cst["a_inv_c"], cst["a_inv_s"], mult_r, tn=4096)


def _row_tile(seq_len, want):
    return min(want, seq_len)


def _layer(x, mem, lw, batch, seq_len, dims):
    d, dh, da, dm = dims
    n_heads = da // HEAD_DIM
    n_mem = mem.shape[0] // batch
    tm = _row_tile(seq_len, 512)

    proj = norm_matmul_conv(x, lw["g_pre_mix"], lw["w_in"], lw["conv_w"], lw["conv_b"], seq_len, tm, tn=512)

    cst = _fft_consts(seq_len)
    a_h, n2 = cst["a_out"], cst["n2"]
    p4 = proj.reshape(batch, a_h, n2, proj.shape[1])
    v0, x1, x2 = [p4[..., s * dh:(s + 1) * dh].reshape(batch, a_h, n2 * dh) for s in range(3)]
    spectra = lw["spectra"][seq_len]
    z = long_conv_gated(v0, x1, spectra, 0, seq_len, dh)
    y_h = long_conv_gated(z, x2, spectra, 1, seq_len, dh).reshape(batch * seq_len, dh)

    qcol = 3 * dh // HEAD_DIM
    y_a = dil_attn(proj, batch, seq_len, qcol, qcol + n_heads, qcol + 2 * n_heads, n_heads)
    kv = norm_matmul(mem, lw["g_mem"], lw["w_mem_kv"], tm=n_mem, tn=512)
    y_m = mem_attn(proj, kv, batch, seq_len, n_mem, qcol + 3 * n_heads, dm // HEAD_DIM, tq=_row_tile(seq_len, 1024))

    x = mix_out(y_h, y_a, y_m, lw["g_grp"], lw["w_out"], lw["g_post_mix"], x, tm)
    return ffn(x, lw["g_pre_ffn"], lw["w_up"], lw["ffn_conv_w"], lw["ffn_conv_b"], lw["w_down"],
               lw["g_post_ffn"], seq_len, tm, tf=512)


def kernel(x_prompt, x_sample, mem_prompt, mem_sample, g_pre_mix, w_in, conv_w, conv_b, f_w1, f_b1, f_w2, f_b2, f_w3, f_b3, f_w4, f_freq, f_bias, g_mem, w_mem_kv, g_grp, w_out, g_post_mix, g_pre_ffn, w_up, ffn_conv_w, ffn_conv_b, w_down, g_post_ffn):
    depth, d, d_in = w_in.shape
    dh = conv_w.shape[2] // 3
    dm = w_mem_kv.shape[2] // 2
    da = (d_in - 3 * dh - dm) // 3
    dims = (d, dh, da, dm)
    groups = [(x_prompt, mem_prompt), (x_sample, mem_sample)]
    seq_lens = sorted({g[0].shape[1] for g in groups})

    xs = [g[0].reshape(-1, d) for g in groups]
    mems = [g[1].reshape(-1, d) for g in groups]
    hid = FILTER_HIDDEN
    for i in range(depth):
        row = lambda v: v[i][None, :]
        cw = jnp.zeros((3, d_in), F32).at[1].set(1.0).at[:, :3 * dh].set(conv_w[i])
        cb = jnp.zeros((1, d_in), F32).at[0, :3 * dh].set(conv_b[i])
        w1p = jnp.zeros((hid, hid), F32).at[:FILTER_EMB].set(f_w1[i])
        w4d = f_w4[i].reshape(hid, 2, 2, dh).transpose(2, 0, 1, 3).reshape(2, hid, 2 * dh)
        fw = (w1p, row(f_b1), f_w2[i], row(f_b2), f_w3[i], row(f_b3), row(f_freq), w4d)
        lw = dict(
            g_pre_mix=row(g_pre_mix), w_in=w_in[i].astype(BF16), conv_w=cw, conv_b=cb,
            g_mem=row(g_mem), w_mem_kv=w_mem_kv[i].astype(BF16), g_grp=row(g_grp),
            w_out=w_out[i].astype(BF16), g_post_mix=row(g_post_mix), g_pre_ffn=row(g_pre_ffn),
            w_up=w_up[i].astype(BF16), ffn_conv_w=ffn_conv_w[i], ffn_conv_b=row(ffn_conv_b),
            w_down=w_down[i].astype(BF16), g_post_ffn=row(g_post_ffn),
            spectra={sl: hyena_filter_spectra(sl, dh, fw, f_bias[i]) for sl in seq_lens},
        )
        for gi, (xg, _) in enumerate(groups):
            xs[gi] = _layer(xs[gi], mems[gi], lw, xg.shape[0], xg.shape[1], dims)
    return tuple(x.reshape(g[0].shape) for x, g in zip(xs, groups))
```

```python
import functools
import math

import numpy as np
import jax
import jax.numpy as jnp
from jax import lax
from jax.experimental import pallas as pl
from jax.experimental.pallas import tpu as pltpu

F32 = jnp.float32
BF16 = jnp.bfloat16

HEAD_DIM = 128
DILATED_PATTERNS = ((128, 1), (512, 4), (2048, 16))
ROPE_THETA = 500000.0
ROT_DIM = HEAD_DIM // 4
FILTER_EMB = 33
FILTER_HIDDEN = 64
DECAY_FAST = 0.3
DECAY_SLOW = 1.5
DECAY_TARGET = 1e-2
DECAY_SHIFT = 0.05
EPS = 1e-6
NEG = -1e30

HALO = 16
ATTN_BLOCKS_PER_STEP = 4
SUB = 8
VMEM_LIMIT = 56 * 1024 * 1024


def _cparams(sem):
    return pltpu.CompilerParams(dimension_semantics=sem, vmem_limit_bytes=VMEM_LIMIT)


def _split(x):
    hi = x.astype(BF16)
    lo = (x - hi.astype(F32)).astype(BF16)
    return hi, lo


def _dot(a, b):
    return jnp.dot(a, b, preferred_element_type=F32)


def _dot3(a_hi, a_lo, b):
    b_hi, b_lo = _split(b)
    return _dot(a_hi, b_hi) + _dot(a_lo, b_hi) + _dot(a_hi, b_lo)


def _rms(v, g):
    return v * lax.rsqrt(jnp.mean(v * v, axis=-1, keepdims=True) + EPS) * g


def _fill_normed(hb_ref, xp_ref, x_ref, xn_ref, g_ref, i, tm, blocks_per_seq):
    g = g_ref[...]
    pos = i % blocks_per_seq
    keep_p = (pos != 0).astype(F32)
    keep_n = (pos != blocks_per_seq - 1).astype(F32)
    hb_ref[0:HALO, :] = (_rms(xp_ref[...], g) * keep_p).astype(BF16)
    hb_ref[HALO:HALO + tm, :] = _rms(x_ref[...], g).astype(BF16)
    hb_ref[HALO + tm:, :] = (_rms(xn_ref[...], g) * keep_n).astype(BF16)


def _conv3(u, cw, cb, tm):
    rows = tm + 2 * HALO
    up = pltpu.roll(u, 1, 0)[HALO:HALO + tm]
    un = pltpu.roll(u, rows - 1, 0)[HALO:HALO + tm]
    uc = u[HALO:HALO + tm]
    return up * cw[0:1] + uc * cw[1:2] + un * cw[2:3] + cb


def _nm_conv_body(xp_ref, x_ref, xn_ref, g_ref, w_ref, cw_ref, cb_ref, o_ref, hb_ref, *, tm, blocks_per_seq):
    i = pl.program_id(0)

    @pl.when(pl.program_id(1) == 0)
    def _():
        _fill_normed(hb_ref, xp_ref, x_ref, xn_ref, g_ref, i, tm, blocks_per_seq)

    u = _dot(hb_ref[...], w_ref[...])
    o_ref[...] = _conv3(u, cw_ref[...], cb_ref[...], tm)


def _nm_body(x_ref, g_ref, w_ref, o_ref, hb_ref):
    @pl.when(pl.program_id(1) == 0)
    def _():
        hb_ref[...] = _rms(x_ref[...], g_ref[...]).astype(BF16)

    o_ref[...] = _dot(hb_ref[...], w_ref[...])


def _halo_specs(tm, d, n_rows):
    hb = tm // HALO
    last = n_rows // HALO - 1
    return [
        pl.BlockSpec((HALO, d), lambda i, j: (jnp.maximum(i * hb - 1, 0), 0)),
        pl.BlockSpec((tm, d), lambda i, j: (i, 0)),
        pl.BlockSpec((HALO, d), lambda i, j: (jnp.minimum((i + 1) * hb, last), 0)),
    ]


def norm_matmul_conv(x, g, w, cw, cb, seq_len, tm, tn):
    t, d = x.shape
    n = w.shape[1]
    body = functools.partial(_nm_conv_body, tm=tm, blocks_per_seq=seq_len // tm)
    return pl.pallas_call(
        body,
        out_shape=jax.ShapeDtypeStruct((t, n), F32),
        grid=(t // tm, n // tn),
        in_specs=_halo_specs(tm, d, t) + [
            pl.BlockSpec((1, d), lambda i, j: (0, 0)),
            pl.BlockSpec((d, tn), lambda i, j: (0, j)),
            pl.BlockSpec((3, tn), lambda i, j: (0, j)),
            pl.BlockSpec((1, tn), lambda i, j: (0, j)),
        ],
        out_specs=pl.BlockSpec((tm, tn), lambda i, j: (i, j)),
        scratch_shapes=[pltpu.VMEM((tm + 2 * HALO, d), BF16)],
        compiler_params=_cparams(("parallel", "arbitrary")),
        name="norm_matmul_conv",
    )(x, x, x, g, w, cw, cb)


def norm_matmul(x, g, w, tm, tn):
    t, d = x.shape
    n = w.shape[1]
    return pl.pallas_call(
        _nm_body,
        out_shape=jax.ShapeDtypeStruct((t, n), F32),
        grid=(t // tm, n // tn),
        in_specs=[
            pl.BlockSpec((tm, d), lambda i, j: (i, 0)),
            pl.BlockSpec((1, d), lambda i, j: (0, 0)),
            pl.BlockSpec((d, tn), lambda i, j: (0, j)),
        ],
        out_specs=pl.BlockSpec((tm, tn), lambda i, j: (i, j)),
        scratch_shapes=[pltpu.VMEM((tm, d), BF16)],
        compiler_params=_cparams(("parallel", "arbitrary")),
        name="norm_matmul",
    )(x, g, w)


def _ffn_body(xp_ref, x_ref, xn_ref, g_ref, wg_ref, wv_ref, cw_ref, cb_ref, wd_ref, gp_ref, o_ref,
              hb_ref, acc_ref, *, tm, blocks_per_seq, nj):
    i = pl.program_id(0)
    j = pl.program_id(1)

    @pl.when(j == 0)
    def _():
        _fill_normed(hb_ref, xp_ref, x_ref, xn_ref, g_ref, i, tm, blocks_per_seq)
        acc_ref[...] = jnp.zeros_like(acc_ref)

    ug = _dot(hb_ref[...], wg_ref[...])
    val = _dot(hb_ref[HALO:HALO + tm, :], wv_ref[...])
    gate = _conv3(ug, cw_ref[...], cb_ref[...], tm)
    ff = jax.nn.gelu(gate, approximate=True) * val
    acc_ref[...] += _dot(ff.astype(BF16), wd_ref[...])

    @pl.when(j == nj - 1)
    def _():
        o_ref[...] = x_ref[...] + _rms(acc_ref[...], gp_ref[...])


def ffn(x, g_pre, w_up, cw, cb, w_down, g_post, seq_len, tm, tf):
    t, d = x.shape
    d_ff = w_down.shape[0]
    nj = d_ff // tf
    body = functools.partial(_ffn_body, tm=tm, blocks_per_seq=seq_len // tm, nj=nj)
    return pl.pallas_call(
        body,
        out_shape=jax.ShapeDtypeStruct((t, d), F32),
        grid=(t // tm, nj),
        in_specs=_halo_specs(tm, d, t) + [
            pl.BlockSpec((1, d), lambda i, j: (0, 0)),
            pl.BlockSpec((d, tf), lambda i, j: (0, j)),
            pl.BlockSpec((d, tf), lambda i, j: (0, j + nj)),
            pl.BlockSpec((3, tf), lambda i, j: (0, j)),
            pl.BlockSpec((1, tf), lambda i, j: (0, j)),
            pl.BlockSpec((tf, d), lambda i, j: (j, 0)),
            pl.BlockSpec((1, d), lambda i, j: (0, 0)),
        ],
        out_specs=pl.BlockSpec((tm, d), lambda i, j: (i, 0)),
        scratch_shapes=[pltpu.VMEM((tm + 2 * HALO, d), BF16), pltpu.VMEM((tm, d), F32)],
        compiler_params=_cparams(("parallel", "arbitrary")),
        name="ffn",
    )(x, x, x, g_pre, w_up, w_up, cw, cb, w_down, g_post)


def _mix_out_body(yh_ref, ya_ref, ym_ref, gg_ref, w_ref, gp_ref, x_ref, o_ref, *, dh, da):
    gg = gg_ref[...]
    acc = _dot(_rms(yh_ref[...], gg[:, :dh]).astype(BF16), w_ref[0:dh, :])
    acc += _dot(_rms(ya_ref[...], gg[:, dh:dh + da]).astype(BF16), w_ref[dh:dh + da, :])
    acc += _dot(_rms(ym_ref[...], gg[:, dh + da:]).astype(BF16), w_ref[dh + da:, :])
    o_ref[...] = x_ref[...] + _rms(acc, gp_ref[...])


def mix_out(yh, ya, ym, g_grp, w_out, g_post, x, tm):
    t, d = x.shape
    dh, da, dm = yh.shape[1], ya.shape[1], ym.shape[1]
    body = functools.partial(_mix_out_body, dh=dh, da=da)
    return pl.pallas_call(
        body,
        out_shape=jax.ShapeDtypeStruct((t, d), F32),
        grid=(t // tm,),
        in_specs=[
            pl.BlockSpec((tm, dh), lambda i: (i, 0)),
            pl.BlockSpec((tm, da), lambda i: (i, 0)),
            pl.BlockSpec((tm, dm), lambda i: (i, 0)),
            pl.BlockSpec((1, d), lambda i: (0, 0)),
            pl.BlockSpec((d, d), lambda i: (0, 0)),
            pl.BlockSpec((1, d), lambda i: (0, 0)),
            pl.BlockSpec((tm, d), lambda i: (i, 0)),
        ],
        out_specs=pl.BlockSpec((tm, d), lambda i: (i, 0)),
        compiler_params=_cparams(("parallel",)),
        name="mix_out",
    )(yh, ya, ym, g_grp, w_out, g_post, x)


def _rope_tables(seq_len):
    half = ROT_DIM // 2
    inv_freq = np.exp(-math.log(ROPE_THETA) * np.arange(0, ROT_DIM, 2, dtype=np.float64) / ROT_DIM)
    ang = np.arange(seq_len, dtype=np.float64)[:, None] * inv_freq.astype(np.float32).astype(np.float64)[None, :]
    c, s = np.cos(ang), np.sin(ang)
    cos_t = np.ones((seq_len, HEAD_DIM), np.float64)
    sin_t = np.zeros((seq_len, HEAD_DIM), np.float64)
    cos_t[:, :half] = c
    cos_t[:, half:ROT_DIM] = c
    sin_t[:, :half] = -s
    sin_t[:, half:ROT_DIM] = s
    return cos_t.astype(np.float32), sin_t.astype(np.float32)


def _dil_attn_body(q_ref, k_ref, v_ref, cos_ref, sin_ref, o_ref, qs_ref, ks_ref, m_ref, l_ref, *, seq_len):
    half = ROT_DIM // 2
    scale = 1.0 / math.sqrt(HEAD_DIM)
    chunk = min(512, seq_len)

    def rope_chunk(c, carry):
        rows = pl.ds(pl.multiple_of(c * chunk, chunk), chunk)
        cs = cos_ref[rows, :]
        sn = sin_ref[rows, :]
        lane = lax.broadcasted_iota(jnp.int32, (chunk, HEAD_DIM), 1)
        for src, dst, mul in ((q_ref, qs_ref, scale), (k_ref, ks_ref, None)):
            x = src[rows, :]
            partner = jnp.where(lane < half, pltpu.roll(x, HEAD_DIM - half, 1), pltpu.roll(x, half, 1))
            y = x * cs + partner * sn
            dst[rows, :] = y if mul is None else y * mul
        return carry

    lax.fori_loop(0, seq_len // chunk, rope_chunk, 0)

    n_br = len(DILATED_PATTERNS)
    for bi, (window, dil) in enumerate(DILATED_PATTERNS):
        radius = window // (2 * dil)
        n = seq_len // dil
        tq = min(128, n)
        kw = min(tq + 2 * radius, n)
        nblk = n // tq

        def one_block(idx, bi=bi, dil=dil, radius=radius, n=n, tq=tq, kw=kw, nblk=nblk):
            r = idx // nblk
            q0 = (idx % nblk) * tq
            k0 = jnp.clip(q0 - radius, 0, n - kw)
            if dil == 1:
                qsl = pl.ds(pl.multiple_of(q0, tq), tq)
                ksl = pl.ds(pl.multiple_of(k0, 8), kw)
            else:
                qsl = pl.ds(r + q0 * dil, tq, stride=dil)
                ksl = pl.ds(r + k0 * dil, kw, stride=dil)
            qb = qs_ref[qsl, :].astype(BF16)
            kb = ks_ref[ksl, :].astype(BF16)
            vb = v_ref[ksl, :].astype(BF16)
            s = lax.dot_general(qb, kb, (((1,), (1,)), ((), ())), preferred_element_type=F32)
            rel = (k0 + lax.broadcasted_iota(jnp.int32, (tq, kw), 1)) - (q0 + lax.broadcasted_iota(jnp.int32, (tq, kw), 0))
            s = jnp.where(jnp.abs(rel) <= radius, s, NEG)
            m = jnp.max(s, axis=-1, keepdims=True)
            p = jnp.exp(s - m)
            l = jnp.sum(p, axis=-1, keepdims=True)
            acc = _dot(p.astype(BF16), vb)
            m = jnp.broadcast_to(m, (tq, HEAD_DIM))
            l = jnp.broadcast_to(l, (tq, HEAD_DIM))
            if bi > 0:
                m_old = m_ref[qsl, :]
                m_new = jnp.maximum(m_old, m)
                a_old = jnp.exp(m_old - m_new)
                a_new = jnp.exp(m - m_new)
                acc = o_ref[qsl, :] * a_old + acc * a_new
                l = l_ref[qsl, :] * a_old + l * a_new
                m = m_new
            return qsl, m, l, acc

        total = dil * nblk
        unroll = min(ATTN_BLOCKS_PER_STEP, total)

        def step(it, carry, bi=bi, unroll=unroll, one_block=one_block):
            done = [one_block(it * unroll + u) for u in range(unroll)]
            for qsl, m, l, acc in done:
                if bi == n_br - 1:
                    o_ref[qsl, :] = acc / l
                else:
                    o_ref[qsl, :] = acc
                    m_ref[qsl, :] = m
                    l_ref[qsl, :] = l
            return carry

        lax.fori_loop(0, total // unroll, step, 0)


def dil_attn(proj, batch, seq_len, q_col, k_col, v_col, n_heads):
    cos_t, sin_t = _rope_tables(seq_len)
    nb = 1 if seq_len * HEAD_DIM * 4 > (2 << 20) else 2

    def col_spec(col):
        return pl.BlockSpec((seq_len, HEAD_DIM), lambda b, h: (b, col + h), pipeline_mode=pl.Buffered(nb))

    tab_spec = pl.BlockSpec((seq_len, HEAD_DIM), lambda b, h: (0, 0), pipeline_mode=pl.Buffered(1))
    body = functools.partial(_dil_attn_body, seq_len=seq_len)
    return pl.pallas_call(
        body,
        out_shape=jax.ShapeDtypeStruct((batch * seq_len, n_heads * HEAD_DIM), F32),
        grid=(batch, n_heads),
        in_specs=[col_spec(q_col), col_spec(k_col), col_spec(v_col), tab_spec, tab_spec],
        out_specs=pl.BlockSpec((seq_len, HEAD_DIM), lambda b, h: (b, h), pipeline_mode=pl.Buffered(nb)),
        scratch_shapes=[pltpu.VMEM((seq_len, HEAD_DIM), F32)] * 4,
        compiler_params=_cparams(("parallel", "parallel")),
        name="dil_attn",
    )(proj, proj, proj, cos_t, sin_t)


def _mem_attn_body(q_ref, k_ref, v_ref, o_ref):
    s = lax.dot_general(q_ref[...].astype(BF16), k_ref[...].astype(BF16), (((1,), (1,)), ((), ())),
                        preferred_element_type=F32) * (1.0 / math.sqrt(HEAD_DIM))
    m = jnp.max(s, axis=-1, keepdims=True)
    p = jnp.exp(s - m)
    l = jnp.sum(p, axis=-1, keepdims=True)
    o_ref[...] = _dot(p.astype(BF16), v_ref[...].astype(BF16)) / l


def mem_attn(proj, kv, batch, seq_len, n_mem, q_col, n_heads, tq):
    nq = seq_len // tq
    return pl.pallas_call(
        _mem_attn_body,
        out_shape=jax.ShapeDtypeStruct((batch * seq_len, n_heads * HEAD_DIM), F32),
        grid=(batch, nq, n_heads),
        in_specs=[
            pl.BlockSpec((tq, HEAD_DIM), lambda b, i, h: (b * nq + i, q_col + h)),
            pl.BlockSpec((n_mem, HEAD_DIM), lambda b, i, h: (b, h)),
            pl.BlockSpec((n_mem, HEAD_DIM), lambda b, i, h: (b, n_heads + h)),
        ],
        out_specs=pl.BlockSpec((tq, HEAD_DIM), lambda b, i, h: (b * nq + i, h)),
        compiler_params=_cparams(("parallel", "parallel", "parallel")),
        name="mem_attn",
    )(proj, kv, kv)


def _fft_split(seq_len):
    n = 2 * seq_len
    n2 = 128
    return n // n2, n2


def _np_split(x):
    x32 = np.asarray(x, np.float32)
    hi = x32.astype(BF16)
    lo = (x32 - hi.astype(np.float32)).astype(BF16)
    return hi, lo


@functools.lru_cache(maxsize=None)
def _fft_consts(seq_len):
    n1, n2 = _fft_split(seq_len)
    n = n1 * n2
    kh = n1 // 2 + 1
    a_out = n1 // 2
    a = np.arange(n1)
    k1 = np.arange(kh)
    th = 2.0 * np.pi * ((k1[:, None] * a[None, :]) % n1) / n1
    eye = np.eye(SUB)
    fwd = np.stack([np.cos(th), -np.sin(th)], axis=1).reshape(2 * kh, n1)
    alpha = np.where((k1 == 0) | (k1 == n1 // 2), 1.0, 2.0) / n
    th_i = th[:, :a_out].T
    inv = np.stack([np.cos(th_i) * alpha[None, :], -np.sin(th_i) * alpha[None, :]], axis=2)
    inv = inv.reshape(a_out, 2 * kh)
    b = np.arange(n2)
    k2 = np.arange(n2)
    kk = k1[:, None, None] + n1 * k2[None, :, None]
    ph = 2.0 * np.pi * ((kk * b[None, None, :]) % n) / n
    wr, wi = np.cos(ph), -np.sin(ph)
    vr, vi = np.transpose(wr, (0, 2, 1)), -np.transpose(wi, (0, 2, 1))

    def stack(re, im):
        blk = np.concatenate([np.concatenate([re, -im], axis=2), np.concatenate([im, re], axis=2)], axis=1)
        hi, lo = _np_split(blk)
        return np.concatenate([hi, lo], axis=1)

    return dict(n1=n1, n2=n2, kh=kh, a_out=a_out,
                a_fwd=_np_split(np.kron(fwd, eye)), a_fwd_half=_np_split(np.kron(fwd[:, :a_out], eye)),
                a_inv=_np_split(np.kron(inv, eye)),
                w_fwd=stack(wr, wi), w_inv=stack(vr, vi))


def _filter_tables(seq_len, channels):
    n = 2 * seq_len
    bands = (FILTER_EMB - 1) // 2
    pos = np.concatenate([np.arange(seq_len), [0], np.arange(seq_len - 1, 0, -1)])
    t = np.linspace(0.0, 1.0, seq_len)[pos]
    w = 2.0 * np.pi * pos / seq_len
    f = np.linspace(1e-4, bands - 1, bands)
    ang = w[:, None] * f[None, :]
    z = np.zeros((n, FILTER_HIDDEN), np.float64)
    z[:, 0] = t
    z[:, 1:1 + bands] = np.cos(ang)
    z[:, 1 + bands:1 + 2 * bands] = -np.sin(ang)
    deltas = np.abs(np.linspace(math.log(DECAY_TARGET) / DECAY_SLOW, math.log(DECAY_TARGET) / DECAY_FAST, channels))
    return z.astype(np.float32), deltas[None, :].astype(np.float32)


def _filter_body(z_ref, w1_ref, b1_ref, w2_ref, b2_ref, w3_ref, b3_ref, fr_ref, w4_ref, dl_ref,
                 k_ref, s_ref, *, tr, seq_len, channels):
    i = pl.program_id(0)
    fr = fr_ref[...]
    z = z_ref[...]

    def layer(h, w_ref, b_ref):
        w_hi, w_lo = _split(w_ref[...])
        h_hi, h_lo = _split(h)
        pre = _dot(h_hi, w_hi) + _dot(h_lo, w_hi) + _dot(h_hi, w_lo)
        return jnp.sin(fr * (pre + b_ref[...]))

    h = layer(z, w1_ref, b1_ref)
    h = layer(h, w2_ref, b2_ref)
    h = layer(h, w3_ref, b3_ref)
    w_hi, w_lo = _split(w4_ref[...])
    h_hi, h_lo = _split(h)
    k = _dot(h_hi, w_hi) + _dot(h_lo, w_hi) + _dot(h_hi, w_lo)
    decay = jnp.exp(-z[:, 0:1] * dl_ref[...]) + DECAY_SHIFT
    row = i * tr + lax.broadcasted_iota(jnp.int32, (tr, 1), 0)
    decay = jnp.where(row == seq_len, 0.0, decay)
    k = k * jnp.concatenate([decay, decay], axis=1)
    k_ref[...] = k

    @pl.when(i == 0)
    def _():
        s_ref[...] = jnp.zeros_like(s_ref)

    s_ref[...] += jnp.sum(jnp.abs(k).reshape(tr // 8, 8, 2 * channels), axis=0)


def filter_gen(seq_len, channels, w1p, b1, w2, b2, w3, b3, freq, w4d):
    n = 2 * seq_len
    tr = min(512, seq_len)
    z, deltas = _filter_tables(seq_len, channels)
    hid = FILTER_HIDDEN
    full = lambda shape: pl.BlockSpec(shape, lambda i: (0,) * len(shape))
    body = functools.partial(_filter_body, tr=tr, seq_len=seq_len, channels=channels)
    return pl.pallas_call(
        body,
        out_shape=(jax.ShapeDtypeStruct((n, 2 * channels), F32), jax.ShapeDtypeStruct((8, 2 * channels), F32)),
        grid=(n // tr,),
        in_specs=[
            pl.BlockSpec((tr, hid), lambda i: (i, 0)),
            full((hid, hid)), full((1, hid)), full((hid, hid)), full((1, hid)), full((hid, hid)), full((1, hid)),
            full((1, hid)),
            pl.BlockSpec((None, hid, 2 * channels), lambda i: ((i * tr) // seq_len, 0, 0)),
            full((1, channels)),
        ],
        out_specs=(pl.BlockSpec((tr, 2 * channels), lambda i: (i, 0)), full((8, 2 * channels))),
        compiler_params=_cparams(("arbitrary",)),
        name="filter_gen",
    )(z, w1p, b1, w2, b2, w3, b3, freq, w4d, deltas)


def _major_fwd_body(u_ref, ah_ref, al_ref, y_ref, *, groups):
    a_in, _, ct = u_ref.shape
    kh = y_ref.shape[0]
    for q in range(groups):
        u = u_ref[:, q * SUB:(q + 1) * SUB, :].reshape(a_in * SUB, ct)
        y = _dot3(ah_ref[...], al_ref[...], u)
        y_ref[:, :, q * SUB:(q + 1) * SUB, :] = y.reshape(kh, 2, SUB, ct)


def major_fwd(u, col, channels, a_hi, a_lo, kh, ct, groups):
    bsz, a_in, n2, _ = u.shape
    nc = channels // ct
    rows = groups * SUB
    body = functools.partial(_major_fwd_body, groups=groups)
    return pl.pallas_call(
        body,
        out_shape=jax.ShapeDtypeStruct((bsz, kh, 2, n2, channels), F32),
        grid=(bsz, n2 // rows, nc),
        in_specs=[
            pl.BlockSpec((None, a_in, rows, ct), lambda b, r, j: (b, 0, r, col * nc + j)),
            pl.BlockSpec(a_hi.shape, lambda b, r, j: (0, 0)),
            pl.BlockSpec(a_lo.shape, lambda b, r, j: (0, 0)),
        ],
        out_specs=pl.BlockSpec((None, kh, 2, rows, ct), lambda b, r, j: (b, 0, 0, r, j)),
        compiler_params=_cparams(("parallel", "parallel", "parallel")),
        name="major_fwd",
    )(u, a_hi, a_lo)


def _major_inv_body(g_ref, ah_ref, al_ref, m_ref, o_ref, *, groups):
    kh, _, _, ct = g_ref.shape
    a_out = o_ref.shape[0]
    for q in range(groups):
        sl = slice(q * SUB, (q + 1) * SUB)
        g = g_ref[:, :, sl, :].reshape(kh * 2 * SUB, ct)
        y = _dot3(ah_ref[...], al_ref[...], g).reshape(a_out, SUB, ct)
        o_ref[:, sl, :] = y * m_ref[:, sl, :]


def major_inv(g, a_hi, a_lo, mult, col, ct, groups):
    bsz, kh, _, n2, channels = g.shape
    a_out = mult.shape[1]
    nc = channels // ct
    rows = groups * SUB
    body = functools.partial(_major_inv_body, groups=groups)
    return pl.pallas_call(
        body,
        out_shape=jax.ShapeDtypeStruct((bsz, a_out, n2, channels), F32),
        grid=(bsz, n2 // rows, nc),
        in_specs=[
            pl.BlockSpec((None, kh, 2, rows, ct), lambda b, r, j: (b, 0, 0, r, j)),
            pl.BlockSpec(a_hi.shape, lambda b, r, j: (0, 0)),
            pl.BlockSpec(a_lo.shape, lambda b, r, j: (0, 0)),
            pl.BlockSpec((None, a_out, rows, ct), lambda b, r, j: (b, 0, r, col * nc + j)),
        ],
        out_specs=pl.BlockSpec((None, a_out, rows, ct), lambda b, r, j: (b, 0, r, j)),
        compiler_params=_cparams(("parallel", "parallel", "parallel")),
        name="major_inv",
    )(g, a_hi, a_lo, mult)


def _cplx_apply(w_ref, xr, xi, n2):
    xr_hi, xr_lo = _split(xr)
    xi_hi, xi_lo = _split(xi)
    x_hi = jnp.concatenate([xr_hi, xi_hi], axis=0)
    x_lo = jnp.concatenate([xr_lo, xi_lo], axis=0)
    p = _dot(w_ref[...], x_hi)
    q = _dot(w_ref[0:2 * n2, :], x_lo)
    z = p[:2 * n2] + p[2 * n2:] + q
    return z[:n2], z[n2:]


def _minor_conv_body(y_ref, wf_ref, wi_ref, k_ref, g_ref, *, n2):
    zr, zi = _cplx_apply(wf_ref, y_ref[0], y_ref[1], n2)
    kr = k_ref[0]
    ki = k_ref[1]
    pr = zr * kr - zi * ki
    pi = zr * ki + zi * kr
    gr, gi = _cplx_apply(wi_ref, pr, pi, n2)
    g_ref[0] = gr
    g_ref[1] = gi


def minor_conv(y, w_fwd, w_inv, spec, order, ct):
    bsz, kh, _, n2, c = y.shape
    nc = c // ct
    y_spec = pl.BlockSpec((None, None, 2, n2, ct), lambda k, j, b: (b, k, 0, 0, j))
    w_spec = pl.BlockSpec((None, 4 * n2, 2 * n2), lambda k, j, b: (k, 0, 0))
    k_spec = pl.BlockSpec((None, 2, n2, ct), lambda k, j, b: (k, 0, 0, order * nc + j))
    body = functools.partial(_minor_conv_body, n2=n2)
    return pl.pallas_call(
        body,
        out_shape=jax.ShapeDtypeStruct(y.shape, F32),
        grid=(kh, nc, bsz),
        in_specs=[y_spec, w_spec, w_spec, k_spec],
        out_specs=y_spec,
        compiler_params=_cparams(("parallel", "parallel", "parallel")),
        name="minor_conv",
    )(y, w_fwd, w_inv, spec)


def _minor_filter_body(y_ref, wf_ref, inv_ref, bias_ref, k_ref, *, n2):
    zr, zi = _cplx_apply(wf_ref, y_ref[0], y_ref[1], n2)
    inv = inv_ref[...]
    k_ref[0] = zr * inv + bias_ref[...]
    k_ref[1] = zi * inv


def minor_filter(y, w_fwd, inv_norm, bias, ct):
    kh, _, n2, c2 = y.shape
    y_spec = pl.BlockSpec((None, 2, n2, ct), lambda k, j: (k, 0, 0, j))
    v_spec = pl.BlockSpec((1, ct), lambda k, j: (0, j))
    body = functools.partial(_minor_filter_body, n2=n2)
    return pl.pallas_call(
        body,
        out_shape=jax.ShapeDtypeStruct(y.shape, F32),
        grid=(kh, c2 // ct),
        in_specs=[y_spec, pl.BlockSpec((None, 4 * n2, 2 * n2), lambda k, j: (k, 0, 0)), v_spec, v_spec],
        out_specs=y_spec,
        compiler_params=_cparams(("parallel", "parallel")),
        name="minor_filter",
    )(y, w_fwd, inv_norm, bias)


def _major_groups(a_in):
    return max(1, 64 // a_in)


def hyena_filter_spectra(seq_len, channels, fw, f_bias):
    cst = _fft_consts(seq_len)
    n1, n2, kh = cst["n1"], cst["n2"], cst["kh"]
    kern, sums = filter_gen(seq_len, channels, *fw)
    inv_norm = 1.0 / jnp.sum(sums, axis=0, keepdims=True)
    u = kern.reshape(1, n1, n2, 2 * channels)
    y = major_fwd(u, 0, 2 * channels, *cst["a_fwd"], kh, ct=512, groups=_major_groups(n1))
    return minor_filter(y[0], cst["w_fwd"], inv_norm, f_bias.reshape(1, 2 * channels), ct=512)


def long_conv_gated(u, u_col, mult, mult_col, spectra, order, seq_len, channels):
    cst = _fft_consts(seq_len)
    groups = _major_groups(cst["a_out"])
    y = major_fwd(u, u_col, channels, *cst["a_fwd_half"], cst["kh"], ct=512, groups=groups)
    g = minor_conv(y, cst["w_fwd"], cst["w_inv"], spectra, order, ct=512)
    return major_inv(g, *cst["a_inv"], mult, mult_col, ct=512, groups=groups)


def _row_tile(seq_len, want):
    return min(want, seq_len)


def _layer(x, mem, lw, batch, seq_len, dims):
    d, dh, da, dm = dims
    n_heads = da // HEAD_DIM
    n_mem = mem.shape[0] // batch
    tm = _row_tile(seq_len, 512)

    proj = norm_matmul_conv(x, lw["g_pre_mix"], lw["w_in"], lw["conv_w"], lw["conv_b"], seq_len, tm, tn=512)

    cst = _fft_consts(seq_len)
    p4 = proj.reshape(batch, cst["a_out"], cst["n2"], proj.shape[1])
    spectra = lw["spectra"][seq_len]
    z = long_conv_gated(p4, 0, p4, 1, spectra, 0, seq_len, dh)
    y_h = long_conv_gated(z, 0, p4, 2, spectra, 1, seq_len, dh).reshape(batch * seq_len, dh)

    qcol = 3 * dh // HEAD_DIM
    y_a = dil_attn(proj, batch, seq_len, qcol, qcol + n_heads, qcol + 2 * n_heads, n_heads)
    kv = norm_matmul(mem, lw["g_mem"], lw["w_mem_kv"], tm=n_mem, tn=512)
    y_m = mem_attn(proj, kv, batch, seq_len, n_mem, qcol + 3 * n_heads, dm // HEAD_DIM, tq=_row_tile(seq_len, 1024))

    x = mix_out(y_h, y_a, y_m, lw["g_grp"], lw["w_out"], lw["g_post_mix"], x, tm)
    return ffn(x, lw["g_pre_ffn"], lw["w_up"], lw["ffn_conv_w"], lw["ffn_conv_b"], lw["w_down"],
               lw["g_post_ffn"], seq_len, tm, tf=512)


def kernel(x_prompt, x_sample, mem_prompt, mem_sample, g_pre_mix, w_in, conv_w, conv_b, f_w1, f_b1, f_w2, f_b2, f_w3, f_b3, f_w4, f_freq, f_bias, g_mem, w_mem_kv, g_grp, w_out, g_post_mix, g_pre_ffn, w_up, ffn_conv_w, ffn_conv_b, w_down, g_post_ffn):
    depth, d, d_in = w_in.shape
    dh = conv_w.shape[2] // 3
    dm = w_mem_kv.shape[2] // 2
    da = (d_in - 3 * dh - dm) // 3
    dims = (d, dh, da, dm)
    groups = [(x_prompt, mem_prompt), (x_sample, mem_sample)]
    seq_lens = sorted({g[0].shape[1] for g in groups})

    xs = [g[0].reshape(-1, d) for g in groups]
    mems = [g[1].reshape(-1, d) for g in groups]
    hid = FILTER_HIDDEN
    for i in range(depth):
        row = lambda v: v[i][None, :]
        cw = jnp.zeros((3, d_in), F32).at[1].set(1.0).at[:, :3 * dh].set(conv_w[i])
        cb = jnp.zeros((1, d_in), F32).at[0, :3 * dh].set(conv_b[i])
        w1p = jnp.zeros((hid, hid), F32).at[:FILTER_EMB].set(f_w1[i])
        w4d = f_w4[i].reshape(hid, 2, 2, dh).transpose(2, 0, 1, 3).reshape(2, hid, 2 * dh)
        fw = (w1p, row(f_b1), f_w2[i], row(f_b2), f_w3[i], row(f_b3), row(f_freq), w4d)
        lw = dict(
            g_pre_mix=row(g_pre_mix), w_in=w_in[i].astype(BF16), conv_w=cw, conv_b=cb,
            g_mem=row(g_mem), w_mem_kv=w_mem_kv[i].astype(BF16), g_grp=row(g_grp),
            w_out=w_out[i].astype(BF16), g_post_mix=row(g_post_mix), g_pre_ffn=row(g_pre_ffn),
            w_up=w_up[i].astype(BF16), ffn_conv_w=ffn_conv_w[i], ffn_conv_b=row(ffn_conv_b),
            w_down=w_down[i].astype(BF16), g_post_ffn=row(g_post_ffn),
            spectra={sl: hyena_filter_spectra(sl, dh, fw, f_bias[i]) for sl in seq_lens},
        )
        for gi, (xg, _) in enumerate(groups):
            xs[gi] = _layer(xs[gi], mems[gi], lw, xg.shape[0], xg.shape[1], dims)
    return tuple(x.reshape(g[0].shape) for x, g in zip(xs, groups))
```

```python
import functools
import math

import numpy as np
import jax
import jax.numpy as jnp
from jax import lax
from jax.experimental import pallas as pl
from jax.experimental.pallas import tpu as pltpu

F32 = jnp.float32
BF16 = jnp.bfloat16

HEAD_DIM = 128
DILATED_PATTERNS = ((128, 1), (512, 4), (2048, 16))
ROPE_THETA = 500000.0
ROT_DIM = HEAD_DIM // 4
FILTER_EMB = 33
FILTER_HIDDEN = 64
DECAY_FAST = 0.3
DECAY_SLOW = 1.5
DECAY_TARGET = 1e-2
DECAY_SHIFT = 0.05
EPS = 1e-6
NEG = -1e30

HALO = 16
ATTN_BLOCKS_PER_STEP = 4
SUB = 8
SUB_BF16 = 16
VMEM_LIMIT = 56 * 1024 * 1024


def _cparams(sem):
    return pltpu.CompilerParams(dimension_semantics=sem, vmem_limit_bytes=VMEM_LIMIT)


def _split(x):
    hi = x.astype(BF16)
    lo = (x - hi.astype(F32)).astype(BF16)
    return hi, lo


def _dot(a, b):
    return jnp.dot(a, b, preferred_element_type=F32)


def _dot3(a_hi, a_lo, b):
    b_hi, b_lo = _split(b)
    return _dot(a_hi, b_hi) + _dot(a_lo, b_hi) + _dot(a_hi, b_lo)


def _rms(v, g):
    return v * lax.rsqrt(jnp.mean(v * v, axis=-1, keepdims=True) + EPS) * g


def _fill_normed(hb_ref, xp_ref, x_ref, xn_ref, g_ref, i, tm, blocks_per_seq):
    g = g_ref[...]
    pos = i % blocks_per_seq
    keep_p = (pos != 0).astype(F32)
    keep_n = (pos != blocks_per_seq - 1).astype(F32)
    hb_ref[0:HALO, :] = (_rms(xp_ref[...], g) * keep_p).astype(BF16)
    hb_ref[HALO:HALO + tm, :] = _rms(x_ref[...], g).astype(BF16)
    hb_ref[HALO + tm:, :] = (_rms(xn_ref[...], g) * keep_n).astype(BF16)


def _conv3(u, cw, cb, tm):
    rows = tm + 2 * HALO
    up = pltpu.roll(u, 1, 0)[HALO:HALO + tm]
    un = pltpu.roll(u, rows - 1, 0)[HALO:HALO + tm]
    uc = u[HALO:HALO + tm]
    return up * cw[0:1] + uc * cw[1:2] + un * cw[2:3] + cb


def _nm_conv_body(xp_ref, x_ref, xn_ref, g_ref, w_ref, cw_ref, cb_ref, o_ref, hb_ref, *, tm, blocks_per_seq):
    i = pl.program_id(0)

    @pl.when(pl.program_id(1) == 0)
    def _():
        _fill_normed(hb_ref, xp_ref, x_ref, xn_ref, g_ref, i, tm, blocks_per_seq)

    u = _dot(hb_ref[...], w_ref[...])
    o_ref[...] = _conv3(u, cw_ref[...], cb_ref[...], tm)


def _nm_body(x_ref, g_ref, w_ref, o_ref, hb_ref):
    @pl.when(pl.program_id(1) == 0)
    def _():
        hb_ref[...] = _rms(x_ref[...], g_ref[...]).astype(BF16)

    o_ref[...] = _dot(hb_ref[...], w_ref[...])


def _halo_specs(tm, d, n_rows):
    hb = tm // HALO
    last = n_rows // HALO - 1
    return [
        pl.BlockSpec((HALO, d), lambda i, j: (jnp.maximum(i * hb - 1, 0), 0)),
        pl.BlockSpec((tm, d), lambda i, j: (i, 0), pipeline_mode=pl.Buffered(1)),
        pl.BlockSpec((HALO, d), lambda i, j: (jnp.minimum((i + 1) * hb, last), 0)),
    ]


def norm_matmul_conv(x, g, w, cw, cb, seq_len, tm, tn):
    t, d = x.shape
    n = w.shape[1]
    body = functools.partial(_nm_conv_body, tm=tm, blocks_per_seq=seq_len // tm)
    return pl.pallas_call(
        body,
        out_shape=jax.ShapeDtypeStruct((t, n), F32),
        grid=(t // tm, n // tn),
        in_specs=_halo_specs(tm, d, t) + [
            pl.BlockSpec((1, d), lambda i, j: (0, 0)),
            pl.BlockSpec((d, tn), lambda i, j: (0, j)),
            pl.BlockSpec((3, tn), lambda i, j: (0, j)),
            pl.BlockSpec((1, tn), lambda i, j: (0, j)),
        ],
        out_specs=pl.BlockSpec((tm, tn), lambda i, j: (i, j)),
        scratch_shapes=[pltpu.VMEM((tm + 2 * HALO, d), BF16)],
        compiler_params=_cparams(("parallel", "arbitrary")),
        name="norm_matmul_conv",
    )(x, x, x, g, w, cw, cb)


def norm_matmul(x, g, w, tm, tn):
    t, d = x.shape
    n = w.shape[1]
    return pl.pallas_call(
        _nm_body,
        out_shape=jax.ShapeDtypeStruct((t, n), F32),
        grid=(t // tm, n // tn),
        in_specs=[
            pl.BlockSpec((tm, d), lambda i, j: (i, 0)),
            pl.BlockSpec((1, d), lambda i, j: (0, 0)),
            pl.BlockSpec((d, tn), lambda i, j: (0, j)),
        ],
        out_specs=pl.BlockSpec((tm, tn), lambda i, j: (i, j)),
        scratch_shapes=[pltpu.VMEM((tm, d), BF16)],
        compiler_params=_cparams(("parallel", "arbitrary")),
        name="norm_matmul",
    )(x, g, w)


def _ffn_body(xp_ref, x_ref, xn_ref, g_ref, wg_ref, wv_ref, cw_ref, cb_ref, wd_ref, gp_ref, o_ref,
              hb_ref, *, tm, blocks_per_seq, nj):
    i = pl.program_id(0)
    j = pl.program_id(1)

    @pl.when(j == 0)
    def _():
        _fill_normed(hb_ref, xp_ref, x_ref, xn_ref, g_ref, i, tm, blocks_per_seq)

    ug = _dot(hb_ref[...], wg_ref[...])
    val = _dot(hb_ref[HALO:HALO + tm, :], wv_ref[...])
    gate = _conv3(ug, cw_ref[...], cb_ref[...], tm)
    ff = jax.nn.gelu(gate, approximate=True) * val
    part = _dot(ff.astype(BF16), wd_ref[...])

    @pl.when(j == 0)
    def _():
        o_ref[...] = part

    @pl.when(j > 0)
    def _():
        o_ref[...] += part

    @pl.when(j == nj - 1)
    def _():
        o_ref[...] = x_ref[...] + _rms(o_ref[...], gp_ref[...])


def ffn(x, g_pre, w_up, cw, cb, w_down, g_post, seq_len, tm, tf):
    t, d = x.shape
    d_ff = w_down.shape[0]
    nj = d_ff // tf
    body = functools.partial(_ffn_body, tm=tm, blocks_per_seq=seq_len // tm, nj=nj)
    return pl.pallas_call(
        body,
        out_shape=jax.ShapeDtypeStruct((t, d), F32),
        grid=(t // tm, nj),
        in_specs=_halo_specs(tm, d, t) + [
            pl.BlockSpec((1, d), lambda i, j: (0, 0)),
            pl.BlockSpec((d, tf), lambda i, j: (0, j)),
            pl.BlockSpec((d, tf), lambda i, j: (0, j + nj)),
            pl.BlockSpec((3, tf), lambda i, j: (0, j)),
            pl.BlockSpec((1, tf), lambda i, j: (0, j)),
            pl.BlockSpec((tf, d), lambda i, j: (j, 0)),
            pl.BlockSpec((1, d), lambda i, j: (0, 0)),
        ],
        out_specs=pl.BlockSpec((tm, d), lambda i, j: (i, 0), pipeline_mode=pl.Buffered(1)),
        scratch_shapes=[pltpu.VMEM((tm + 2 * HALO, d), BF16)],
        compiler_params=_cparams(("parallel", "arbitrary")),
        name="ffn",
    )(x, x, x, g_pre, w_up, w_up, cw, cb, w_down, g_post)


def _mix_out_body(yh_ref, ya_ref, ym_ref, gg_ref, w_ref, gp_ref, x_ref, o_ref, *, dh, da):
    gg = gg_ref[...]
    acc = _dot(_rms(yh_ref[...], gg[:, :dh]).astype(BF16), w_ref[0:dh, :])
    acc += _dot(_rms(ya_ref[...], gg[:, dh:dh + da]).astype(BF16), w_ref[dh:dh + da, :])
    acc += _dot(_rms(ym_ref[...], gg[:, dh + da:]).astype(BF16), w_ref[dh + da:, :])
    o_ref[...] = x_ref[...] + _rms(acc, gp_ref[...])


def mix_out(yh, ya, ym, g_grp, w_out, g_post, x, tm):
    t, d = x.shape
    dh, da, dm = yh.shape[1], ya.shape[1], ym.shape[1]
    body = functools.partial(_mix_out_body, dh=dh, da=da)
    return pl.pallas_call(
        body,
        out_shape=jax.ShapeDtypeStruct((t, d), F32),
        grid=(t // tm,),
        in_specs=[
            pl.BlockSpec((tm, dh), lambda i: (i, 0)),
            pl.BlockSpec((tm, da), lambda i: (i, 0)),
            pl.BlockSpec((tm, dm), lambda i: (i, 0)),
            pl.BlockSpec((1, d), lambda i: (0, 0)),
            pl.BlockSpec((d, d), lambda i: (0, 0)),
            pl.BlockSpec((1, d), lambda i: (0, 0)),
            pl.BlockSpec((tm, d), lambda i: (i, 0)),
        ],
        out_specs=pl.BlockSpec((tm, d), lambda i: (i, 0)),
        compiler_params=_cparams(("parallel",)),
        name="mix_out",
    )(yh, ya, ym, g_grp, w_out, g_post, x)


def _rope_tables(seq_len):
    half = ROT_DIM // 2
    inv_freq = np.exp(-math.log(ROPE_THETA) * np.arange(0, ROT_DIM, 2, dtype=np.float64) / ROT_DIM)
    ang = np.arange(seq_len, dtype=np.float64)[:, None] * inv_freq.astype(np.float32).astype(np.float64)[None, :]
    c, s = np.cos(ang), np.sin(ang)
    cos_t = np.ones((seq_len, HEAD_DIM), np.float64)
    sin_t = np.zeros((seq_len, HEAD_DIM), np.float64)
    cos_t[:, :half] = c
    cos_t[:, half:ROT_DIM] = c
    sin_t[:, :half] = -s
    sin_t[:, half:ROT_DIM] = s
    return cos_t.astype(np.float32), sin_t.astype(np.float32)


def _dil_attn_body(q_ref, k_ref, v_ref, cos_ref, sin_ref, o_ref, qs_ref, ks_ref, m_ref, l_ref, *, seq_len):
    half = ROT_DIM // 2
    scale = 1.0 / math.sqrt(HEAD_DIM)
    chunk = min(512, seq_len)

    def rope_chunk(c, carry):
        rows = pl.ds(pl.multiple_of(c * chunk, chunk), chunk)
        cs = cos_ref[rows, :]
        sn = sin_ref[rows, :]
        lane = lax.broadcasted_iota(jnp.int32, (chunk, HEAD_DIM), 1)
        for src, dst, mul in ((q_ref, qs_ref, scale), (k_ref, ks_ref, None)):
            x = src[rows, :]
            partner = jnp.where(lane < half, pltpu.roll(x, HEAD_DIM - half, 1), pltpu.roll(x, half, 1))
            y = x * cs + partner * sn
            dst[rows, :] = y if mul is None else y * mul
        return carry

    lax.fori_loop(0, seq_len // chunk, rope_chunk, 0)

    n_br = len(DILATED_PATTERNS)
    for bi, (window, dil) in enumerate(DILATED_PATTERNS):
        radius = window // (2 * dil)
        n = seq_len // dil
        tq = min(128, n)
        kw = min(tq + 2 * radius, n)
        nblk = n // tq

        def one_block(idx, bi=bi, dil=dil, radius=radius, n=n, tq=tq, kw=kw, nblk=nblk):
            r = idx // nblk
            q0 = (idx % nblk) * tq
            k0 = jnp.clip(q0 - radius, 0, n - kw)
            if dil == 1:
                qsl = pl.ds(pl.multiple_of(q0, tq), tq)
                ksl = pl.ds(pl.multiple_of(k0, 8), kw)
            else:
                qsl = pl.ds(r + q0 * dil, tq, stride=dil)
                ksl = pl.ds(r + k0 * dil, kw, stride=dil)
            qb = qs_ref[qsl, :].astype(BF16)
            kb = ks_ref[ksl, :].astype(BF16)
            vb = v_ref[ksl, :].astype(BF16)
            s = lax.dot_general(qb, kb, (((1,), (1,)), ((), ())), preferred_element_type=F32)
            rel = (k0 + lax.broadcasted_iota(jnp.int32, (tq, kw), 1)) - (q0 + lax.broadcasted_iota(jnp.int32, (tq, kw), 0))
            s = jnp.where(jnp.abs(rel) <= radius, s, NEG)
            m = jnp.max(s, axis=-1, keepdims=True)
            p = jnp.exp(s - m)
            l = jnp.sum(p, axis=-1, keepdims=True)
            acc = _dot(p.astype(BF16), vb)
            m = jnp.broadcast_to(m, (tq, HEAD_DIM))
            l = jnp.broadcast_to(l, (tq, HEAD_DIM))
            if bi > 0:
                m_old = m_ref[qsl, :]
                m_new = jnp.maximum(m_old, m)
                a_old = jnp.exp(m_old - m_new)
                a_new = jnp.exp(m - m_new)
                acc = o_ref[qsl, :] * a_old + acc * a_new
                l = l_ref[qsl, :] * a_old + l * a_new
                m = m_new
            return qsl, m, l, acc

        total = dil * nblk
        unroll = min(ATTN_BLOCKS_PER_STEP, total)

        def step(it, carry, bi=bi, unroll=unroll, one_block=one_block):
            done = [one_block(it * unroll + u) for u in range(unroll)]
            for qsl, m, l, acc in done:
                if bi == n_br - 1:
                    o_ref[qsl, :] = acc / l
                else:
                    o_ref[qsl, :] = acc
                    m_ref[qsl, :] = m
                    l_ref[qsl, :] = l
            return carry

        lax.fori_loop(0, total // unroll, step, 0)


def dil_attn(proj, batch, seq_len, q_col, k_col, v_col, n_heads):
    cos_t, sin_t = _rope_tables(seq_len)
    nb = 1 if seq_len * HEAD_DIM * 4 > (2 << 20) else 2

    def col_spec(col):
        return pl.BlockSpec((seq_len, HEAD_DIM), lambda b, h: (b, col + h), pipeline_mode=pl.Buffered(nb))

    tab_spec = pl.BlockSpec((seq_len, HEAD_DIM), lambda b, h: (0, 0), pipeline_mode=pl.Buffered(1))
    body = functools.partial(_dil_attn_body, seq_len=seq_len)
    return pl.pallas_call(
        body,
        out_shape=jax.ShapeDtypeStruct((batch * seq_len, n_heads * HEAD_DIM), F32),
        grid=(batch, n_heads),
        in_specs=[col_spec(q_col), col_spec(k_col), col_spec(v_col), tab_spec, tab_spec],
        out_specs=pl.BlockSpec((seq_len, HEAD_DIM), lambda b, h: (b, h), pipeline_mode=pl.Buffered(nb)),
        scratch_shapes=[pltpu.VMEM((seq_len, HEAD_DIM), F32)] * 4,
        compiler_params=_cparams(("parallel", "parallel")),
        name="dil_attn",
    )(proj, proj, proj, cos_t, sin_t)


def _mem_attn_body(q_ref, k_ref, v_ref, o_ref):
    s = lax.dot_general(q_ref[...].astype(BF16), k_ref[...].astype(BF16), (((1,), (1,)), ((), ())),
                        preferred_element_type=F32) * (1.0 / math.sqrt(HEAD_DIM))
    m = jnp.max(s, axis=-1, keepdims=True)
    p = jnp.exp(s - m)
    l = jnp.sum(p, axis=-1, keepdims=True)
    o_ref[...] = _dot(p.astype(BF16), v_ref[...].astype(BF16)) / l


def mem_attn(proj, kv, batch, seq_len, n_mem, q_col, n_heads, tq):
    nq = seq_len // tq
    return pl.pallas_call(
        _mem_attn_body,
        out_shape=jax.ShapeDtypeStruct((batch * seq_len, n_heads * HEAD_DIM), F32),
        grid=(batch, nq, n_heads),
        in_specs=[
            pl.BlockSpec((tq, HEAD_DIM), lambda b, i, h: (b * nq + i, q_col + h)),
            pl.BlockSpec((n_mem, HEAD_DIM), lambda b, i, h: (b, h)),
            pl.BlockSpec((n_mem, HEAD_DIM), lambda b, i, h: (b, n_heads + h)),
        ],
        out_specs=pl.BlockSpec((tq, HEAD_DIM), lambda b, i, h: (b * nq + i, h)),
        compiler_params=_cparams(("parallel", "parallel", "parallel")),
        name="mem_attn",
    )(proj, kv, kv)


def _fft_split(seq_len):
    n = 2 * seq_len
    n2 = 128
    return n // n2, n2


def _np_split(x):
    x32 = np.asarray(x, np.float32)
    hi = x32.astype(BF16)
    lo = (x32 - hi.astype(np.float32)).astype(BF16)
    return hi, lo


@functools.lru_cache(maxsize=None)
def _fft_consts(seq_len):
    n1, n2 = _fft_split(seq_len)
    n = n1 * n2
    kh = n1 // 2 + 1
    a_out = n1 // 2
    a = np.arange(n1)
    k1 = np.arange(kh)
    th = 2.0 * np.pi * ((k1[:, None] * a[None, :]) % n1) / n1
    eye = np.eye(SUB)
    fwd = np.stack([np.cos(th), -np.sin(th)], axis=1).reshape(2 * kh, n1)
    alpha = np.where((k1 == 0) | (k1 == n1 // 2), 1.0, 2.0) / n
    th_i = th[:, :a_out].T
    inv = np.stack([np.cos(th_i) * alpha[None, :], -np.sin(th_i) * alpha[None, :]], axis=2)
    inv = inv.reshape(a_out, 2 * kh)
    b = np.arange(n2)
    k2 = np.arange(n2)
    kk = k1[:, None, None] + n1 * k2[None, :, None]
    ph = 2.0 * np.pi * ((kk * b[None, None, :]) % n) / n
    wr, wi = np.cos(ph), -np.sin(ph)
    vr, vi = np.transpose(wr, (0, 2, 1)), -np.transpose(wi, (0, 2, 1))

    def stack(re, im):
        blk = np.concatenate([np.concatenate([re, -im], axis=2), np.concatenate([im, re], axis=2)], axis=1)
        hi, lo = _np_split(blk)
        return np.concatenate([hi, lo], axis=1)

    eye_p = np.eye(SUB_BF16)
    return dict(n1=n1, n2=n2, kh=kh, a_out=a_out,
                a_fwd=_np_split(np.kron(fwd, eye)),
                a_fwd_half=_np_split(np.kron(fwd[:, :a_out], eye_p))[0],
                a_inv=_np_split(np.kron(inv, eye_p))[0],
                w_fwd=stack(wr, wi), w_inv=stack(vr, vi))


def _filter_tables(seq_len, channels):
    n = 2 * seq_len
    bands = (FILTER_EMB - 1) // 2
    pos = np.concatenate([np.arange(seq_len), [0], np.arange(seq_len - 1, 0, -1)])
    t = np.linspace(0.0, 1.0, seq_len)[pos]
    w = 2.0 * np.pi * pos / seq_len
    f = np.linspace(1e-4, bands - 1, bands)
    ang = w[:, None] * f[None, :]
    z = np.zeros((n, FILTER_HIDDEN), np.float64)
    z[:, 0] = t
    z[:, 1:1 + bands] = np.cos(ang)
    z[:, 1 + bands:1 + 2 * bands] = -np.sin(ang)
    deltas = np.abs(np.linspace(math.log(DECAY_TARGET) / DECAY_SLOW, math.log(DECAY_TARGET) / DECAY_FAST, channels))
    return z.astype(np.float32), deltas[None, :].astype(np.float32)


def _filter_body(z_ref, w1_ref, b1_ref, w2_ref, b2_ref, w3_ref, b3_ref, fr_ref, w4_ref, dl_ref,
                 k_ref, s_ref, *, tr, seq_len, channels):
    i = pl.program_id(0)
    fr = fr_ref[...]
    z = z_ref[...]

    def layer(h, w_ref, b_ref):
        w_hi, w_lo = _split(w_ref[...])
        h_hi, h_lo = _split(h)
        pre = _dot(h_hi, w_hi) + _dot(h_lo, w_hi) + _dot(h_hi, w_lo)
        return jnp.sin(fr * (pre + b_ref[...]))

    h = layer(z, w1_ref, b1_ref)
    h = layer(h, w2_ref, b2_ref)
    h = layer(h, w3_ref, b3_ref)
    w_hi, w_lo = _split(w4_ref[...])
    h_hi, h_lo = _split(h)
    k = _dot(h_hi, w_hi) + _dot(h_lo, w_hi) + _dot(h_hi, w_lo)
    decay = jnp.exp(-z[:, 0:1] * dl_ref[...]) + DECAY_SHIFT
    row = i * tr + lax.broadcasted_iota(jnp.int32, (tr, 1), 0)
    decay = jnp.where(row == seq_len, 0.0, decay)
    k = k * jnp.concatenate([decay, decay], axis=1)
    k_ref[...] = k

    @pl.when(i == 0)
    def _():
        s_ref[...] = jnp.zeros_like(s_ref)

    s_ref[...] += jnp.sum(jnp.abs(k).reshape(tr // 8, 8, 2 * channels), axis=0)


def filter_gen(seq_len, channels, w1p, b1, w2, b2, w3, b3, freq, w4d):
    n = 2 * seq_len
    tr = min(512, seq_len)
    z, deltas = _filter_tables(seq_len, channels)
    hid = FILTER_HIDDEN
    full = lambda shape: pl.BlockSpec(shape, lambda i: (0,) * len(shape))
    body = functools.partial(_filter_body, tr=tr, seq_len=seq_len, channels=channels)
    return pl.pallas_call(
        body,
        out_shape=(jax.ShapeDtypeStruct((n, 2 * channels), F32), jax.ShapeDtypeStruct((8, 2 * channels), F32)),
        grid=(n // tr,),
        in_specs=[
            pl.BlockSpec((tr, hid), lambda i: (i, 0)),
            full((hid, hid)), full((1, hid)), full((hid, hid)), full((1, hid)), full((hid, hid)), full((1, hid)),
            full((1, hid)),
            pl.BlockSpec((None, hid, 2 * channels), lambda i: ((i * tr) // seq_len, 0, 0)),
            full((1, channels)),
        ],
        out_specs=(pl.BlockSpec((tr, 2 * channels), lambda i: (i, 0)), full((8, 2 * channels))),
        compiler_params=_cparams(("arbitrary",)),
        name="filter_gen",
    )(z, w1p, b1, w2, b2, w3, b3, freq, w4d, deltas)


def _major_fwd_body(u_ref, *refs, groups, sub):
    *a_refs, y_ref = refs
    a_in, _, ct = u_ref.shape
    kh = y_ref.shape[0]
    for q in range(groups):
        rows = slice(q * sub, (q + 1) * sub)
        u = u_ref[:, rows, :].reshape(a_in * sub, ct)
        if len(a_refs) == 2:
            y = _dot3(a_refs[0][...], a_refs[1][...], u)
        else:
            y = _dot(a_refs[0][...], u.astype(BF16))
        y_ref[:, :, rows, :] = y.reshape(kh, 2, sub, ct).astype(y_ref.dtype)


def major_fwd(u, col, channels, mats, kh, ct, groups, sub, out_dtype):
    bsz, a_in, n2, _ = u.shape
    nc = channels // ct
    rows = groups * sub
    body = functools.partial(_major_fwd_body, groups=groups, sub=sub)
    return pl.pallas_call(
        body,
        out_shape=jax.ShapeDtypeStruct((bsz, kh, 2, n2, channels), out_dtype),
        grid=(bsz, n2 // rows, nc),
        in_specs=[pl.BlockSpec((None, a_in, rows, ct), lambda b, r, j: (b, 0, r, col * nc + j))]
        + [pl.BlockSpec(m.shape, lambda b, r, j: (0, 0)) for m in mats],
        out_specs=pl.BlockSpec((None, kh, 2, rows, ct), lambda b, r, j: (b, 0, 0, r, j)),
        compiler_params=_cparams(("parallel", "parallel", "parallel")),
        name="major_fwd",
    )(u, *mats)


def _major_inv_body(g_ref, a_ref, m_ref, o_ref, *, groups, sub):
    kh, _, _, ct = g_ref.shape
    a_out = o_ref.shape[0]
    for q in range(groups):
        rows = slice(q * sub, (q + 1) * sub)
        g = g_ref[:, :, rows, :].reshape(kh * 2 * sub, ct)
        y = _dot(a_ref[...], g).reshape(a_out, sub, ct)
        o_ref[:, rows, :] = y * m_ref[:, rows, :]


def major_inv(g, a_inv, mult, col, ct, groups, sub):
    bsz, kh, _, n2, channels = g.shape
    a_out = mult.shape[1]
    nc = channels // ct
    rows = groups * sub
    body = functools.partial(_major_inv_body, groups=groups, sub=sub)
    return pl.pallas_call(
        body,
        out_shape=jax.ShapeDtypeStruct((bsz, a_out, n2, channels), F32),
        grid=(bsz, n2 // rows, nc),
        in_specs=[
            pl.BlockSpec((None, kh, 2, rows, ct), lambda b, r, j: (b, 0, 0, r, j)),
            pl.BlockSpec(a_inv.shape, lambda b, r, j: (0, 0)),
            pl.BlockSpec((None, a_out, rows, ct), lambda b, r, j: (b, 0, r, col * nc + j)),
        ],
        out_specs=pl.BlockSpec((None, a_out, rows, ct), lambda b, r, j: (b, 0, r, j)),
        compiler_params=_cparams(("parallel", "parallel", "parallel")),
        name="major_inv",
    )(g, a_inv, mult)


def _cplx_apply(w_ref, xr, xi, n2):
    xr_hi, xr_lo = _split(xr)
    xi_hi, xi_lo = _split(xi)
    x_hi = jnp.concatenate([xr_hi, xi_hi], axis=0)
    x_lo = jnp.concatenate([xr_lo, xi_lo], axis=0)
    p = _dot(w_ref[...], x_hi)
    q = _dot(w_ref[0:2 * n2, :], x_lo)
    z = p[:2 * n2] + p[2 * n2:] + q
    return z[:n2], z[n2:]


def _minor_conv_body(y_ref, wf_ref, wi_ref, k_ref, g_ref, *, n2):
    z = _dot(wf_ref[...], jnp.concatenate([y_ref[0], y_ref[1]], axis=0))
    zr, zi = z[:n2], z[n2:]
    kr = k_ref[0]
    ki = k_ref[1]
    pr = (zr * kr - zi * ki).astype(BF16)
    pi = (zr * ki + zi * kr).astype(BF16)
    g = _dot(wi_ref[...], jnp.concatenate([pr, pi], axis=0))
    g_ref[0] = g[:n2].astype(g_ref.dtype)
    g_ref[1] = g[n2:].astype(g_ref.dtype)


def minor_conv(y, w_fwd, w_inv, spec, order, ct):
    bsz, kh, _, n2, c = y.shape
    nc = c // ct
    y_spec = pl.BlockSpec((None, None, 2, n2, ct), lambda k, j, b: (b, k, 0, 0, j))
    w_spec = pl.BlockSpec((None, 2 * n2, 2 * n2), lambda k, j, b: (k, 0, 0))
    k_spec = pl.BlockSpec((None, 2, n2, ct), lambda k, j, b: (k, 0, 0, order * nc + j))
    body = functools.partial(_minor_conv_body, n2=n2)
    return pl.pallas_call(
        body,
        out_shape=jax.ShapeDtypeStruct(y.shape, y.dtype),
        grid=(kh, nc, bsz),
        in_specs=[y_spec, w_spec, w_spec, k_spec],
        out_specs=y_spec,
        compiler_params=_cparams(("parallel", "parallel", "parallel")),
        name="minor_conv",
    )(y, w_fwd, w_inv, spec)


def _minor_filter_body(y_ref, wf_ref, inv_ref, bias_ref, k_ref, *, n2):
    zr, zi = _cplx_apply(wf_ref, y_ref[0], y_ref[1], n2)
    inv = inv_ref[...]
    k_ref[0] = zr * inv + bias_ref[...]
    k_ref[1] = zi * inv


def minor_filter(y, w_fwd, inv_norm, bias, ct):
    kh, _, n2, c2 = y.shape
    y_spec = pl.BlockSpec((None, 2, n2, ct), lambda k, j: (k, 0, 0, j))
    v_spec = pl.BlockSpec((1, ct), lambda k, j: (0, j))
    body = functools.partial(_minor_filter_body, n2=n2)
    return pl.pallas_call(
        body,
        out_shape=jax.ShapeDtypeStruct(y.shape, F32),
        grid=(kh, c2 // ct),
        in_specs=[y_spec, pl.BlockSpec((None, 4 * n2, 2 * n2), lambda k, j: (k, 0, 0)), v_spec, v_spec],
        out_specs=y_spec,
        compiler_params=_cparams(("parallel", "parallel")),
        name="minor_filter",
    )(y, w_fwd, inv_norm, bias)


def _major_groups(a_in, sub, n2):
    return min(n2 // sub, max(1, 2048 // (a_in * sub)))


def hyena_filter_spectra(seq_len, channels, fw, f_bias):
    cst = _fft_consts(seq_len)
    n1, n2, kh = cst["n1"], cst["n2"], cst["kh"]
    kern, sums = filter_gen(seq_len, channels, *fw)
    inv_norm = 1.0 / jnp.sum(sums, axis=0, keepdims=True)
    u = kern.reshape(1, n1, n2, 2 * channels)
    y = major_fwd(u, 0, 2 * channels, cst["a_fwd"], kh, ct=512, groups=_major_groups(n1, SUB, n2), sub=SUB,
                  out_dtype=F32)
    return minor_filter(y[0], cst["w_fwd"], inv_norm, f_bias.reshape(1, 2 * channels), ct=512)


def long_conv_gated(u, u_col, mult, mult_col, spectra, order, seq_len, channels):
    cst = _fft_consts(seq_len)
    groups = _major_groups(cst["a_out"], SUB_BF16, cst["n2"])
    y = major_fwd(u, u_col, channels, (cst["a_fwd_half"],), cst["kh"], ct=512, groups=groups, sub=SUB_BF16,
                  out_dtype=BF16)
    g = minor_conv(y, cst["w_fwd"], cst["w_inv"], spectra, order, ct=512)
    return major_inv(g, cst["a_inv"], mult, mult_col, ct=512, groups=groups, sub=SUB_BF16)


def _row_tile(seq_len, want):
    return min(want, seq_len)


def _layer(x, mem, lw, batch, seq_len, dims):
    d, dh, da, dm = dims
    n_heads = da // HEAD_DIM
    n_mem = mem.shape[0] // batch
    tm = _row_tile(seq_len, 1024)

    proj = norm_matmul_conv(x, lw["g_pre_mix"], lw["w_in"], lw["conv_w"], lw["conv_b"], seq_len, tm, tn=512)

    cst = _fft_consts(seq_len)
    p4 = proj.reshape(batch, cst["a_out"], cst["n2"], proj.shape[1])
    spectra = lw["spectra"][seq_len]
    z = long_conv_gated(p4, 0, p4, 1, spectra, 0, seq_len, dh)
    y_h = long_conv_gated(z, 0, p4, 2, spectra, 1, seq_len, dh).reshape(batch * seq_len, dh)

    qcol = 3 * dh // HEAD_DIM
    y_a = dil_attn(proj, batch, seq_len, qcol, qcol + n_heads, qcol + 2 * n_heads, n_heads)
    kv = norm_matmul(mem, lw["g_mem"], lw["w_mem_kv"], tm=n_mem, tn=512)
    y_m = mem_attn(proj, kv, batch, seq_len, n_mem, qcol + 3 * n_heads, dm // HEAD_DIM, tq=_row_tile(seq_len, 1024))

    x = mix_out(y_h, y_a, y_m, lw["g_grp"], lw["w_out"], lw["g_post_mix"], x, _row_tile(seq_len, 512))
    return ffn(x, lw["g_pre_ffn"], lw["w_up"], lw["ffn_conv_w"], lw["ffn_conv_b"], lw["w_down"],
               lw["g_post_ffn"], seq_len, tm, tf=512)


def kernel(x_prompt, x_sample, mem_prompt, mem_sample, g_pre_mix, w_in, conv_w, conv_b, f_w1, f_b1, f_w2, f_b2, f_w3, f_b3, f_w4, f_freq, f_bias, g_mem, w_mem_kv, g_grp, w_out, g_post_mix, g_pre_ffn, w_up, ffn_conv_w, ffn_conv_b, w_down, g_post_ffn):
    depth, d, d_in = w_in.shape
    dh = conv_w.shape[2] // 3
    dm = w_mem_kv.shape[2] // 2
    da = (d_in - 3 * dh - dm) // 3
    dims = (d, dh, da, dm)
    groups = [(x_prompt, mem_prompt), (x_sample, mem_sample)]
    seq_lens = sorted({g[0].shape[1] for g in groups})

    xs = [g[0].reshape(-1, d) for g in groups]
    mems = [g[1].reshape(-1, d) for g in groups]
    hid = FILTER_HIDDEN
    for i in range(depth):
        row = lambda v: v[i][None, :]
        cw = jnp.zeros((3, d_in), F32).at[1].set(1.0).at[:, :3 * dh].set(conv_w[i])
        cb = jnp.zeros((1, d_in), F32).at[0, :3 * dh].set(conv_b[i])
        w1p = jnp.zeros((hid, hid), F32).at[:FILTER_EMB].set(f_w1[i])
        w4d = f_w4[i].reshape(hid, 2, 2, dh).transpose(2, 0, 1, 3).reshape(2, hid, 2 * dh)
        fw = (w1p, row(f_b1), f_w2[i], row(f_b2), f_w3[i], row(f_b3), row(f_freq), w4d)
        lw = dict(
            g_pre_mix=row(g_pre_mix), w_in=w_in[i].astype(BF16), conv_w=cw, conv_b=cb,
            g_mem=row(g_mem), w_mem_kv=w_mem_kv[i].astype(BF16), g_grp=row(g_grp),
            w_out=w_out[i].astype(BF16), g_post_mix=row(g_post_mix), g_pre_ffn=row(g_pre_ffn),
            w_up=w_up[i].astype(BF16), ffn_conv_w=ffn_conv_w[i], ffn_conv_b=row(ffn_conv_b),
            w_down=w_down[i].astype(BF16), g_post_ffn=row(g_post_ffn),
            spectra={sl: hyena_filter_spectra(sl, dh, fw, f_bias[i]) for sl in seq_lens},
        )
        for gi, (xg, _) in enumerate(groups):
            xs[gi] = _layer(xs[gi], mems[gi], lw, xg.shape[0], xg.shape[1], dims)
    return tuple(x.reshape(g[0].shape) for x, g in zip(xs, groups))
```

```python
import functools
import math

import numpy as np
import jax
import jax.numpy as jnp
from jax import lax
from jax.experimental import pallas as pl
from jax.experimental.pallas import tpu as pltpu

F32 = jnp.float32
BF16 = jnp.bfloat16

HEAD_DIM = 128
DILATED_PATTERNS = ((128, 1), (512, 4), (2048, 16))
ROPE_THETA = 500000.0
ROT_DIM = HEAD_DIM // 4
FILTER_EMB = 33
FILTER_HIDDEN = 64
DECAY_FAST = 0.3
DECAY_SLOW = 1.5
DECAY_TARGET = 1e-2
DECAY_SHIFT = 0.05
EPS = 1e-6
NEG = -1e30

HALO = 16
ATTN_BLOCKS_PER_STEP = 4
SUB = 8
SUB_BF16 = 16
VMEM_LIMIT = 56 * 1024 * 1024


def _cparams(sem):
    return pltpu.CompilerParams(dimension_semantics=sem, vmem_limit_bytes=VMEM_LIMIT)


def _split(x):
    hi = x.astype(BF16)
    lo = (x - hi.astype(F32)).astype(BF16)
    return hi, lo


def _dot(a, b):
    return jnp.dot(a, b, preferred_element_type=F32)


def _rms(v, g):
    return v * lax.rsqrt(jnp.mean(v * v, axis=-1, keepdims=True) + EPS) * g


def _fill_normed(hb_ref, xp_ref, x_ref, xn_ref, g_ref, i, tm, blocks_per_seq):
    g = g_ref[...]
    pos = i % blocks_per_seq
    keep_p = (pos != 0).astype(F32)
    keep_n = (pos != blocks_per_seq - 1).astype(F32)
    hb_ref[0:HALO, :] = (_rms(xp_ref[...], g) * keep_p).astype(BF16)
    hb_ref[HALO:HALO + tm, :] = _rms(x_ref[...], g).astype(BF16)
    hb_ref[HALO + tm:, :] = (_rms(xn_ref[...], g) * keep_n).astype(BF16)


def _conv3(u, cw, cb, tm):
    rows = tm + 2 * HALO
    up = pltpu.roll(u, 1, 0)[HALO:HALO + tm]
    un = pltpu.roll(u, rows - 1, 0)[HALO:HALO + tm]
    uc = u[HALO:HALO + tm]
    return up * cw[0:1] + uc * cw[1:2] + un * cw[2:3] + cb


def _nm_conv_body(xp_ref, x_ref, xn_ref, g_ref, w_ref, cw_ref, cb_ref, o_ref, hb_ref, *, tm, blocks_per_seq):
    i = pl.program_id(0)

    @pl.when(pl.program_id(1) == 0)
    def _():
        _fill_normed(hb_ref, xp_ref, x_ref, xn_ref, g_ref, i, tm, blocks_per_seq)

    u = _dot(hb_ref[...], w_ref[...])
    o_ref[...] = _conv3(u, cw_ref[...], cb_ref[...], tm)


def _nm_body(x_ref, g_ref, w_ref, o_ref, hb_ref):
    @pl.when(pl.program_id(1) == 0)
    def _():
        hb_ref[...] = _rms(x_ref[...], g_ref[...]).astype(BF16)

    o_ref[...] = _dot(hb_ref[...], w_ref[...])


def _halo_specs(tm, d, n_rows):
    hb = tm // HALO
    last = n_rows // HALO - 1
    return [
        pl.BlockSpec((HALO, d), lambda i, j: (jnp.maximum(i * hb - 1, 0), 0)),
        pl.BlockSpec((tm, d), lambda i, j: (i, 0)),
        pl.BlockSpec((HALO, d), lambda i, j: (jnp.minimum((i + 1) * hb, last), 0)),
    ]


def norm_matmul_conv(x, g, w, cw, cb, seq_len, tm, tn):
    t, d = x.shape
    n = w.shape[1]
    body = functools.partial(_nm_conv_body, tm=tm, blocks_per_seq=seq_len // tm)
    return pl.pallas_call(
        body,
        out_shape=jax.ShapeDtypeStruct((t, n), F32),
        grid=(t // tm, n // tn),
        in_specs=_halo_specs(tm, d, t) + [
            pl.BlockSpec((1, d), lambda i, j: (0, 0)),
            pl.BlockSpec((d, tn), lambda i, j: (0, j)),
            pl.BlockSpec((3, tn), lambda i, j: (0, j)),
            pl.BlockSpec((1, tn), lambda i, j: (0, j)),
        ],
        out_specs=pl.BlockSpec((tm, tn), lambda i, j: (i, j)),
        scratch_shapes=[pltpu.VMEM((tm + 2 * HALO, d), BF16)],
        compiler_params=_cparams(("parallel", "arbitrary")),
        name="norm_matmul_conv",
    )(x, x, x, g, w, cw, cb)


def norm_matmul(x, g, w, tm, tn):
    t, d = x.shape
    n = w.shape[1]
    return pl.pallas_call(
        _nm_body,
        out_shape=jax.ShapeDtypeStruct((t, n), F32),
        grid=(t // tm, n // tn),
        in_specs=[
            pl.BlockSpec((tm, d), lambda i, j: (i, 0)),
            pl.BlockSpec((1, d), lambda i, j: (0, 0)),
            pl.BlockSpec((d, tn), lambda i, j: (0, j)),
        ],
        out_specs=pl.BlockSpec((tm, tn), lambda i, j: (i, j)),
        scratch_shapes=[pltpu.VMEM((tm, d), BF16)],
        compiler_params=_cparams(("parallel", "arbitrary")),
        name="norm_matmul",
    )(x, g, w)


def _ffn_body(xp_ref, x_ref, xn_ref, g_ref, wg_ref, wv_ref, cw_ref, cb_ref, wd_ref, gp_ref, o_ref,
              hb_ref, *, tm, blocks_per_seq, nj):
    i = pl.program_id(0)
    j = pl.program_id(1)

    @pl.when(j == 0)
    def _():
        _fill_normed(hb_ref, xp_ref, x_ref, xn_ref, g_ref, i, tm, blocks_per_seq)

    ug = _dot(hb_ref[...], wg_ref[...])
    val = _dot(hb_ref[HALO:HALO + tm, :], wv_ref[...])
    gate = _conv3(ug, cw_ref[...], cb_ref[...], tm)
    ff = jax.nn.gelu(gate, approximate=True) * val
    part = _dot(ff.astype(BF16), wd_ref[...])

    @pl.when(j == 0)
    def _():
        o_ref[...] = part

    @pl.when(j > 0)
    def _():
        o_ref[...] += part

    @pl.when(j == nj - 1)
    def _():
        o_ref[...] = x_ref[...] + _rms(o_ref[...], gp_ref[...])


def ffn(x, g_pre, w_up, cw, cb, w_down, g_post, seq_len, tm, tf):
    t, d = x.shape
    d_ff = w_down.shape[0]
    nj = d_ff // tf
    body = functools.partial(_ffn_body, tm=tm, blocks_per_seq=seq_len // tm, nj=nj)
    return pl.pallas_call(
        body,
        out_shape=jax.ShapeDtypeStruct((t, d), F32),
        grid=(t // tm, nj),
        in_specs=_halo_specs(tm, d, t) + [
            pl.BlockSpec((1, d), lambda i, j: (0, 0)),
            pl.BlockSpec((d, tf), lambda i, j: (0, j)),
            pl.BlockSpec((d, tf), lambda i, j: (0, j + nj)),
            pl.BlockSpec((3, tf), lambda i, j: (0, j)),
            pl.BlockSpec((1, tf), lambda i, j: (0, j)),
            pl.BlockSpec((tf, d), lambda i, j: (j, 0)),
            pl.BlockSpec((1, d), lambda i, j: (0, 0)),
        ],
        out_specs=pl.BlockSpec((tm, d), lambda i, j: (i, 0)),
        scratch_shapes=[pltpu.VMEM((tm + 2 * HALO, d), BF16)],
        compiler_params=_cparams(("parallel", "arbitrary")),
        name="ffn",
    )(x, x, x, g_pre, w_up, w_up, cw, cb, w_down, g_post)


def _mix_out_body(yh_ref, ya_ref, ym_ref, gg_ref, w_ref, gp_ref, x_ref, o_ref, *, dh, da):
    gg = gg_ref[...]
    acc = _dot(_rms(yh_ref[...], gg[:, :dh]).astype(BF16), w_ref[0:dh, :])
    acc += _dot(_rms(ya_ref[...], gg[:, dh:dh + da]).astype(BF16), w_ref[dh:dh + da, :])
    acc += _dot(_rms(ym_ref[...], gg[:, dh + da:]).astype(BF16), w_ref[dh + da:, :])
    o_ref[...] = x_ref[...] + _rms(acc, gp_ref[...])


def mix_out(yh, ya, ym, g_grp, w_out, g_post, x, tm):
    t, d = x.shape
    dh, da, dm = yh.shape[1], ya.shape[1], ym.shape[1]
    body = functools.partial(_mix_out_body, dh=dh, da=da)
    return pl.pallas_call(
        body,
        out_shape=jax.ShapeDtypeStruct((t, d), F32),
        grid=(t // tm,),
        in_specs=[
            pl.BlockSpec((tm, dh), lambda i: (i, 0)),
            pl.BlockSpec((tm, da), lambda i: (i, 0)),
            pl.BlockSpec((tm, dm), lambda i: (i, 0)),
            pl.BlockSpec((1, d), lambda i: (0, 0)),
            pl.BlockSpec((d, d), lambda i: (0, 0)),
            pl.BlockSpec((1, d), lambda i: (0, 0)),
            pl.BlockSpec((tm, d), lambda i: (i, 0)),
        ],
        out_specs=pl.BlockSpec((tm, d), lambda i: (i, 0)),
        compiler_params=_cparams(("parallel",)),
        name="mix_out",
    )(yh, ya, ym, g_grp, w_out, g_post, x)


def _rope_tables(seq_len):
    half = ROT_DIM // 2
    inv_freq = np.exp(-math.log(ROPE_THETA) * np.arange(0, ROT_DIM, 2, dtype=np.float64) / ROT_DIM)
    ang = np.arange(seq_len, dtype=np.float64)[:, None] * inv_freq.astype(np.float32).astype(np.float64)[None, :]
    c, s = np.cos(ang), np.sin(ang)
    cos_t = np.ones((seq_len, HEAD_DIM), np.float64)
    sin_t = np.zeros((seq_len, HEAD_DIM), np.float64)
    cos_t[:, :half] = c
    cos_t[:, half:ROT_DIM] = c
    sin_t[:, :half] = -s
    sin_t[:, half:ROT_DIM] = s
    return cos_t.astype(np.float32), sin_t.astype(np.float32)


def _dil_attn_body(q_ref, k_ref, v_ref, cos_ref, sin_ref, o_ref, qs_ref, ks_ref, m_ref, l_ref, *, seq_len):
    half = ROT_DIM // 2
    scale = 1.0 / math.sqrt(HEAD_DIM)
    chunk = min(512, seq_len)

    def rope_chunk(c, carry):
        rows = pl.ds(pl.multiple_of(c * chunk, chunk), chunk)
        cs = cos_ref[rows, :]
        sn = sin_ref[rows, :]
        lane = lax.broadcasted_iota(jnp.int32, (chunk, HEAD_DIM), 1)
        for src, dst, mul in ((q_ref, qs_ref, scale), (k_ref, ks_ref, None)):
            x = src[rows, :]
            partner = jnp.where(lane < half, pltpu.roll(x, HEAD_DIM - half, 1), pltpu.roll(x, half, 1))
            y = x * cs + partner * sn
            dst[rows, :] = y if mul is None else y * mul
        return carry

    lax.fori_loop(0, seq_len // chunk, rope_chunk, 0)

    n_br = len(DILATED_PATTERNS)
    for bi, (window, dil) in enumerate(DILATED_PATTERNS):
        radius = window // (2 * dil)
        n = seq_len // dil
        tq = min(128, n)
        kw = min(tq + 2 * radius, n)
        nblk = n // tq

        def one_block(idx, bi=bi, dil=dil, radius=radius, n=n, tq=tq, kw=kw, nblk=nblk):
            r = idx // nblk
            q0 = (idx % nblk) * tq
            k0 = jnp.clip(q0 - radius, 0, n - kw)
            if dil == 1:
                qsl = pl.ds(pl.multiple_of(q0, tq), tq)
                ksl = pl.ds(pl.multiple_of(k0, 8), kw)
            else:
                qsl = pl.ds(r + q0 * dil, tq, stride=dil)
                ksl = pl.ds(r + k0 * dil, kw, stride=dil)
            qb = qs_ref[qsl, :].astype(BF16)
            kb = ks_ref[ksl, :].astype(BF16)
            vb = v_ref[ksl, :].astype(BF16)
            s = lax.dot_general(qb, kb, (((1,), (1,)), ((), ())), preferred_element_type=F32)
            rel = (k0 + lax.broadcasted_iota(jnp.int32, (tq, kw), 1)) - (q0 + lax.broadcasted_iota(jnp.int32, (tq, kw), 0))
            s = jnp.where(jnp.abs(rel) <= radius, s, NEG)
            m = jnp.max(s, axis=-1, keepdims=True)
            p = jnp.exp(s - m)
            l = jnp.sum(p, axis=-1, keepdims=True)
            acc = _dot(p.astype(BF16), vb)
            m = jnp.broadcast_to(m, (tq, HEAD_DIM))
            l = jnp.broadcast_to(l, (tq, HEAD_DIM))
            if bi > 0:
                m_old = m_ref[qsl, :]
                m_new = jnp.maximum(m_old, m)
                a_old = jnp.exp(m_old - m_new)
                a_new = jnp.exp(m - m_new)
                acc = o_ref[qsl, :] * a_old + acc * a_new
                l = l_ref[qsl, :] * a_old + l * a_new
                m = m_new
            return qsl, m, l, acc

        total = dil * nblk
        unroll = min(ATTN_BLOCKS_PER_STEP, total)

        def step(it, carry, bi=bi, unroll=unroll, one_block=one_block):
            done = [one_block(it * unroll + u) for u in range(unroll)]
            for qsl, m, l, acc in done:
                if bi == n_br - 1:
                    o_ref[qsl, :] = acc / l
                else:
                    o_ref[qsl, :] = acc
                    m_ref[qsl, :] = m
                    l_ref[qsl, :] = l
            return carry

        lax.fori_loop(0, total // unroll, step, 0)


def dil_attn(proj, batch, seq_len, q_col, k_col, v_col, n_heads):
    cos_t, sin_t = _rope_tables(seq_len)
    nb = 1 if seq_len * HEAD_DIM * 4 > (2 << 20) else 2

    def col_spec(col):
        return pl.BlockSpec((seq_len, HEAD_DIM), lambda b, h: (b, col + h), pipeline_mode=pl.Buffered(nb))

    tab_spec = pl.BlockSpec((seq_len, HEAD_DIM), lambda b, h: (0, 0), pipeline_mode=pl.Buffered(1))
    body = functools.partial(_dil_attn_body, seq_len=seq_len)
    return pl.pallas_call(
        body,
        out_shape=jax.ShapeDtypeStruct((batch * seq_len, n_heads * HEAD_DIM), F32),
        grid=(batch, n_heads),
        in_specs=[col_spec(q_col), col_spec(k_col), col_spec(v_col), tab_spec, tab_spec],
        out_specs=pl.BlockSpec((seq_len, HEAD_DIM), lambda b, h: (b, h), pipeline_mode=pl.Buffered(nb)),
        scratch_shapes=[pltpu.VMEM((seq_len, HEAD_DIM), F32)] * 4,
        compiler_params=_cparams(("parallel", "parallel")),
        name="dil_attn",
    )(proj, proj, proj, cos_t, sin_t)


def _mem_attn_body(q_ref, k_ref, v_ref, o_ref):
    s = lax.dot_general(q_ref[...].astype(BF16), k_ref[...].astype(BF16), (((1,), (1,)), ((), ())),
                        preferred_element_type=F32) * (1.0 / math.sqrt(HEAD_DIM))
    m = jnp.max(s, axis=-1, keepdims=True)
    p = jnp.exp(s - m)
    l = jnp.sum(p, axis=-1, keepdims=True)
    o_ref[...] = _dot(p.astype(BF16), v_ref[...].astype(BF16)) / l


def mem_attn(proj, kv, batch, seq_len, n_mem, q_col, n_heads, tq):
    nq = seq_len // tq
    return pl.pallas_call(
        _mem_attn_body,
        out_shape=jax.ShapeDtypeStruct((batch * seq_len, n_heads * HEAD_DIM), F32),
        grid=(batch, nq, n_heads),
        in_specs=[
            pl.BlockSpec((tq, HEAD_DIM), lambda b, i, h: (b * nq + i, q_col + h)),
            pl.BlockSpec((n_mem, HEAD_DIM), lambda b, i, h: (b, h)),
            pl.BlockSpec((n_mem, HEAD_DIM), lambda b, i, h: (b, n_heads + h)),
        ],
        out_specs=pl.BlockSpec((tq, HEAD_DIM), lambda b, i, h: (b * nq + i, h)),
        compiler_params=_cparams(("parallel", "parallel", "parallel")),
        name="mem_attn",
    )(proj, kv, kv)


def _fft_split(seq_len):
    n = 2 * seq_len
    n2 = 128
    return n // n2, n2


@functools.lru_cache(maxsize=None)
def _fft_consts(seq_len):
    n1, n2 = _fft_split(seq_len)
    n = n1 * n2
    kh = n1 // 2 + 1
    a_out = n1 // 2
    a = np.arange(n1)
    k1 = np.arange(kh)
    th = 2.0 * np.pi * ((k1[:, None] * a[None, :]) % n1) / n1
    fwd = np.stack([np.cos(th), -np.sin(th)], axis=1).reshape(2 * kh, n1)
    alpha = np.where((k1 == 0) | (k1 == n1 // 2), 1.0, 2.0) / n
    th_i = th[:, :a_out].T
    inv = np.stack([np.cos(th_i) * alpha[None, :], -np.sin(th_i) * alpha[None, :]], axis=2)
    inv = inv.reshape(a_out, 2 * kh)
    b = np.arange(n2)
    k2 = np.arange(n2)
    kk = k1[:, None, None] + n1 * k2[None, :, None]
    ph = 2.0 * np.pi * ((kk * b[None, None, :]) % n) / n
    wr, wi = np.cos(ph), -np.sin(ph)
    vr, vi = np.transpose(wr, (0, 2, 1)), -np.transpose(wi, (0, 2, 1))

    def block(re, im):
        blk = np.concatenate([np.concatenate([re, -im], axis=2), np.concatenate([im, re], axis=2)], axis=1)
        return np.asarray(blk, np.float32).astype(BF16)

    def kron_bf16(m, sub):
        return np.asarray(np.kron(m, np.eye(sub)), np.float32).astype(BF16)

    return dict(n1=n1, n2=n2, kh=kh, a_out=a_out,
                a_fwd=kron_bf16(fwd, SUB),
                a_fwd_half=kron_bf16(fwd[:, :a_out], SUB_BF16),
                a_inv=kron_bf16(inv, SUB_BF16),
                w_fwd=block(wr, wi), w_inv=block(vr, vi))


def _filter_tables(seq_len, channels):
    n = 2 * seq_len
    bands = (FILTER_EMB - 1) // 2
    pos = np.concatenate([np.arange(seq_len), [0], np.arange(seq_len - 1, 0, -1)])
    t = np.linspace(0.0, 1.0, seq_len)[pos]
    w = 2.0 * np.pi * pos / seq_len
    f = np.linspace(1e-4, bands - 1, bands)
    ang = w[:, None] * f[None, :]
    z = np.zeros((n, FILTER_HIDDEN), np.float64)
    z[:, 0] = t
    z[:, 1:1 + bands] = np.cos(ang)
    z[:, 1 + bands:1 + 2 * bands] = -np.sin(ang)
    deltas = np.abs(np.linspace(math.log(DECAY_TARGET) / DECAY_SLOW, math.log(DECAY_TARGET) / DECAY_FAST, channels))
    return z.astype(np.float32), deltas[None, :].astype(np.float32)


def _filter_body(z_ref, w1_ref, b1_ref, w2_ref, b2_ref, w3_ref, b3_ref, fr_ref, w4_ref, dl_ref,
                 k_ref, s_ref, *, tr, seq_len, channels):
    i = pl.program_id(0)
    fr = fr_ref[...]
    z = z_ref[...]

    def layer(h, w_ref, b_ref):
        w_hi, w_lo = _split(w_ref[...])
        h_hi, h_lo = _split(h)
        pre = _dot(h_hi, w_hi) + _dot(h_lo, w_hi) + _dot(h_hi, w_lo)
        return jnp.sin(fr * (pre + b_ref[...]))

    h = layer(z, w1_ref, b1_ref)
    h = layer(h, w2_ref, b2_ref)
    h = layer(h, w3_ref, b3_ref)
    w_hi, w_lo = _split(w4_ref[...])
    h_hi, h_lo = _split(h)
    k = _dot(h_hi, w_hi) + _dot(h_lo, w_hi) + _dot(h_hi, w_lo)
    decay = jnp.exp(-z[:, 0:1] * dl_ref[...]) + DECAY_SHIFT
    row = i * tr + lax.broadcasted_iota(jnp.int32, (tr, 1), 0)
    decay = jnp.where(row == seq_len, 0.0, decay)
    k = k * jnp.concatenate([decay, decay], axis=1)
    k_ref[...] = k

    @pl.when(i == 0)
    def _():
        s_ref[...] = jnp.zeros_like(s_ref)

    s_ref[...] += jnp.sum(jnp.abs(k).reshape(tr // 8, 8, 2 * channels), axis=0)


def filter_gen(seq_len, channels, w1p, b1, w2, b2, w3, b3, freq, w4d):
    n = 2 * seq_len
    tr = min(512, seq_len)
    z, deltas = _filter_tables(seq_len, channels)
    hid = FILTER_HIDDEN
    full = lambda shape: pl.BlockSpec(shape, lambda i: (0,) * len(shape))
    body = functools.partial(_filter_body, tr=tr, seq_len=seq_len, channels=channels)
    return pl.pallas_call(
        body,
        out_shape=(jax.ShapeDtypeStruct((n, 2 * channels), F32), jax.ShapeDtypeStruct((8, 2 * channels), F32)),
        grid=(n // tr,),
        in_specs=[
            pl.BlockSpec((tr, hid), lambda i: (i, 0)),
            full((hid, hid)), full((1, hid)), full((hid, hid)), full((1, hid)), full((hid, hid)), full((1, hid)),
            full((1, hid)),
            pl.BlockSpec((None, hid, 2 * channels), lambda i: ((i * tr) // seq_len, 0, 0)),
            full((1, channels)),
        ],
        out_specs=(pl.BlockSpec((tr, 2 * channels), lambda i: (i, 0)), full((8, 2 * channels))),
        compiler_params=_cparams(("arbitrary",)),
        name="filter_gen",
    )(z, w1p, b1, w2, b2, w3, b3, freq, w4d, deltas)


def _major_fwd_body(u_ref, a_ref, y_ref, *, groups, sub):
    a_in, _, ct = u_ref.shape
    kh = y_ref.shape[0]
    for q in range(groups):
        rows = slice(q * sub, (q + 1) * sub)
        u = u_ref[:, rows, :].reshape(a_in * sub, ct)
        y = _dot(a_ref[...], u.astype(BF16))
        y_ref[:, :, rows, :] = y.reshape(kh, 2, sub, ct).astype(y_ref.dtype)


def major_fwd(u, col, channels, a_fwd, kh, ct, groups, sub, out_dtype):
    bsz, a_in, n2, _ = u.shape
    nc = channels // ct
    rows = groups * sub
    body = functools.partial(_major_fwd_body, groups=groups, sub=sub)
    return pl.pallas_call(
        body,
        out_shape=jax.ShapeDtypeStruct((bsz, kh, 2, n2, channels), out_dtype),
        grid=(bsz, n2 // rows, nc),
        in_specs=[
            pl.BlockSpec((None, a_in, rows, ct), lambda b, r, j: (b, 0, r, col * nc + j)),
            pl.BlockSpec(a_fwd.shape, lambda b, r, j: (0, 0)),
        ],
        out_specs=pl.BlockSpec((None, kh, 2, rows, ct), lambda b, r, j: (b, 0, 0, r, j)),
        compiler_params=_cparams(("parallel", "parallel", "parallel")),
        name="major_fwd",
    )(u, a_fwd)


def _major_inv_body(g_ref, a_ref, m_ref, o_ref, *, groups, sub):
    kh, _, _, ct = g_ref.shape
    a_out = o_ref.shape[0]
    for q in range(groups):
        rows = slice(q * sub, (q + 1) * sub)
        g = g_ref[:, :, rows, :].reshape(kh * 2 * sub, ct)
        y = _dot(a_ref[...], g).reshape(a_out, sub, ct)
        o_ref[:, rows, :] = y * m_ref[:, rows, :]


def major_inv(g, a_inv, mult, col, ct, groups, sub):
    bsz, kh, _, n2, channels = g.shape
    a_out = mult.shape[1]
    nc = channels // ct
    rows = groups * sub
    body = functools.partial(_major_inv_body, groups=groups, sub=sub)
    return pl.pallas_call(
        body,
        out_shape=jax.ShapeDtypeStruct((bsz, a_out, n2, channels), F32),
        grid=(bsz, n2 // rows, nc),
        in_specs=[
            pl.BlockSpec((None, kh, 2, rows, ct), lambda b, r, j: (b, 0, 0, r, j)),
            pl.BlockSpec(a_inv.shape, lambda b, r, j: (0, 0)),
            pl.BlockSpec((None, a_out, rows, ct), lambda b, r, j: (b, 0, r, col * nc + j)),
        ],
        out_specs=pl.BlockSpec((None, a_out, rows, ct), lambda b, r, j: (b, 0, r, j)),
        compiler_params=_cparams(("parallel", "parallel", "parallel")),
        name="major_inv",
    )(g, a_inv, mult)


def _minor_conv_body(y_ref, wf_ref, wi_ref, k_ref, g_ref, *, n2):
    kr = k_ref[0]
    ki = k_ref[1]
    for i in range(y_ref.shape[0]):
        z = _dot(wf_ref[...], jnp.concatenate([y_ref[i, 0], y_ref[i, 1]], axis=0))
        zr, zi = z[:n2], z[n2:]
        pr = (zr * kr - zi * ki).astype(BF16)
        pi = (zr * ki + zi * kr).astype(BF16)
        g = _dot(wi_ref[...], jnp.concatenate([pr, pi], axis=0))
        g_ref[i, 0] = g[:n2].astype(g_ref.dtype)
        g_ref[i, 1] = g[n2:].astype(g_ref.dtype)


def minor_conv(y, w_fwd, w_inv, spec, order, ct, bb):
    bsz, kh, _, n2, c = y.shape
    nc = c // ct
    y_spec = pl.BlockSpec((bb, None, 2, n2, ct), lambda k, j, b: (b, k, 0, 0, j))
    w_spec = pl.BlockSpec((None, 2 * n2, 2 * n2), lambda k, j, b: (k, 0, 0))
    k_spec = pl.BlockSpec((None, 2, n2, ct), lambda k, j, b: (k, 0, 0, order * nc + j))
    body = functools.partial(_minor_conv_body, n2=n2)
    return pl.pallas_call(
        body,
        out_shape=jax.ShapeDtypeStruct(y.shape, y.dtype),
        grid=(kh, nc, bsz // bb),
        in_specs=[y_spec, w_spec, w_spec, k_spec],
        out_specs=y_spec,
        compiler_params=_cparams(("parallel", "parallel", "parallel")),
        name="minor_conv",
    )(y, w_fwd, w_inv, spec)


def _minor_filter_body(y_ref, wf_ref, inv_ref, bias_ref, k_ref, *, n2):
    z = _dot(wf_ref[...], jnp.concatenate([y_ref[0].astype(BF16), y_ref[1].astype(BF16)], axis=0))
    inv = inv_ref[...]
    k_ref[0] = z[:n2] * inv + bias_ref[...]
    k_ref[1] = z[n2:] * inv


def minor_filter(y, w_fwd, inv_norm, bias, ct):
    kh, _, n2, c2 = y.shape
    y_spec = pl.BlockSpec((None, 2, n2, ct), lambda k, j: (k, 0, 0, j))
    v_spec = pl.BlockSpec((1, ct), lambda k, j: (0, j))
    body = functools.partial(_minor_filter_body, n2=n2)
    return pl.pallas_call(
        body,
        out_shape=jax.ShapeDtypeStruct(y.shape, F32),
        grid=(kh, c2 // ct),
        in_specs=[y_spec, pl.BlockSpec((None, 2 * n2, 2 * n2), lambda k, j: (k, 0, 0)), v_spec, v_spec],
        out_specs=y_spec,
        compiler_params=_cparams(("parallel", "parallel")),
        name="minor_filter",
    )(y, w_fwd, inv_norm, bias)


def _major_groups(a_in, sub, n2):
    return min(n2 // sub, max(1, 2048 // (a_in * sub)))


def hyena_filter_spectra(seq_len, channels, fw, f_bias):
    cst = _fft_consts(seq_len)
    n1, n2, kh = cst["n1"], cst["n2"], cst["kh"]
    kern, sums = filter_gen(seq_len, channels, *fw)
    inv_norm = 1.0 / jnp.sum(sums, axis=0, keepdims=True)
    u = kern.reshape(1, n1, n2, 2 * channels)
    y = major_fwd(u, 0, 2 * channels, cst["a_fwd"], kh, ct=512, groups=_major_groups(n1, SUB, n2), sub=SUB,
                  out_dtype=F32)
    return minor_filter(y[0], cst["w_fwd"], inv_norm, f_bias.reshape(1, 2 * channels), ct=1024)


def long_conv_gated(u, u_col, mult, mult_col, spectra, order, seq_len, channels):
    cst = _fft_consts(seq_len)
    groups = _major_groups(cst["a_out"], SUB_BF16, cst["n2"])
    y = major_fwd(u, u_col, channels, cst["a_fwd_half"], cst["kh"], ct=512, groups=groups, sub=SUB_BF16,
                  out_dtype=BF16)
    g = minor_conv(y, cst["w_fwd"], cst["w_inv"], spectra, order, ct=channels, bb=min(u.shape[0], 4))
    return major_inv(g, cst["a_inv"], mult, mult_col, ct=512, groups=groups, sub=SUB_BF16)


def _row_tile(seq_len, want):
    return min(want, seq_len)


def _layer(x, mem, lw, batch, seq_len, dims):
    d, dh, da, dm = dims
    n_heads = da // HEAD_DIM
    n_mem = mem.shape[0] // batch
    tm = _row_tile(seq_len, 1024)

    proj = norm_matmul_conv(x, lw["g_pre_mix"], lw["w_in"], lw["conv_w"], lw["conv_b"], seq_len, tm, tn=512)

    cst = _fft_consts(seq_len)
    p4 = proj.reshape(batch, cst["a_out"], cst["n2"], proj.shape[1])
    spectra = lw["spectra"][seq_len]
    z = long_conv_gated(p4, 0, p4, 1, spectra, 0, seq_len, dh)
    y_h = long_conv_gated(z, 0, p4, 2, spectra, 1, seq_len, dh).reshape(batch * seq_len, dh)

    qcol = 3 * dh // HEAD_DIM
    y_a = dil_attn(proj, batch, seq_len, qcol, qcol + n_heads, qcol + 2 * n_heads, n_heads)
    kv = norm_matmul(mem, lw["g_mem"], lw["w_mem_kv"], tm=n_mem, tn=512)
    y_m = mem_attn(proj, kv, batch, seq_len, n_mem, qcol + 3 * n_heads, dm // HEAD_DIM, tq=_row_tile(seq_len, 1024))

    x = mix_out(y_h, y_a, y_m, lw["g_grp"], lw["w_out"], lw["g_post_mix"], x, _row_tile(seq_len, 512))
    return ffn(x, lw["g_pre_ffn"], lw["w_up"], lw["ffn_conv_w"], lw["ffn_conv_b"], lw["w_down"],
               lw["g_post_ffn"], seq_len, _row_tile(seq_len, 512), tf=512)


def kernel(x_prompt, x_sample, mem_prompt, mem_sample, g_pre_mix, w_in, conv_w, conv_b, f_w1, f_b1, f_w2, f_b2, f_w3, f_b3, f_w4, f_freq, f_bias, g_mem, w_mem_kv, g_grp, w_out, g_post_mix, g_pre_ffn, w_up, ffn_conv_w, ffn_conv_b, w_down, g_post_ffn):
    depth, d, d_in = w_in.shape
    dh = conv_w.shape[2] // 3
    dm = w_mem_kv.shape[2] // 2
    da = (d_in - 3 * dh - dm) // 3
    dims = (d, dh, da, dm)
    groups = [(x_prompt, mem_prompt), (x_sample, mem_sample)]
    seq_lens = sorted({g[0].shape[1] for g in groups})

    xs = [g[0].reshape(-1, d) for g in groups]
    mems = [g[1].reshape(-1, d) for g in groups]
    hid = FILTER_HIDDEN
    for i in range(depth):
        row = lambda v: v[i][None, :]
        cw = jnp.zeros((3, d_in), F32).at[1].set(1.0).at[:, :3 * dh].set(conv_w[i])
        cb = jnp.zeros((1, d_in), F32).at[0, :3 * dh].set(conv_b[i])
        w1p = jnp.zeros((hid, hid), F32).at[:FILTER_EMB].set(f_w1[i])
        w4d = f_w4[i].reshape(hid, 2, 2, dh).transpose(2, 0, 1, 3).reshape(2, hid, 2 * dh)
        fw = (w1p, row(f_b1), f_w2[i], row(f_b2), f_w3[i], row(f_b3), row(f_freq), w4d)
        lw = dict(
            g_pre_mix=row(g_pre_mix), w_in=w_in[i].astype(BF16), conv_w=cw, conv_b=cb,
            g_mem=row(g_mem), w_mem_kv=w_mem_kv[i].astype(BF16), g_grp=row(g_grp),
            w_out=w_out[i].astype(BF16), g_post_mix=row(g_post_mix), g_pre_ffn=row(g_pre_ffn),
            w_up=w_up[i].astype(BF16), ffn_conv_w=ffn_conv_w[i], ffn_conv_b=row(ffn_conv_b),
            w_down=w_down[i].astype(BF16), g_post_ffn=row(g_post_ffn),
            spectra={sl: hyena_filter_spectra(sl, dh, fw, f_bias[i]) for sl in seq_lens},
        )
        for gi, (xg, _) in enumerate(groups):
            xs[gi] = _layer(xs[gi], mems[gi], lw, xg.shape[0], xg.shape[1], dims)
    return tuple(x.reshape(g[0].shape) for x, g in zip(xs, groups))
```

```python
import functools
import math

import numpy as np
import jax
import jax.numpy as jnp
from jax import lax
from jax.experimental import pallas as pl
from jax.experimental.pallas import tpu as pltpu

F32 = jnp.float32
BF16 = jnp.bfloat16

HEAD_DIM = 128
DILATED_PATTERNS = ((128, 1), (512, 4), (2048, 16))
ROPE_THETA = 500000.0
ROT_DIM = HEAD_DIM // 4
FILTER_EMB = 33
FILTER_HIDDEN = 64
DECAY_FAST = 0.3
DECAY_SLOW = 1.5
DECAY_TARGET = 1e-2
DECAY_SHIFT = 0.05
EPS = 1e-6
NEG = -1e30

HALO = 16
ATTN_BLOCKS_PER_STEP = 4
SUB = 8
SUB_BF16 = 16
VMEM_LIMIT = 56 * 1024 * 1024


def _cparams(sem):
    return pltpu.CompilerParams(dimension_semantics=sem, vmem_limit_bytes=VMEM_LIMIT)


def _split(x):
    hi = x.astype(BF16)
    lo = (x - hi.astype(F32)).astype(BF16)
    return hi, lo


def _dot(a, b):
    return jnp.dot(a, b, preferred_element_type=F32)


def _rms(v, g):
    return v * lax.rsqrt(jnp.mean(v * v, axis=-1, keepdims=True) + EPS) * g


def _fill_normed(hb_ref, xp_ref, x_ref, xn_ref, g_ref, i, tm, blocks_per_seq):
    g = g_ref[...]
    pos = i % blocks_per_seq
    keep_p = (pos != 0).astype(F32)
    keep_n = (pos != blocks_per_seq - 1).astype(F32)
    hb_ref[0:HALO, :] = (_rms(xp_ref[...], g) * keep_p).astype(BF16)
    hb_ref[HALO:HALO + tm, :] = _rms(x_ref[...], g).astype(BF16)
    hb_ref[HALO + tm:, :] = (_rms(xn_ref[...], g) * keep_n).astype(BF16)


def _conv3(u, cw, cb, tm):
    rows = tm + 2 * HALO
    up = pltpu.roll(u, 1, 0)[HALO:HALO + tm]
    un = pltpu.roll(u, rows - 1, 0)[HALO:HALO + tm]
    uc = u[HALO:HALO + tm]
    return up * cw[0:1] + uc * cw[1:2] + un * cw[2:3] + cb


def _nm_conv_body(xp_ref, x_ref, xn_ref, g_ref, w_ref, cw_ref, cb_ref, o_ref, hb_ref, *, tm, blocks_per_seq,
                  n_conv):
    i = pl.program_id(0)
    j = pl.program_id(1)

    @pl.when(j == 0)
    def _():
        _fill_normed(hb_ref, xp_ref, x_ref, xn_ref, g_ref, i, tm, blocks_per_seq)

    u = _dot(hb_ref[...], w_ref[...])

    @pl.when(j < n_conv)
    def _():
        o_ref[...] = _conv3(u, cw_ref[...], cb_ref[...], tm).astype(o_ref.dtype)

    @pl.when(j >= n_conv)
    def _():
        o_ref[...] = u[HALO:HALO + tm].astype(o_ref.dtype)


def _nm_body(x_ref, g_ref, w_ref, o_ref, hb_ref):
    @pl.when(pl.program_id(1) == 0)
    def _():
        hb_ref[...] = _rms(x_ref[...], g_ref[...]).astype(BF16)

    o_ref[...] = _dot(hb_ref[...], w_ref[...]).astype(o_ref.dtype)


def _halo_specs(tm, d, n_rows):
    hb = tm // HALO
    last = n_rows // HALO - 1
    return [
        pl.BlockSpec((HALO, d), lambda i, j: (jnp.maximum(i * hb - 1, 0), 0)),
        pl.BlockSpec((tm, d), lambda i, j: (i, 0)),
        pl.BlockSpec((HALO, d), lambda i, j: (jnp.minimum((i + 1) * hb, last), 0)),
    ]


def norm_matmul_conv(x, g, w, layer, cw, cb, seq_len, tm, tn):
    t, d = x.shape
    n = w.shape[2]
    n_conv = cw.shape[1] // tn
    body = functools.partial(_nm_conv_body, tm=tm, blocks_per_seq=seq_len // tm, n_conv=n_conv)
    return pl.pallas_call(
        body,
        out_shape=jax.ShapeDtypeStruct((t, n), BF16),
        grid=(t // tm, n // tn),
        in_specs=_halo_specs(tm, d, t) + [
            pl.BlockSpec((1, d), lambda i, j: (0, 0)),
            pl.BlockSpec((None, d, tn), lambda i, j: (layer, 0, j)),
            pl.BlockSpec((3, tn), lambda i, j: (0, jnp.minimum(j, n_conv - 1))),
            pl.BlockSpec((1, tn), lambda i, j: (0, jnp.minimum(j, n_conv - 1))),
        ],
        out_specs=pl.BlockSpec((tm, tn), lambda i, j: (i, j)),
        scratch_shapes=[pltpu.VMEM((tm + 2 * HALO, d), BF16)],
        compiler_params=_cparams(("parallel", "arbitrary")),
        name="norm_matmul_conv",
    )(x, x, x, g, w, cw, cb)


def norm_matmul(x, g, w, layer, tm, tn):
    t, d = x.shape
    n = w.shape[2]
    return pl.pallas_call(
        _nm_body,
        out_shape=jax.ShapeDtypeStruct((t, n), BF16),
        grid=(t // tm, n // tn),
        in_specs=[
            pl.BlockSpec((tm, d), lambda i, j: (i, 0)),
            pl.BlockSpec((1, d), lambda i, j: (0, 0)),
            pl.BlockSpec((None, d, tn), lambda i, j: (layer, 0, j)),
        ],
        out_specs=pl.BlockSpec((tm, tn), lambda i, j: (i, j)),
        scratch_shapes=[pltpu.VMEM((tm, d), BF16)],
        compiler_params=_cparams(("parallel", "arbitrary")),
        name="norm_matmul",
    )(x, g, w)


def _ffn_body(xp_ref, x_ref, xn_ref, g_ref, wg_ref, wv_ref, cw_ref, cb_ref, wd_ref, gp_ref, o_ref,
              hb_ref, acc_ref, *, tm, blocks_per_seq, nj):
    i = pl.program_id(0)
    j = pl.program_id(1)

    @pl.when(j == 0)
    def _():
        _fill_normed(hb_ref, xp_ref, x_ref, xn_ref, g_ref, i, tm, blocks_per_seq)
        acc_ref[...] = jnp.zeros_like(acc_ref)

    ug = _dot(hb_ref[...], wg_ref[...])
    val = _dot(hb_ref[HALO:HALO + tm, :], wv_ref[...])
    gate = _conv3(ug, cw_ref[...], cb_ref[...], tm)
    ff = jax.nn.gelu(gate, approximate=True) * val
    acc_ref[...] += _dot(ff.astype(BF16), wd_ref[...])

    @pl.when(j == nj - 1)
    def _():
        o_ref[...] = x_ref[...] + _rms(acc_ref[...], gp_ref[...])


def ffn(x, g_pre, w_up, cw, cb, w_down, g_post, layer, seq_len, tm, tf):
    t, d = x.shape
    d_ff = w_down.shape[1]
    nj = d_ff // tf
    body = functools.partial(_ffn_body, tm=tm, blocks_per_seq=seq_len // tm, nj=nj)
    return pl.pallas_call(
        body,
        out_shape=jax.ShapeDtypeStruct((t, d), F32),
        grid=(t // tm, nj),
        in_specs=_halo_specs(tm, d, t) + [
            pl.BlockSpec((1, d), lambda i, j: (0, 0)),
            pl.BlockSpec((None, d, tf), lambda i, j: (layer, 0, j)),
            pl.BlockSpec((None, d, tf), lambda i, j: (layer, 0, j + nj)),
            pl.BlockSpec((3, tf), lambda i, j: (0, j)),
            pl.BlockSpec((1, tf), lambda i, j: (0, j)),
            pl.BlockSpec((None, tf, d), lambda i, j: (layer, j, 0)),
            pl.BlockSpec((1, d), lambda i, j: (0, 0)),
        ],
        out_specs=pl.BlockSpec((tm, d), lambda i, j: (i, 0)),
        scratch_shapes=[pltpu.VMEM((tm + 2 * HALO, d), BF16), pltpu.VMEM((tm, d), F32)],
        compiler_params=_cparams(("parallel", "arbitrary")),
        name="ffn",
    )(x, x, x, g_pre, w_up, w_up, cw, cb, w_down, g_post)


def _mix_out_body(yh_ref, ya_ref, ym_ref, gg_ref, w_ref, gp_ref, x_ref, o_ref, *, dh, da):
    gg = gg_ref[...]
    acc = _dot(_rms(yh_ref[...], gg[:, :dh]).astype(BF16), w_ref[0:dh, :])
    acc += _dot(_rms(ya_ref[...], gg[:, dh:dh + da]).astype(BF16), w_ref[dh:dh + da, :])
    acc += _dot(_rms(ym_ref[...], gg[:, dh + da:]).astype(BF16), w_ref[dh + da:, :])
    o_ref[...] = x_ref[...] + _rms(acc, gp_ref[...])


def mix_out(yh, ya, ym, g_grp, w_out, layer, g_post, x, tm):
    t, d = x.shape
    dh, da, dm = yh.shape[1], ya.shape[1], ym.shape[1]
    body = functools.partial(_mix_out_body, dh=dh, da=da)
    return pl.pallas_call(
        body,
        out_shape=jax.ShapeDtypeStruct((t, d), F32),
        grid=(t // tm,),
        in_specs=[
            pl.BlockSpec((tm, dh), lambda i: (i, 0)),
            pl.BlockSpec((tm, da), lambda i: (i, 0)),
            pl.BlockSpec((tm, dm), lambda i: (i, 0)),
            pl.BlockSpec((1, d), lambda i: (0, 0)),
            pl.BlockSpec((None, d, d), lambda i: (layer, 0, 0)),
            pl.BlockSpec((1, d), lambda i: (0, 0)),
            pl.BlockSpec((tm, d), lambda i: (i, 0)),
        ],
        out_specs=pl.BlockSpec((tm, d), lambda i: (i, 0)),
        compiler_params=_cparams(("parallel",)),
        name="mix_out",
    )(yh, ya, ym, g_grp, w_out, g_post, x)


def _rope_tables(seq_len):
    half = ROT_DIM // 2
    inv_freq = np.exp(-math.log(ROPE_THETA) * np.arange(0, ROT_DIM, 2, dtype=np.float64) / ROT_DIM)
    ang = np.arange(seq_len, dtype=np.float64)[:, None] * inv_freq.astype(np.float32).astype(np.float64)[None, :]
    c, s = np.cos(ang), np.sin(ang)
    cos_t = np.ones((seq_len, HEAD_DIM), np.float64)
    sin_t = np.zeros((seq_len, HEAD_DIM), np.float64)
    cos_t[:, :half] = c
    cos_t[:, half:ROT_DIM] = c
    sin_t[:, :half] = -s
    sin_t[:, half:ROT_DIM] = s
    return cos_t.astype(np.float32), sin_t.astype(np.float32)


def _dil_attn_body(q_ref, k_ref, v_ref, cos_ref, sin_ref, o_ref, qs_ref, ks_ref, vs_ref, m_ref, l_ref, *, seq_len):
    half = ROT_DIM // 2
    scale = 1.0 / math.sqrt(HEAD_DIM)
    chunk = min(512, seq_len)

    def rope_chunk(c, carry):
        rows = pl.ds(pl.multiple_of(c * chunk, chunk), chunk)
        cs = cos_ref[rows, :]
        sn = sin_ref[rows, :]
        lane = lax.broadcasted_iota(jnp.int32, (chunk, HEAD_DIM), 1)
        for src, dst, mul in ((q_ref, qs_ref, scale), (k_ref, ks_ref, None)):
            x = src[rows, :].astype(F32)
            partner = jnp.where(lane < half, pltpu.roll(x, HEAD_DIM - half, 1), pltpu.roll(x, half, 1))
            y = x * cs + partner * sn
            dst[rows, :] = y if mul is None else y * mul
        vs_ref[rows, :] = v_ref[rows, :].astype(F32)
        return carry

    lax.fori_loop(0, seq_len // chunk, rope_chunk, 0)

    n_br = len(DILATED_PATTERNS)
    for bi, (window, dil) in enumerate(DILATED_PATTERNS):
        radius = window // (2 * dil)
        n = seq_len // dil
        tq = min(128, n)
        kw = min(tq + 2 * radius, n)
        nblk = n // tq

        def one_block(idx, bi=bi, dil=dil, radius=radius, n=n, tq=tq, kw=kw, nblk=nblk):
            r = idx // nblk
            q0 = (idx % nblk) * tq
            k0 = jnp.clip(q0 - radius, 0, n - kw)
            if dil == 1:
                qsl = pl.ds(pl.multiple_of(q0, tq), tq)
                ksl = pl.ds(pl.multiple_of(k0, 8), kw)
            else:
                qsl = pl.ds(r + q0 * dil, tq, stride=dil)
                ksl = pl.ds(r + k0 * dil, kw, stride=dil)
            qb = qs_ref[qsl, :].astype(BF16)
            kb = ks_ref[ksl, :].astype(BF16)
            vb = vs_ref[ksl, :].astype(BF16)
            s = lax.dot_general(qb, kb, (((1,), (1,)), ((), ())), preferred_element_type=F32)
            rel = (k0 + lax.broadcasted_iota(jnp.int32, (tq, kw), 1)) - (q0 + lax.broadcasted_iota(jnp.int32, (tq, kw), 0))
            s = jnp.where(jnp.abs(rel) <= radius, s, NEG)
            m = jnp.max(s, axis=-1, keepdims=True)
            p = jnp.exp(s - m)
            l = jnp.sum(p, axis=-1, keepdims=True)
            acc = _dot(p.astype(BF16), vb)
            m = jnp.broadcast_to(m, (tq, HEAD_DIM))
            l = jnp.broadcast_to(l, (tq, HEAD_DIM))
            if bi > 0:
                m_old = m_ref[qsl, :]
                m_new = jnp.maximum(m_old, m)
                a_old = jnp.exp(m_old - m_new)
                a_new = jnp.exp(m - m_new)
                acc = o_ref[qsl, :] * a_old + acc * a_new
                l = l_ref[qsl, :] * a_old + l * a_new
                m = m_new
            return qsl, m, l, acc

        total = dil * nblk
        unroll = min(ATTN_BLOCKS_PER_STEP, total)

        def step(it, carry, bi=bi, unroll=unroll, one_block=one_block):
            done = [one_block(it * unroll + u) for u in range(unroll)]
            for qsl, m, l, acc in done:
                if bi == n_br - 1:
                    o_ref[qsl, :] = acc / l
                else:
                    o_ref[qsl, :] = acc
                    m_ref[qsl, :] = m
                    l_ref[qsl, :] = l
            return carry

        lax.fori_loop(0, total // unroll, step, 0)


def dil_attn(proj, batch, seq_len, q_col, k_col, v_col, n_heads):
    cos_t, sin_t = _rope_tables(seq_len)
    nb = 1 if seq_len * HEAD_DIM * 4 > (2 << 20) else 2

    def col_spec(col):
        return pl.BlockSpec((seq_len, HEAD_DIM), lambda b, h: (b, col + h))

    tab_spec = pl.BlockSpec((seq_len, HEAD_DIM), lambda b, h: (0, 0), pipeline_mode=pl.Buffered(1))
    body = functools.partial(_dil_attn_body, seq_len=seq_len)
    return pl.pallas_call(
        body,
        out_shape=jax.ShapeDtypeStruct((batch * seq_len, n_heads * HEAD_DIM), F32),
        grid=(batch, n_heads),
        in_specs=[col_spec(q_col), col_spec(k_col), col_spec(v_col), tab_spec, tab_spec],
        out_specs=pl.BlockSpec((seq_len, HEAD_DIM), lambda b, h: (b, h), pipeline_mode=pl.Buffered(nb)),
        scratch_shapes=[pltpu.VMEM((seq_len, HEAD_DIM), F32)] * 5,
        compiler_params=_cparams(("parallel", "parallel")),
        name="dil_attn",
    )(proj, proj, proj, cos_t, sin_t)


def _mem_attn_body(q_ref, k_ref, v_ref, o_ref):
    s = lax.dot_general(q_ref[...].astype(BF16), k_ref[...].astype(BF16), (((1,), (1,)), ((), ())),
                        preferred_element_type=F32) * (1.0 / math.sqrt(HEAD_DIM))
    m = jnp.max(s, axis=-1, keepdims=True)
    p = jnp.exp(s - m)
    l = jnp.sum(p, axis=-1, keepdims=True)
    o_ref[...] = _dot(p.astype(BF16), v_ref[...].astype(BF16)) / l


def mem_attn(proj, kv, batch, seq_len, n_mem, q_col, n_heads, tq):
    nq = seq_len // tq
    return pl.pallas_call(
        _mem_attn_body,
        out_shape=jax.ShapeDtypeStruct((batch * seq_len, n_heads * HEAD_DIM), F32),
        grid=(batch, nq, n_heads),
        in_specs=[
            pl.BlockSpec((tq, HEAD_DIM), lambda b, i, h: (b * nq + i, q_col + h)),
            pl.BlockSpec((n_mem, HEAD_DIM), lambda b, i, h: (b, h)),
            pl.BlockSpec((n_mem, HEAD_DIM), lambda b, i, h: (b, n_heads + h)),
        ],
        out_specs=pl.BlockSpec((tq, HEAD_DIM), lambda b, i, h: (b * nq + i, h)),
        compiler_params=_cparams(("parallel", "parallel", "parallel")),
        name="mem_attn",
    )(proj, kv, kv)


def _fft_split(seq_len):
    n = 2 * seq_len
    n2 = 128
    return n // n2, n2


@functools.lru_cache(maxsize=None)
def _fft_consts(seq_len):
    n1, n2 = _fft_split(seq_len)
    n = n1 * n2
    kh = n1 // 2 + 1
    a_out = n1 // 2
    a = np.arange(n1)
    k1 = np.arange(kh)
    th = 2.0 * np.pi * ((k1[:, None] * a[None, :]) % n1) / n1
    fwd = np.stack([np.cos(th), -np.sin(th)], axis=1).reshape(2 * kh, n1)
    alpha = np.where((k1 == 0) | (k1 == n1 // 2), 1.0, 2.0) / n
    th_i = th[:, :a_out].T
    inv = np.stack([np.cos(th_i) * alpha[None, :], -np.sin(th_i) * alpha[None, :]], axis=2)
    inv = inv.reshape(a_out, 2 * kh)
    b = np.arange(n2)
    k2 = np.arange(n2)
    kk = k1[:, None, None] + n1 * k2[None, :, None]
    ph = 2.0 * np.pi * ((kk * b[None, None, :]) % n) / n
    wr, wi = np.cos(ph), -np.sin(ph)
    vr, vi = np.transpose(wr, (0, 2, 1)), -np.transpose(wi, (0, 2, 1))

    def block(re, im):
        blk = np.concatenate([np.concatenate([re, -im], axis=2), np.concatenate([im, re], axis=2)], axis=1)
        return np.asarray(blk, np.float32).astype(BF16)

    def kron_bf16(m, sub):
        return np.asarray(np.kron(m, np.eye(sub)), np.float32).astype(BF16)

    return dict(n1=n1, n2=n2, kh=kh, a_out=a_out,
                a_fwd=kron_bf16(fwd, SUB),
                a_fwd_half=kron_bf16(fwd[:, :a_out], SUB_BF16),
                a_inv=kron_bf16(inv, SUB_BF16),
                w_fwd=block(wr, wi), w_inv=block(vr, vi))


def _filter_tables(seq_len, channels):
    n = 2 * seq_len
    bands = (FILTER_EMB - 1) // 2
    pos = np.concatenate([np.arange(seq_len), [0], np.arange(seq_len - 1, 0, -1)])
    t = np.linspace(0.0, 1.0, seq_len)[pos]
    w = 2.0 * np.pi * pos / seq_len
    f = np.linspace(1e-4, bands - 1, bands)
    ang = w[:, None] * f[None, :]
    z = np.zeros((n, FILTER_HIDDEN), np.float64)
    z[:, 0] = t
    z[:, 1:1 + bands] = np.cos(ang)
    z[:, 1 + bands:1 + 2 * bands] = -np.sin(ang)
    deltas = np.abs(np.linspace(math.log(DECAY_TARGET) / DECAY_SLOW, math.log(DECAY_TARGET) / DECAY_FAST, channels))
    return z.astype(np.float32), deltas[None, :].astype(np.float32)


def _filter_body(z_ref, w1_ref, b1_ref, w2_ref, b2_ref, w3_ref, b3_ref, fr_ref, w4_ref, dl_ref,
                 k_ref, s_ref, *, tr, seq_len, channels):
    i = pl.program_id(0)
    fr = fr_ref[...]
    z = z_ref[...]

    def layer(h, w_ref, b_ref):
        w_hi, w_lo = _split(w_ref[...])
        h_hi, h_lo = _split(h)
        pre = _dot(h_hi, w_hi) + _dot(h_lo, w_hi) + _dot(h_hi, w_lo)
        return jnp.sin(fr * (pre + b_ref[...]))

    h = layer(z, w1_ref, b1_ref)
    h = layer(h, w2_ref, b2_ref)
    h = layer(h, w3_ref, b3_ref)
    w_hi, w_lo = _split(w4_ref[...])
    h_hi, h_lo = _split(h)
    k = _dot(h_hi, w_hi) + _dot(h_lo, w_hi) + _dot(h_hi, w_lo)
    decay = jnp.exp(-z[:, 0:1] * dl_ref[...]) + DECAY_SHIFT
    row = i * tr + lax.broadcasted_iota(jnp.int32, (tr, 1), 0)
    decay = jnp.where(row == seq_len, 0.0, decay)
    k = k * jnp.concatenate([decay, decay], axis=1)
    k_ref[...] = k

    @pl.when(i == 0)
    def _():
        s_ref[...] = jnp.zeros_like(s_ref)

    s_ref[...] += jnp.sum(jnp.abs(k).reshape(tr // 8, 8, 2 * channels), axis=0)


def filter_gen(seq_len, channels, w1p, b1, w2, b2, w3, b3, freq, w4d):
    n = 2 * seq_len
    tr = min(512, seq_len)
    z, deltas = _filter_tables(seq_len, channels)
    hid = FILTER_HIDDEN
    full = lambda shape: pl.BlockSpec(shape, lambda i: (0,) * len(shape))
    body = functools.partial(_filter_body, tr=tr, seq_len=seq_len, channels=channels)
    return pl.pallas_call(
        body,
        out_shape=(jax.ShapeDtypeStruct((n, 2 * channels), F32), jax.ShapeDtypeStruct((8, 2 * channels), F32)),
        grid=(n // tr,),
        in_specs=[
            pl.BlockSpec((tr, hid), lambda i: (i, 0)),
            full((hid, hid)), full((1, hid)), full((hid, hid)), full((1, hid)), full((hid, hid)), full((1, hid)),
            full((1, hid)),
            pl.BlockSpec((None, hid, 2 * channels), lambda i: ((i * tr) // seq_len, 0, 0)),
            full((1, channels)),
        ],
        out_specs=(pl.BlockSpec((tr, 2 * channels), lambda i: (i, 0)), full((8, 2 * channels))),
        compiler_params=_cparams(("arbitrary",)),
        name="filter_gen",
    )(z, w1p, b1, w2, b2, w3, b3, freq, w4d, deltas)


def _major_fwd_body(u_ref, a_ref, y_ref, *, groups, sub):
    a_in, _, ct = u_ref.shape
    kh = y_ref.shape[0]
    for q in range(groups):
        rows = slice(q * sub, (q + 1) * sub)
        u = u_ref[:, rows, :].reshape(a_in * sub, ct)
        y = _dot(a_ref[...], u.astype(BF16))
        y_ref[:, :, rows, :] = y.reshape(kh, 2, sub, ct).astype(y_ref.dtype)


def major_fwd(u, col, channels, a_fwd, kh, ct, groups, sub, out_dtype):
    bsz, a_in, n2, _ = u.shape
    nc = channels // ct
    rows = groups * sub
    body = functools.partial(_major_fwd_body, groups=groups, sub=sub)
    return pl.pallas_call(
        body,
        out_shape=jax.ShapeDtypeStruct((bsz, kh, 2, n2, channels), out_dtype),
        grid=(bsz, n2 // rows, nc),
        in_specs=[
            pl.BlockSpec((None, a_in, rows, ct), lambda b, r, j: (b, 0, r, col * nc + j)),
            pl.BlockSpec(a_fwd.shape, lambda b, r, j: (0, 0)),
        ],
        out_specs=pl.BlockSpec((None, kh, 2, rows, ct), lambda b, r, j: (b, 0, 0, r, j)),
        compiler_params=_cparams(("parallel", "parallel", "parallel")),
        name="major_fwd",
    )(u, a_fwd)


def _major_inv_body(g_ref, a_ref, m_ref, o_ref, *, groups, sub):
    kh, _, _, ct = g_ref.shape
    a_out = o_ref.shape[0]
    for q in range(groups):
        rows = slice(q * sub, (q + 1) * sub)
        g = g_ref[:, :, rows, :].reshape(kh * 2 * sub, ct)
        y = _dot(a_ref[...], g).reshape(a_out, sub, ct)
        o_ref[:, rows, :] = (y * m_ref[:, rows, :].astype(F32)).astype(o_ref.dtype)


def major_inv(g, a_inv, mult, col, ct, groups, sub, out_dtype):
    bsz, kh, _, n2, channels = g.shape
    a_out = mult.shape[1]
    nc = channels // ct
    rows = groups * sub
    body = functools.partial(_major_inv_body, groups=groups, sub=sub)
    return pl.pallas_call(
        body,
        out_shape=jax.ShapeDtypeStruct((bsz, a_out, n2, channels), out_dtype),
        grid=(bsz, n2 // rows, nc),
        in_specs=[
            pl.BlockSpec((None, kh, 2, rows, ct), lambda b, r, j: (b, 0, 0, r, j)),
            pl.BlockSpec(a_inv.shape, lambda b, r, j: (0, 0)),
            pl.BlockSpec((None, a_out, rows, ct), lambda b, r, j: (b, 0, r, col * nc + j)),
        ],
        out_specs=pl.BlockSpec((None, a_out, rows, ct), lambda b, r, j: (b, 0, r, j)),
        compiler_params=_cparams(("parallel", "parallel", "parallel")),
        name="major_inv",
    )(g, a_inv, mult)


def _minor_conv_body(y_ref, wf_ref, wi_ref, k_ref, g_ref, *, n2):
    kr = k_ref[0]
    ki = k_ref[1]
    for i in range(y_ref.shape[0]):
        z = _dot(wf_ref[...], jnp.concatenate([y_ref[i, 0], y_ref[i, 1]], axis=0))
        zr, zi = z[:n2], z[n2:]
        pr = (zr * kr - zi * ki).astype(BF16)
        pi = (zr * ki + zi * kr).astype(BF16)
        g = _dot(wi_ref[...], jnp.concatenate([pr, pi], axis=0))
        g_ref[i, 0] = g[:n2].astype(g_ref.dtype)
        g_ref[i, 1] = g[n2:].astype(g_ref.dtype)


def minor_conv(y, w_fwd, w_inv, spec, order, ct, bb):
    bsz, kh, _, n2, c = y.shape
    nc = c // ct
    y_spec = pl.BlockSpec((bb, None, 2, n2, ct), lambda k, j, b: (b, k, 0, 0, j))
    w_spec = pl.BlockSpec((None, 2 * n2, 2 * n2), lambda k, j, b: (k, 0, 0))
    k_spec = pl.BlockSpec((None, 2, n2, ct), lambda k, j, b: (k, 0, 0, order * nc + j))
    body = functools.partial(_minor_conv_body, n2=n2)
    return pl.pallas_call(
        body,
        out_shape=jax.ShapeDtypeStruct(y.shape, y.dtype),
        grid=(kh, nc, bsz // bb),
        in_specs=[y_spec, w_spec, w_spec, k_spec],
        out_specs=y_spec,
        compiler_params=_cparams(("parallel", "parallel", "parallel")),
        name="minor_conv",
    )(y, w_fwd, w_inv, spec)


def _minor_filter_body(y_ref, wf_ref, inv_ref, bias_ref, k_ref, *, n2):
    z = _dot(wf_ref[...], jnp.concatenate([y_ref[0].astype(BF16), y_ref[1].astype(BF16)], axis=0))
    inv = inv_ref[...]
    k_ref[0] = z[:n2] * inv + bias_ref[...]
    k_ref[1] = z[n2:] * inv


def minor_filter(y, w_fwd, inv_norm, bias, ct):
    kh, _, n2, c2 = y.shape
    y_spec = pl.BlockSpec((None, 2, n2, ct), lambda k, j: (k, 0, 0, j))
    v_spec = pl.BlockSpec((1, ct), lambda k, j: (0, j))
    body = functools.partial(_minor_filter_body, n2=n2)
    return pl.pallas_call(
        body,
        out_shape=jax.ShapeDtypeStruct(y.shape, F32),
        grid=(kh, c2 // ct),
        in_specs=[y_spec, pl.BlockSpec((None, 2 * n2, 2 * n2), lambda k, j: (k, 0, 0)), v_spec, v_spec],
        out_specs=y_spec,
        compiler_params=_cparams(("parallel", "parallel")),
        name="minor_filter",
    )(y, w_fwd, inv_norm, bias)


def _major_groups(a_in, sub, n2):
    return min(n2 // sub, max(1, 2048 // (a_in * sub)))


def hyena_filter_spectra(seq_len, channels, fw, f_bias):
    cst = _fft_consts(seq_len)
    n1, n2, kh = cst["n1"], cst["n2"], cst["kh"]
    kern, sums = filter_gen(seq_len, channels, *fw)
    inv_norm = 1.0 / jnp.sum(sums, axis=0, keepdims=True)
    u = kern.reshape(1, n1, n2, 2 * channels)
    y = major_fwd(u, 0, 2 * channels, cst["a_fwd"], kh, ct=512, groups=_major_groups(n1, SUB, n2), sub=SUB,
                  out_dtype=F32)
    return minor_filter(y[0], cst["w_fwd"], inv_norm, f_bias.reshape(1, 2 * channels), ct=1024)


def long_conv_gated(u, u_col, mult, mult_col, spectra, order, seq_len, channels, out_dtype):
    cst = _fft_consts(seq_len)
    groups = _major_groups(cst["a_out"], SUB_BF16, cst["n2"])
    y = major_fwd(u, u_col, channels, cst["a_fwd_half"], cst["kh"], ct=512, groups=groups, sub=SUB_BF16,
                  out_dtype=BF16)
    g = minor_conv(y, cst["w_fwd"], cst["w_inv"], spectra, order, ct=channels, bb=min(u.shape[0], 4))
    return major_inv(g, cst["a_inv"], mult, mult_col, ct=512, groups=groups, sub=SUB_BF16, out_dtype=out_dtype)


def _row_tile(seq_len, want):
    return min(want, seq_len)


def _layer(x, mem, lw, ws, layer, batch, seq_len, dims):
    d, dh, da, dm = dims
    n_heads = da // HEAD_DIM
    n_mem = mem.shape[0] // batch

    proj = norm_matmul_conv(x, lw["g_pre_mix"], ws["w_in"], layer, lw["conv_w"], lw["conv_b"], seq_len,
                            tm=_row_tile(seq_len, 1024), tn=512)

    cst = _fft_consts(seq_len)
    p4 = proj.reshape(batch, cst["a_out"], cst["n2"], proj.shape[1])
    spectra = lw["spectra"][seq_len]
    z = long_conv_gated(p4, 0, p4, 1, spectra, 0, seq_len, dh, BF16)
    y_h = long_conv_gated(z, 0, p4, 2, spectra, 1, seq_len, dh, F32).reshape(batch * seq_len, dh)

    qcol = 3 * dh // HEAD_DIM
    y_a = dil_attn(proj, batch, seq_len, qcol, qcol + n_heads, qcol + 2 * n_heads, n_heads)
    kv = norm_matmul(mem, lw["g_mem"], ws["w_mem_kv"], layer, tm=n_mem, tn=512)
    y_m = mem_attn(proj, kv, batch, seq_len, n_mem, qcol + 3 * n_heads, dm // HEAD_DIM, tq=_row_tile(seq_len, 1024))

    x = mix_out(y_h, y_a, y_m, lw["g_grp"], ws["w_out"], layer, lw["g_post_mix"], x, _row_tile(seq_len, 512))
    return ffn(x, lw["g_pre_ffn"], ws["w_up"], lw["ffn_conv_w"], lw["ffn_conv_b"], ws["w_down"],
               lw["g_post_ffn"], layer, seq_len, _row_tile(seq_len, 512), tf=512)


def kernel(x_prompt, x_sample, mem_prompt, mem_sample, g_pre_mix, w_in, conv_w, conv_b, f_w1, f_b1, f_w2, f_b2, f_w3, f_b3, f_w4, f_freq, f_bias, g_mem, w_mem_kv, g_grp, w_out, g_post_mix, g_pre_ffn, w_up, ffn_conv_w, ffn_conv_b, w_down, g_post_ffn):
    depth, d, d_in = w_in.shape
    dh = conv_w.shape[2] // 3
    dm = w_mem_kv.shape[2] // 2
    da = (d_in - 3 * dh - dm) // 3
    dims = (d, dh, da, dm)
    groups = [(x_prompt, mem_prompt), (x_sample, mem_sample)]
    seq_lens = sorted({g[0].shape[1] for g in groups})

    xs = [g[0].reshape(-1, d) for g in groups]
    mems = [g[1].reshape(-1, d) for g in groups]
    hid = FILTER_HIDDEN
    ws = dict(w_in=w_in.astype(BF16), w_mem_kv=w_mem_kv.astype(BF16), w_out=w_out.astype(BF16),
              w_up=w_up.astype(BF16), w_down=w_down.astype(BF16))
    for i in range(depth):
        row = lambda v: v[i][None, :]
        w1p = jnp.zeros((hid, hid), F32).at[:FILTER_EMB].set(f_w1[i])
        w4d = f_w4[i].reshape(hid, 2, 2, dh).transpose(2, 0, 1, 3).reshape(2, hid, 2 * dh)
        fw = (w1p, row(f_b1), f_w2[i], row(f_b2), f_w3[i], row(f_b3), row(f_freq), w4d)
        lw = dict(
            g_pre_mix=row(g_pre_mix), conv_w=conv_w[i], conv_b=row(conv_b),
            g_mem=row(g_mem), g_grp=row(g_grp), g_post_mix=row(g_post_mix), g_pre_ffn=row(g_pre_ffn),
            ffn_conv_w=ffn_conv_w[i], ffn_conv_b=row(ffn_conv_b), g_post_ffn=row(g_post_ffn),
            spectra={sl: hyena_filter_spectra(sl, dh, fw, f_bias[i]) for sl in seq_lens},
        )
        for gi, (xg, _) in enumerate(groups):
            xs[gi] = _layer(xs[gi], mems[gi], lw, ws, i, xg.shape[0], xg.shape[1], dims)
    return tuple(x.reshape(g[0].shape) for x, g in zip(xs, groups))
```

```python
import functools
import math

import numpy as np
import jax
import jax.numpy as jnp
from jax import lax
from jax.experimental import pallas as pl
from jax.experimental.pallas import tpu as pltpu

F32 = jnp.float32
BF16 = jnp.bfloat16

HEAD_DIM = 128
DILATED_PATTERNS = ((128, 1), (512, 4), (2048, 16))
ROPE_THETA = 500000.0
ROT_DIM = HEAD_DIM // 4
FILTER_EMB = 33
FILTER_HIDDEN = 64
DECAY_FAST = 0.3
DECAY_SLOW = 1.5
DECAY_TARGET = 1e-2
DECAY_SHIFT = 0.05
EPS = 1e-6
NEG = -1e30

HALO = 16
ATTN_BLOCKS_PER_STEP = 8
SUB = 8
SUB_BF16 = 16
VMEM_LIMIT = 56 * 1024 * 1024


def _cparams(sem):
    return pltpu.CompilerParams(dimension_semantics=sem, vmem_limit_bytes=VMEM_LIMIT)


def _split(x):
    hi = x.astype(BF16)
    lo = (x - hi.astype(F32)).astype(BF16)
    return hi, lo


def _dot(a, b):
    return jnp.dot(a, b, preferred_element_type=F32)


def _rms(v, g):
    return v * lax.rsqrt(jnp.mean(v * v, axis=-1, keepdims=True) + EPS) * g


def _fill_normed(hb_ref, xp_ref, x_ref, xn_ref, g_ref, i, tm, blocks_per_seq):
    g = g_ref[...]
    pos = i % blocks_per_seq
    keep_p = (pos != 0).astype(F32)
    keep_n = (pos != blocks_per_seq - 1).astype(F32)
    hb_ref[0:HALO, :] = (_rms(xp_ref[...], g) * keep_p).astype(BF16)
    hb_ref[HALO:HALO + tm, :] = _rms(x_ref[...], g).astype(BF16)
    hb_ref[HALO + tm:, :] = (_rms(xn_ref[...], g) * keep_n).astype(BF16)


def _conv3(u, cw, cb, tm):
    rows = tm + 2 * HALO
    up = pltpu.roll(u, 1, 0)[HALO:HALO + tm]
    un = pltpu.roll(u, rows - 1, 0)[HALO:HALO + tm]
    uc = u[HALO:HALO + tm]
    return up * cw[0:1] + uc * cw[1:2] + un * cw[2:3] + cb


def _nm_conv_body(xp_ref, x_ref, xn_ref, g_ref, w_ref, cw_ref, cb_ref, o_ref, hb_ref, *, tm, blocks_per_seq,
                  n_conv):
    i = pl.program_id(0)
    j = pl.program_id(1)

    @pl.when(j == 0)
    def _():
        _fill_normed(hb_ref, xp_ref, x_ref, xn_ref, g_ref, i, tm, blocks_per_seq)

    u = _dot(hb_ref[...], w_ref[...])

    @pl.when(j < n_conv)
    def _():
        o_ref[...] = _conv3(u, cw_ref[...], cb_ref[...], tm).astype(o_ref.dtype)

    @pl.when(j >= n_conv)
    def _():
        o_ref[...] = u[HALO:HALO + tm].astype(o_ref.dtype)


def _nm_body(x_ref, g_ref, w_ref, o_ref, hb_ref):
    @pl.when(pl.program_id(1) == 0)
    def _():
        hb_ref[...] = _rms(x_ref[...], g_ref[...]).astype(BF16)

    o_ref[...] = _dot(hb_ref[...], w_ref[...]).astype(o_ref.dtype)


def _halo_specs(tm, d, n_rows, x_buffers=2):
    hb = tm // HALO
    last = n_rows // HALO - 1
    return [
        pl.BlockSpec((HALO, d), lambda i, j: (jnp.maximum(i * hb - 1, 0), 0)),
        pl.BlockSpec((tm, d), lambda i, j: (i, 0), pipeline_mode=pl.Buffered(x_buffers)),
        pl.BlockSpec((HALO, d), lambda i, j: (jnp.minimum((i + 1) * hb, last), 0)),
    ]


def norm_matmul_conv(x, g, w, layer, cw, cb, seq_len, tm, tn):
    t, d = x.shape
    n = w.shape[2]
    n_conv = cw.shape[1] // tn
    body = functools.partial(_nm_conv_body, tm=tm, blocks_per_seq=seq_len // tm, n_conv=n_conv)
    return pl.pallas_call(
        body,
        out_shape=jax.ShapeDtypeStruct((t, n), BF16),
        grid=(t // tm, n // tn),
        in_specs=_halo_specs(tm, d, t) + [
            pl.BlockSpec((1, d), lambda i, j: (0, 0)),
            pl.BlockSpec((None, d, tn), lambda i, j: (layer, 0, j)),
            pl.BlockSpec((3, tn), lambda i, j: (0, jnp.minimum(j, n_conv - 1))),
            pl.BlockSpec((1, tn), lambda i, j: (0, jnp.minimum(j, n_conv - 1))),
        ],
        out_specs=pl.BlockSpec((tm, tn), lambda i, j: (i, j)),
        scratch_shapes=[pltpu.VMEM((tm + 2 * HALO, d), BF16)],
        compiler_params=_cparams(("parallel", "arbitrary")),
        name="norm_matmul_conv",
    )(x, x, x, g, w, cw, cb)


def norm_matmul(x, g, w, layer, tm, tn):
    t, d = x.shape
    n = w.shape[2]
    return pl.pallas_call(
        _nm_body,
        out_shape=jax.ShapeDtypeStruct((t, n), BF16),
        grid=(t // tm, n // tn),
        in_specs=[
            pl.BlockSpec((tm, d), lambda i, j: (i, 0)),
            pl.BlockSpec((1, d), lambda i, j: (0, 0)),
            pl.BlockSpec((None, d, tn), lambda i, j: (layer, 0, j)),
        ],
        out_specs=pl.BlockSpec((tm, tn), lambda i, j: (i, j)),
        scratch_shapes=[pltpu.VMEM((tm, d), BF16)],
        compiler_params=_cparams(("parallel", "arbitrary")),
        name="norm_matmul",
    )(x, g, w)


def _ffn_body(xp_ref, x_ref, xn_ref, g_ref, wg_ref, wv_ref, cw_ref, cb_ref, wd_ref, gp_ref, o_ref,
              hb_ref, *, tm, blocks_per_seq, nj):
    i = pl.program_id(0)
    j = pl.program_id(1)

    @pl.when(j == 0)
    def _():
        _fill_normed(hb_ref, xp_ref, x_ref, xn_ref, g_ref, i, tm, blocks_per_seq)
        o_ref[...] = jnp.zeros_like(o_ref)

    ug = _dot(hb_ref[...], wg_ref[...])
    val = _dot(hb_ref[HALO:HALO + tm, :], wv_ref[...])
    gate = _conv3(ug, cw_ref[...], cb_ref[...], tm)
    ff = jax.nn.gelu(gate, approximate=True) * val
    o_ref[...] += _dot(ff.astype(BF16), wd_ref[...])

    @pl.when(j == nj - 1)
    def _():
        o_ref[...] = x_ref[...] + _rms(o_ref[...], gp_ref[...])


def ffn(x, g_pre, w_up, cw, cb, w_down, g_post, layer, seq_len, tm, tf):
    t, d = x.shape
    d_ff = w_down.shape[1]
    nj = d_ff // tf
    body = functools.partial(_ffn_body, tm=tm, blocks_per_seq=seq_len // tm, nj=nj)
    return pl.pallas_call(
        body,
        out_shape=jax.ShapeDtypeStruct((t, d), F32),
        grid=(t // tm, nj),
        in_specs=_halo_specs(tm, d, t, x_buffers=1) + [
            pl.BlockSpec((1, d), lambda i, j: (0, 0)),
            pl.BlockSpec((None, d, tf), lambda i, j: (layer, 0, j)),
            pl.BlockSpec((None, d, tf), lambda i, j: (layer, 0, j + nj)),
            pl.BlockSpec((3, tf), lambda i, j: (0, j)),
            pl.BlockSpec((1, tf), lambda i, j: (0, j)),
            pl.BlockSpec((None, tf, d), lambda i, j: (layer, j, 0)),
            pl.BlockSpec((1, d), lambda i, j: (0, 0)),
        ],
        out_specs=pl.BlockSpec((tm, d), lambda i, j: (i, 0)),
        scratch_shapes=[pltpu.VMEM((tm + 2 * HALO, d), BF16)],
        compiler_params=_cparams(("parallel", "arbitrary")),
        name="ffn",
    )(x, x, x, g_pre, w_up, w_up, cw, cb, w_down, g_post)


def _mix_out_body(yh_ref, ya_ref, ym_ref, gg_ref, w_ref, gp_ref, x_ref, o_ref, *, dh, da):
    gg = gg_ref[...]
    acc = _dot(_rms(yh_ref[...].astype(F32), gg[:, :dh]).astype(BF16), w_ref[0:dh, :])
    acc += _dot(_rms(ya_ref[...].astype(F32), gg[:, dh:dh + da]).astype(BF16), w_ref[dh:dh + da, :])
    acc += _dot(_rms(ym_ref[...].astype(F32), gg[:, dh + da:]).astype(BF16), w_ref[dh + da:, :])
    o_ref[...] = x_ref[...] + _rms(acc, gp_ref[...])


def mix_out(yh, ya, ym, g_grp, w_out, layer, g_post, x, tm):
    t, d = x.shape
    dh, da, dm = yh.shape[1], ya.shape[1], ym.shape[1]
    body = functools.partial(_mix_out_body, dh=dh, da=da)
    return pl.pallas_call(
        body,
        out_shape=jax.ShapeDtypeStruct((t, d), F32),
        grid=(t // tm,),
        in_specs=[
            pl.BlockSpec((tm, dh), lambda i: (i, 0)),
            pl.BlockSpec((tm, da), lambda i: (i, 0)),
            pl.BlockSpec((tm, dm), lambda i: (i, 0)),
            pl.BlockSpec((1, d), lambda i: (0, 0)),
            pl.BlockSpec((None, d, d), lambda i: (layer, 0, 0)),
            pl.BlockSpec((1, d), lambda i: (0, 0)),
            pl.BlockSpec((tm, d), lambda i: (i, 0)),
        ],
        out_specs=pl.BlockSpec((tm, d), lambda i: (i, 0)),
        compiler_params=_cparams(("parallel",)),
        name="mix_out",
    )(yh, ya, ym, g_grp, w_out, g_post, x)


def _rope_tables(seq_len):
    half = ROT_DIM // 2
    inv_freq = np.exp(-math.log(ROPE_THETA) * np.arange(0, ROT_DIM, 2, dtype=np.float64) / ROT_DIM)
    ang = np.arange(seq_len, dtype=np.float64)[:, None] * inv_freq.astype(np.float32).astype(np.float64)[None, :]
    c, s = np.cos(ang), np.sin(ang)
    cos_t = np.ones((seq_len, HEAD_DIM), np.float64)
    sin_t = np.zeros((seq_len, HEAD_DIM), np.float64)
    cos_t[:, :half] = c
    cos_t[:, half:ROT_DIM] = c
    sin_t[:, :half] = -s
    sin_t[:, half:ROT_DIM] = s
    return cos_t.astype(np.float32), sin_t.astype(np.float32)


def _dil_attn_body(q_ref, k_ref, v_ref, cos_ref, sin_ref, o_ref, qs_ref, ks_ref, vs_ref, m_ref, l_ref, *, seq_len):
    half = ROT_DIM // 2
    scale = 1.0 / math.sqrt(HEAD_DIM)
    chunk = min(512, seq_len)

    def rope_chunk(c, carry):
        rows = pl.ds(pl.multiple_of(c * chunk, chunk), chunk)
        cs = cos_ref[rows, :]
        sn = sin_ref[rows, :]
        lane = lax.broadcasted_iota(jnp.int32, (chunk, HEAD_DIM), 1)
        for src, dst, mul in ((q_ref, qs_ref, scale), (k_ref, ks_ref, None)):
            x = src[rows, :].astype(F32)
            partner = jnp.where(lane < half, pltpu.roll(x, HEAD_DIM - half, 1), pltpu.roll(x, half, 1))
            y = x * cs + partner * sn
            dst[rows, :] = y if mul is None else y * mul
        vs_ref[rows, :] = v_ref[rows, :].astype(F32)
        return carry

    lax.fori_loop(0, seq_len // chunk, rope_chunk, 0)

    n_br = len(DILATED_PATTERNS)
    for bi, (window, dil) in enumerate(DILATED_PATTERNS):
        radius = window // (2 * dil)
        n = seq_len // dil
        tq = min(128, n)
        kw = min(tq + 2 * radius, n)
        nblk = n // tq

        def one_block(idx, bi=bi, dil=dil, radius=radius, n=n, tq=tq, kw=kw, nblk=nblk):
            r = idx // nblk
            q0 = (idx % nblk) * tq
            k0 = jnp.clip(q0 - radius, 0, n - kw)
            if dil == 1:
                qsl = pl.ds(pl.multiple_of(q0, tq), tq)
                ksl = pl.ds(pl.multiple_of(k0, 8), kw)
            else:
                qsl = pl.ds(r + q0 * dil, tq, stride=dil)
                ksl = pl.ds(r + k0 * dil, kw, stride=dil)
            qb = qs_ref[qsl, :].astype(BF16)
            kb = ks_ref[ksl, :].astype(BF16)
            vb = vs_ref[ksl, :].astype(BF16)
            s = lax.dot_general(qb, kb, (((1,), (1,)), ((), ())), preferred_element_type=F32)
            rel = (k0 + lax.broadcasted_iota(jnp.int32, (tq, kw), 1)) - (q0 + lax.broadcasted_iota(jnp.int32, (tq, kw), 0))
            s = jnp.where(jnp.abs(rel) <= radius, s, NEG)
            m = jnp.max(s, axis=-1, keepdims=True)
            p = jnp.exp(s - m)
            l = jnp.sum(p, axis=-1, keepdims=True)
            acc = _dot(p.astype(BF16), vb)
            m = jnp.broadcast_to(m, (tq, HEAD_DIM))
            l = jnp.broadcast_to(l, (tq, HEAD_DIM))
            if bi > 0:
                m_old = m_ref[qsl, :]
                m_new = jnp.maximum(m_old, m)
                a_old = jnp.exp(m_old - m_new)
                a_new = jnp.exp(m - m_new)
                acc = o_ref[qsl, :] * a_old + acc * a_new
                l = l_ref[qsl, :] * a_old + l * a_new
                m = m_new
            return qsl, m, l, acc

        total = dil * nblk
        unroll = min(ATTN_BLOCKS_PER_STEP, total)

        def step(it, carry, bi=bi, unroll=unroll, one_block=one_block):
            done = [one_block(it * unroll + u) for u in range(unroll)]
            for qsl, m, l, acc in done:
                if bi == n_br - 1:
                    o_ref[qsl, :] = acc / l
                else:
                    o_ref[qsl, :] = acc
                    m_ref[qsl, :] = m
                    l_ref[qsl, :] = l
            return carry

        lax.fori_loop(0, total // unroll, step, 0)


def dil_attn(proj, batch, seq_len, q_col, k_col, v_col, n_heads):
    cos_t, sin_t = _rope_tables(seq_len)
    nb = 1 if seq_len * HEAD_DIM * 4 > (2 << 20) else 2

    def col_spec(col):
        return pl.BlockSpec((seq_len, HEAD_DIM), lambda b, h: (b, col + h))

    tab_spec = pl.BlockSpec((seq_len, HEAD_DIM), lambda b, h: (0, 0), pipeline_mode=pl.Buffered(1))
    body = functools.partial(_dil_attn_body, seq_len=seq_len)
    return pl.pallas_call(
        body,
        out_shape=jax.ShapeDtypeStruct((batch * seq_len, n_heads * HEAD_DIM), F32),
        grid=(batch, n_heads),
        in_specs=[col_spec(q_col), col_spec(k_col), col_spec(v_col), tab_spec, tab_spec],
        out_specs=pl.BlockSpec((seq_len, HEAD_DIM), lambda b, h: (b, h), pipeline_mode=pl.Buffered(nb)),
        scratch_shapes=[pltpu.VMEM((seq_len, HEAD_DIM), F32)] * 5,
        compiler_params=_cparams(("parallel", "parallel")),
        name="dil_attn",
    )(proj, proj, proj, cos_t, sin_t)


def _mem_attn_body(q_ref, k_ref, v_ref, o_ref):
    s = lax.dot_general(q_ref[...].astype(BF16), k_ref[...].astype(BF16), (((1,), (1,)), ((), ())),
                        preferred_element_type=F32) * (1.0 / math.sqrt(HEAD_DIM))
    m = jnp.max(s, axis=-1, keepdims=True)
    p = jnp.exp(s - m)
    l = jnp.sum(p, axis=-1, keepdims=True)
    o_ref[...] = (_dot(p.astype(BF16), v_ref[...].astype(BF16)) / l).astype(o_ref.dtype)


def mem_attn(proj, kv, batch, seq_len, n_mem, q_col, n_heads, tq):
    nq = seq_len // tq
    return pl.pallas_call(
        _mem_attn_body,
        out_shape=jax.ShapeDtypeStruct((batch * seq_len, n_heads * HEAD_DIM), BF16),
        grid=(batch, nq, n_heads),
        in_specs=[
            pl.BlockSpec((tq, HEAD_DIM), lambda b, i, h: (b * nq + i, q_col + h)),
            pl.BlockSpec((n_mem, HEAD_DIM), lambda b, i, h: (b, h)),
            pl.BlockSpec((n_mem, HEAD_DIM), lambda b, i, h: (b, n_heads + h)),
        ],
        out_specs=pl.BlockSpec((tq, HEAD_DIM), lambda b, i, h: (b * nq + i, h)),
        compiler_params=_cparams(("parallel", "parallel", "parallel")),
        name="mem_attn",
    )(proj, kv, kv)


def _fft_split(seq_len):
    n = 2 * seq_len
    n2 = 128
    return n // n2, n2


@functools.lru_cache(maxsize=None)
def _fft_consts(seq_len):
    n1, n2 = _fft_split(seq_len)
    n = n1 * n2
    kh = n1 // 2 + 1
    a_out = n1 // 2
    a = np.arange(n1)
    k1 = np.arange(kh)
    th = 2.0 * np.pi * ((k1[:, None] * a[None, :]) % n1) / n1
    fwd = np.stack([np.cos(th), -np.sin(th)], axis=1).reshape(2 * kh, n1)
    alpha = np.where((k1 == 0) | (k1 == n1 // 2), 1.0, 2.0) / n
    th_i = th[:, :a_out].T
    inv = np.stack([np.cos(th_i) * alpha[None, :], -np.sin(th_i) * alpha[None, :]], axis=2)
    inv = inv.reshape(a_out, 2 * kh)
    b = np.arange(n2)
    k2 = np.arange(n2)
    kk = k1[:, None, None] + n1 * k2[None, :, None]
    ph = 2.0 * np.pi * ((kk * b[None, None, :]) % n) / n
    wr, wi = np.cos(ph), -np.sin(ph)
    vr, vi = np.transpose(wr, (0, 2, 1)), -np.transpose(wi, (0, 2, 1))

    def block(re, im):
        blk = np.concatenate([np.concatenate([re, -im], axis=2), np.concatenate([im, re], axis=2)], axis=1)
        return np.asarray(blk, np.float32).astype(BF16)

    def kron_bf16(m, sub):
        return np.asarray(np.kron(m, np.eye(sub)), np.float32).astype(BF16)

    return dict(n1=n1, n2=n2, kh=kh, a_out=a_out,
                a_fwd=kron_bf16(fwd, SUB),
                a_fwd_half=kron_bf16(fwd[:, :a_out], SUB_BF16),
                a_inv=kron_bf16(inv, SUB_BF16),
                w_fwd=block(wr, wi), w_inv=block(vr, vi))


def _filter_tables(seq_len, channels):
    n = 2 * seq_len
    bands = (FILTER_EMB - 1) // 2
    pos = np.concatenate([np.arange(seq_len), [0], np.arange(seq_len - 1, 0, -1)])
    t = np.linspace(0.0, 1.0, seq_len)[pos]
    w = 2.0 * np.pi * pos / seq_len
    f = np.linspace(1e-4, bands - 1, bands)
    ang = w[:, None] * f[None, :]
    z = np.zeros((n, FILTER_HIDDEN), np.float64)
    z[:, 0] = t
    z[:, 1:1 + bands] = np.cos(ang)
    z[:, 1 + bands:1 + 2 * bands] = -np.sin(ang)
    deltas = np.abs(np.linspace(math.log(DECAY_TARGET) / DECAY_SLOW, math.log(DECAY_TARGET) / DECAY_FAST, channels))
    return z.astype(np.float32), deltas[None, :].astype(np.float32)


def _filter_body(z_ref, w1_ref, b1_ref, w2_ref, b2_ref, w3_ref, b3_ref, fr_ref, w4_ref, dl_ref,
                 k_ref, s_ref, *, tr, seq_len, channels):
    i = pl.program_id(0)
    fr = fr_ref[...]
    z = z_ref[...]

    def layer(h, w_ref, b_ref):
        w_hi, w_lo = _split(w_ref[...])
        h_hi, h_lo = _split(h)
        pre = _dot(h_hi, w_hi) + _dot(h_lo, w_hi) + _dot(h_hi, w_lo)
        return jnp.sin(fr * (pre + b_ref[...]))

    h = layer(z, w1_ref, b1_ref)
    h = layer(h, w2_ref, b2_ref)
    h = layer(h, w3_ref, b3_ref)
    w_hi, w_lo = _split(w4_ref[...])
    h_hi, h_lo = _split(h)
    k = _dot(h_hi, w_hi) + _dot(h_lo, w_hi) + _dot(h_hi, w_lo)
    decay = jnp.exp(-z[:, 0:1] * dl_ref[...]) + DECAY_SHIFT
    row = i * tr + lax.broadcasted_iota(jnp.int32, (tr, 1), 0)
    decay = jnp.where(row == seq_len, 0.0, decay)
    k = k * jnp.concatenate([decay, decay], axis=1)
    k_ref[...] = k

    @pl.when(i == 0)
    def _():
        s_ref[...] = jnp.zeros_like(s_ref)

    s_ref[...] += jnp.sum(jnp.abs(k).reshape(tr // 8, 8, 2 * channels), axis=0)


def filter_gen(seq_len, channels, w1p, b1, w2, b2, w3, b3, freq, w4d):
    n = 2 * seq_len
    tr = min(512, seq_len)
    z, deltas = _filter_tables(seq_len, channels)
    hid = FILTER_HIDDEN
    full = lambda shape: pl.BlockSpec(shape, lambda i: (0,) * len(shape))
    body = functools.partial(_filter_body, tr=tr, seq_len=seq_len, channels=channels)
    return pl.pallas_call(
        body,
        out_shape=(jax.ShapeDtypeStruct((n, 2 * channels), F32), jax.ShapeDtypeStruct((8, 2 * channels), F32)),
        grid=(n // tr,),
        in_specs=[
            pl.BlockSpec((tr, hid), lambda i: (i, 0)),
            full((hid, hid)), full((1, hid)), full((hid, hid)), full((1, hid)), full((hid, hid)), full((1, hid)),
            full((1, hid)),
            pl.BlockSpec((None, hid, 2 * channels), lambda i: ((i * tr) // seq_len, 0, 0)),
            full((1, channels)),
        ],
        out_specs=(pl.BlockSpec((tr, 2 * channels), lambda i: (i, 0)), full((8, 2 * channels))),
        compiler_params=_cparams(("arbitrary",)),
        name="filter_gen",
    )(z, w1p, b1, w2, b2, w3, b3, freq, w4d, deltas)


def _major_fwd_body(u_ref, a_ref, y_ref, *, groups, sub):
    a_in, _, ct = u_ref.shape
    kh = y_ref.shape[0]
    for q in range(groups):
        rows = slice(q * sub, (q + 1) * sub)
        u = u_ref[:, rows, :].reshape(a_in * sub, ct)
        y = _dot(a_ref[...], u.astype(BF16))
        y_ref[:, :, rows, :] = y.reshape(kh, 2, sub, ct).astype(y_ref.dtype)


def major_fwd(u, col, channels, a_fwd, kh, ct, groups, sub, out_dtype):
    bsz, a_in, n2, _ = u.shape
    nc = channels // ct
    rows = groups * sub
    body = functools.partial(_major_fwd_body, groups=groups, sub=sub)
    return pl.pallas_call(
        body,
        out_shape=jax.ShapeDtypeStruct((bsz, kh, 2, n2, channels), out_dtype),
        grid=(bsz, n2 // rows, nc),
        in_specs=[
            pl.BlockSpec((None, a_in, rows, ct), lambda b, r, j: (b, 0, r, col * nc + j)),
            pl.BlockSpec(a_fwd.shape, lambda b, r, j: (0, 0)),
        ],
        out_specs=pl.BlockSpec((None, kh, 2, rows, ct), lambda b, r, j: (b, 0, 0, r, j)),
        compiler_params=_cparams(("parallel", "parallel", "parallel")),
        name="major_fwd",
    )(u, a_fwd)


def _major_inv_body(g_ref, a_ref, m_ref, o_ref, *, groups, sub):
    kh, _, _, ct = g_ref.shape
    a_out = o_ref.shape[0]
    for q in range(groups):
        rows = slice(q * sub, (q + 1) * sub)
        g = g_ref[:, :, rows, :].reshape(kh * 2 * sub, ct)
        y = _dot(a_ref[...], g).reshape(a_out, sub, ct)
        o_ref[:, rows, :] = (y * m_ref[:, rows, :].astype(F32)).astype(o_ref.dtype)


def major_inv(g, a_inv, mult, col, ct, groups, sub, out_dtype):
    bsz, kh, _, n2, channels = g.shape
    a_out = mult.shape[1]
    nc = channels // ct
    rows = groups * sub
    body = functools.partial(_major_inv_body, groups=groups, sub=sub)
    return pl.pallas_call(
        body,
        out_shape=jax.ShapeDtypeStruct((bsz, a_out, n2, channels), out_dtype),
        grid=(bsz, n2 // rows, nc),
        in_specs=[
            pl.BlockSpec((None, kh, 2, rows, ct), lambda b, r, j: (b, 0, 0, r, j)),
            pl.BlockSpec(a_inv.shape, lambda b, r, j: (0, 0)),
            pl.BlockSpec((None, a_out, rows, ct), lambda b, r, j: (b, 0, r, col * nc + j)),
        ],
        out_specs=pl.BlockSpec((None, a_out, rows, ct), lambda b, r, j: (b, 0, r, j)),
        compiler_params=_cparams(("parallel", "parallel", "parallel")),
        name="major_inv",
    )(g, a_inv, mult)


def _minor_conv_body(y_ref, wf_ref, wi_ref, k_ref, g_ref, *, n2):
    kr = k_ref[0]
    ki = k_ref[1]
    for i in range(y_ref.shape[0]):
        z = _dot(wf_ref[...], jnp.concatenate([y_ref[i, 0], y_ref[i, 1]], axis=0))
        zr, zi = z[:n2], z[n2:]
        pr = (zr * kr - zi * ki).astype(BF16)
        pi = (zr * ki + zi * kr).astype(BF16)
        g = _dot(wi_ref[...], jnp.concatenate([pr, pi], axis=0))
        g_ref[i, 0] = g[:n2].astype(g_ref.dtype)
        g_ref[i, 1] = g[n2:].astype(g_ref.dtype)


def minor_conv(y, w_fwd, w_inv, spec, order, ct, bb):
    bsz, kh, _, n2, c = y.shape
    nc = c // ct
    y_spec = pl.BlockSpec((bb, None, 2, n2, ct), lambda k, j, b: (b, k, 0, 0, j))
    w_spec = pl.BlockSpec((None, 2 * n2, 2 * n2), lambda k, j, b: (k, 0, 0))
    k_spec = pl.BlockSpec((None, 2, n2, ct), lambda k, j, b: (k, 0, 0, order * nc + j))
    body = functools.partial(_minor_conv_body, n2=n2)
    return pl.pallas_call(
        body,
        out_shape=jax.ShapeDtypeStruct(y.shape, y.dtype),
        grid=(kh, nc, bsz // bb),
        in_specs=[y_spec, w_spec, w_spec, k_spec],
        out_specs=y_spec,
        compiler_params=_cparams(("parallel", "parallel", "parallel")),
        name="minor_conv",
    )(y, w_fwd, w_inv, spec)


def _minor_filter_body(y_ref, wf_ref, inv_ref, bias_ref, k_ref, *, n2):
    z = _dot(wf_ref[...], jnp.concatenate([y_ref[0].astype(BF16), y_ref[1].astype(BF16)], axis=0))
    inv = inv_ref[...]
    k_ref[0] = z[:n2] * inv + bias_ref[...]
    k_ref[1] = z[n2:] * inv


def minor_filter(y, w_fwd, inv_norm, bias, ct):
    kh, _, n2, c2 = y.shape
    y_spec = pl.BlockSpec((None, 2, n2, ct), lambda k, j: (k, 0, 0, j))
    v_spec = pl.BlockSpec((1, ct), lambda k, j: (0, j))
    body = functools.partial(_minor_filter_body, n2=n2)
    return pl.pallas_call(
        body,
        out_shape=jax.ShapeDtypeStruct(y.shape, F32),
        grid=(kh, c2 // ct),
        in_specs=[y_spec, pl.BlockSpec((None, 2 * n2, 2 * n2), lambda k, j: (k, 0, 0)), v_spec, v_spec],
        out_specs=y_spec,
        compiler_params=_cparams(("parallel", "parallel")),
        name="minor_filter",
    )(y, w_fwd, inv_norm, bias)


def _major_groups(a_in, sub, n2):
    return min(n2 // sub, max(1, 2048 // (a_in * sub)))


def hyena_filter_spectra(seq_len, channels, fw, f_bias):
    cst = _fft_consts(seq_len)
    n1, n2, kh = cst["n1"], cst["n2"], cst["kh"]
    kern, sums = filter_gen(seq_len, channels, *fw)
    inv_norm = 1.0 / jnp.sum(sums, axis=0, keepdims=True)
    u = kern.reshape(1, n1, n2, 2 * channels)
    y = major_fwd(u, 0, 2 * channels, cst["a_fwd"], kh, ct=512, groups=_major_groups(n1, SUB, n2), sub=SUB,
                  out_dtype=F32)
    return minor_filter(y[0], cst["w_fwd"], inv_norm, f_bias.reshape(1, 2 * channels), ct=1024)


def long_conv_gated(u, u_col, mult, mult_col, spectra, order, seq_len, channels, out_dtype):
    cst = _fft_consts(seq_len)
    groups = _major_groups(cst["a_out"], SUB_BF16, cst["n2"])
    y = major_fwd(u, u_col, channels, cst["a_fwd_half"], cst["kh"], ct=512, groups=groups, sub=SUB_BF16,
                  out_dtype=BF16)
    g = minor_conv(y, cst["w_fwd"], cst["w_inv"], spectra, order, ct=channels, bb=min(u.shape[0], 4))
    return major_inv(g, cst["a_inv"], mult, mult_col, ct=512, groups=groups, sub=SUB_BF16, out_dtype=out_dtype)


def _row_tile(seq_len, want):
    return min(want, seq_len)


def _layer(x, mem, lw, ws, layer, batch, seq_len, dims):
    d, dh, da, dm = dims
    n_heads = da // HEAD_DIM
    n_mem = mem.shape[0] // batch

    proj = norm_matmul_conv(x, lw["g_pre_mix"], ws["w_in"], layer, lw["conv_w"], lw["conv_b"], seq_len,
                            tm=_row_tile(seq_len, 1024), tn=512)

    cst = _fft_consts(seq_len)
    p4 = proj.reshape(batch, cst["a_out"], cst["n2"], proj.shape[1])
    spectra = lw["spectra"][seq_len]
    z = long_conv_gated(p4, 0, p4, 1, spectra, 0, seq_len, dh, BF16)
    y_h = long_conv_gated(z, 0, p4, 2, spectra, 1, seq_len, dh, BF16).reshape(batch * seq_len, dh)

    qcol = 3 * dh // HEAD_DIM
    y_a = dil_attn(proj, batch, seq_len, qcol, qcol + n_heads, qcol + 2 * n_heads, n_heads)
    kv = norm_matmul(mem, lw["g_mem"], ws["w_mem_kv"], layer, tm=n_mem, tn=512)
    y_m = mem_attn(proj, kv, batch, seq_len, n_mem, qcol + 3 * n_heads, dm // HEAD_DIM, tq=_row_tile(seq_len, 1024))

    x = mix_out(y_h, y_a, y_m, lw["g_grp"], ws["w_out"], layer, lw["g_post_mix"], x, _row_tile(seq_len, 512))
    return ffn(x, lw["g_pre_ffn"], ws["w_up"], lw["ffn_conv_w"], lw["ffn_conv_b"], ws["w_down"],
               lw["g_post_ffn"], layer, seq_len, _row_tile(seq_len, 1024), tf=512)


def kernel(x_prompt, x_sample, mem_prompt, mem_sample, g_pre_mix, w_in, conv_w, conv_b, f_w1, f_b1, f_w2, f_b2, f_w3, f_b3, f_w4, f_freq, f_bias, g_mem, w_mem_kv, g_grp, w_out, g_post_mix, g_pre_ffn, w_up, ffn_conv_w, ffn_conv_b, w_down, g_post_ffn):
    depth, d, d_in = w_in.shape
    dh = conv_w.shape[2] // 3
    dm = w_mem_kv.shape[2] // 2
    da = (d_in - 3 * dh - dm) // 3
    dims = (d, dh, da, dm)
    groups = [(x_prompt, mem_prompt), (x_sample, mem_sample)]
    seq_lens = sorted({g[0].shape[1] for g in groups})

    xs = [g[0].reshape(-1, d) for g in groups]
    mems = [g[1].reshape(-1, d) for g in groups]
    hid = FILTER_HIDDEN
    ws = dict(w_in=w_in.astype(BF16), w_mem_kv=w_mem_kv.astype(BF16), w_out=w_out.astype(BF16),
              w_up=w_up.astype(BF16), w_down=w_down.astype(BF16))
    for i in range(depth):
        row = lambda v: v[i][None, :]
        w1p = jnp.zeros((hid, hid), F32).at[:FILTER_EMB].set(f_w1[i])
        w4d = f_w4[i].reshape(hid, 2, 2, dh).transpose(2, 0, 1, 3).reshape(2, hid, 2 * dh)
        fw = (w1p, row(f_b1), f_w2[i], row(f_b2), f_w3[i], row(f_b3), row(f_freq), w4d)
        lw = dict(
            g_pre_mix=row(g_pre_mix), conv_w=conv_w[i], conv_b=row(conv_b),
            g_mem=row(g_mem), g_grp=row(g_grp), g_post_mix=row(g_post_mix), g_pre_ffn=row(g_pre_ffn),
            ffn_conv_w=ffn_conv_w[i], ffn_conv_b=row(ffn_conv_b), g_post_ffn=row(g_post_ffn),
            spectra={sl: hyena_filter_spectra(sl, dh, fw, f_bias[i]) for sl in seq_lens},
        )
        for gi, (xg, _) in enumerate(groups):
            xs[gi] = _layer(xs[gi], mems[gi], lw, ws, i, xg.shape[0], xg.shape[1], dims)
    return tuple(x.reshape(g[0].shape) for x, g in zip(xs, groups))
```

```python
import functools
import math

import numpy as np
import jax
import jax.numpy as jnp
from jax import lax
from jax.experimental import pallas as pl
from jax.experimental.pallas import tpu as pltpu

F32 = jnp.float32
BF16 = jnp.bfloat16

HEAD_DIM = 128
DILATED_PATTERNS = ((128, 1), (512, 4), (2048, 16))
ROPE_THETA = 500000.0
ROT_DIM = HEAD_DIM // 4
FILTER_EMB = 33
FILTER_HIDDEN = 64
DECAY_FAST = 0.3
DECAY_SLOW = 1.5
DECAY_TARGET = 1e-2
DECAY_SHIFT = 0.05
EPS = 1e-6
NEG = -1e30

HALO = 16
ATTN_BLOCKS_PER_STEP = 8
SUB = 8
SUB_BF16 = 16
VMEM_LIMIT = 56 * 1024 * 1024


def _cparams(sem):
    return pltpu.CompilerParams(dimension_semantics=sem, vmem_limit_bytes=VMEM_LIMIT)


def _split(x):
    hi = x.astype(BF16)
    lo = (x - hi.astype(F32)).astype(BF16)
    return hi, lo


def _dot(a, b):
    return jnp.dot(a, b, preferred_element_type=F32)


def _rms(v, g):
    return v * lax.rsqrt(jnp.mean(v * v, axis=-1, keepdims=True) + EPS) * g


def _fill_normed(hb_ref, xp_ref, x_ref, xn_ref, g_ref, i, tm, blocks_per_seq):
    g = g_ref[...]
    pos = i % blocks_per_seq
    keep_p = (pos != 0).astype(F32)
    keep_n = (pos != blocks_per_seq - 1).astype(F32)
    hb_ref[0:HALO, :] = (_rms(xp_ref[...], g) * keep_p).astype(BF16)
    hb_ref[HALO:HALO + tm, :] = _rms(x_ref[...], g).astype(BF16)
    hb_ref[HALO + tm:, :] = (_rms(xn_ref[...], g) * keep_n).astype(BF16)


def _conv3(u, cw, cb, tm):
    rows = tm + 2 * HALO
    up = pltpu.roll(u, 1, 0)[HALO:HALO + tm]
    un = pltpu.roll(u, rows - 1, 0)[HALO:HALO + tm]
    uc = u[HALO:HALO + tm]
    return up * cw[0:1] + uc * cw[1:2] + un * cw[2:3] + cb


def _nm_conv_body(xp_ref, x_ref, xn_ref, g_ref, w_ref, cw_ref, cb_ref, o_ref, hb_ref, *, tm, blocks_per_seq,
                  n_conv):
    i = pl.program_id(0)
    j = pl.program_id(1)

    @pl.when(j == 0)
    def _():
        _fill_normed(hb_ref, xp_ref, x_ref, xn_ref, g_ref, i, tm, blocks_per_seq)

    u = _dot(hb_ref[...], w_ref[...])

    @pl.when(j < n_conv)
    def _():
        o_ref[...] = _conv3(u, cw_ref[...], cb_ref[...], tm).astype(o_ref.dtype)

    @pl.when(j >= n_conv)
    def _():
        o_ref[...] = u[HALO:HALO + tm].astype(o_ref.dtype)


def _nm_body(x_ref, g_ref, w_ref, o_ref, hb_ref):
    @pl.when(pl.program_id(1) == 0)
    def _():
        hb_ref[...] = _rms(x_ref[...], g_ref[...]).astype(BF16)

    o_ref[...] = _dot(hb_ref[...], w_ref[...]).astype(o_ref.dtype)


def _halo_specs(tm, d, n_rows, x_buffers=2):
    hb = tm // HALO
    last = n_rows // HALO - 1
    return [
        pl.BlockSpec((HALO, d), lambda i, j: (jnp.maximum(i * hb - 1, 0), 0)),
        pl.BlockSpec((tm, d), lambda i, j: (i, 0), pipeline_mode=pl.Buffered(x_buffers)),
        pl.BlockSpec((HALO, d), lambda i, j: (jnp.minimum((i + 1) * hb, last), 0)),
    ]


def norm_matmul_conv(x, g, w, layer, cw, cb, seq_len, tm, tn):
    t, d = x.shape
    n = w.shape[2]
    n_conv = cw.shape[1] // tn
    body = functools.partial(_nm_conv_body, tm=tm, blocks_per_seq=seq_len // tm, n_conv=n_conv)
    return pl.pallas_call(
        body,
        out_shape=jax.ShapeDtypeStruct((t, n), BF16),
        grid=(t // tm, n // tn),
        in_specs=_halo_specs(tm, d, t) + [
            pl.BlockSpec((1, d), lambda i, j: (0, 0)),
            pl.BlockSpec((None, d, tn), lambda i, j: (layer, 0, j)),
            pl.BlockSpec((3, tn), lambda i, j: (0, jnp.minimum(j, n_conv - 1))),
            pl.BlockSpec((1, tn), lambda i, j: (0, jnp.minimum(j, n_conv - 1))),
        ],
        out_specs=pl.BlockSpec((tm, tn), lambda i, j: (i, j)),
        scratch_shapes=[pltpu.VMEM((tm + 2 * HALO, d), BF16)],
        compiler_params=_cparams(("parallel", "arbitrary")),
        name="norm_matmul_conv",
    )(x, x, x, g, w, cw, cb)


def norm_matmul(x, g, w, layer, tm, tn):
    t, d = x.shape
    n = w.shape[2]
    return pl.pallas_call(
        _nm_body,
        out_shape=jax.ShapeDtypeStruct((t, n), BF16),
        grid=(t // tm, n // tn),
        in_specs=[
            pl.BlockSpec((tm, d), lambda i, j: (i, 0)),
            pl.BlockSpec((1, d), lambda i, j: (0, 0)),
            pl.BlockSpec((None, d, tn), lambda i, j: (layer, 0, j)),
        ],
        out_specs=pl.BlockSpec((tm, tn), lambda i, j: (i, j)),
        scratch_shapes=[pltpu.VMEM((tm, d), BF16)],
        compiler_params=_cparams(("parallel", "arbitrary")),
        name="norm_matmul",
    )(x, g, w)


def _ffn_body(xp_ref, x_ref, xn_ref, g_ref, wg_ref, wv_ref, cw_ref, cb_ref, wd_ref, gp_ref, o_ref,
              hb_ref, *, tm, blocks_per_seq, nj):
    i = pl.program_id(0)
    j = pl.program_id(1)

    @pl.when(j == 0)
    def _():
        _fill_normed(hb_ref, xp_ref, x_ref, xn_ref, g_ref, i, tm, blocks_per_seq)
        o_ref[...] = jnp.zeros_like(o_ref)

    ug = _dot(hb_ref[...], wg_ref[...])
    val = _dot(hb_ref[HALO:HALO + tm, :], wv_ref[...])
    gate = _conv3(ug, cw_ref[...], cb_ref[...], tm)
    ff = jax.nn.gelu(gate, approximate=True) * val
    o_ref[...] += _dot(ff.astype(BF16), wd_ref[...])

    @pl.when(j == nj - 1)
    def _():
        o_ref[...] = x_ref[...] + _rms(o_ref[...], gp_ref[...])


def ffn(x, g_pre, w_up, cw, cb, w_down, g_post, layer, seq_len, tm, tf):
    t, d = x.shape
    d_ff = w_down.shape[1]
    nj = d_ff // tf
    body = functools.partial(_ffn_body, tm=tm, blocks_per_seq=seq_len // tm, nj=nj)
    return pl.pallas_call(
        body,
        out_shape=jax.ShapeDtypeStruct((t, d), F32),
        grid=(t // tm, nj),
        in_specs=_halo_specs(tm, d, t, x_buffers=1) + [
            pl.BlockSpec((1, d), lambda i, j: (0, 0)),
            pl.BlockSpec((None, d, tf), lambda i, j: (layer, 0, j)),
            pl.BlockSpec((None, d, tf), lambda i, j: (layer, 0, j + nj)),
            pl.BlockSpec((3, tf), lambda i, j: (0, j)),
            pl.BlockSpec((1, tf), lambda i, j: (0, j)),
            pl.BlockSpec((None, tf, d), lambda i, j: (layer, j, 0)),
            pl.BlockSpec((1, d), lambda i, j: (0, 0)),
        ],
        out_specs=pl.BlockSpec((tm, d), lambda i, j: (i, 0)),
        scratch_shapes=[pltpu.VMEM((tm + 2 * HALO, d), BF16)],
        compiler_params=_cparams(("parallel", "arbitrary")),
        name="ffn",
    )(x, x, x, g_pre, w_up, w_up, cw, cb, w_down, g_post)


def _mix_out_body(yh_ref, ya_ref, ym_ref, gg_ref, w_ref, gp_ref, x_ref, o_ref, *, dh, da):
    gg = gg_ref[...]
    acc = _dot(_rms(yh_ref[...].astype(F32), gg[:, :dh]).astype(BF16), w_ref[0:dh, :])
    acc += _dot(_rms(ya_ref[...].astype(F32), gg[:, dh:dh + da]).astype(BF16), w_ref[dh:dh + da, :])
    acc += _dot(_rms(ym_ref[...].astype(F32), gg[:, dh + da:]).astype(BF16), w_ref[dh + da:, :])
    o_ref[...] = x_ref[...] + _rms(acc, gp_ref[...])


def mix_out(yh, ya, ym, g_grp, w_out, layer, g_post, x, tm):
    t, d = x.shape
    dh, da, dm = yh.shape[1], ya.shape[1], ym.shape[1]
    body = functools.partial(_mix_out_body, dh=dh, da=da)
    return pl.pallas_call(
        body,
        out_shape=jax.ShapeDtypeStruct((t, d), F32),
        grid=(t // tm,),
        in_specs=[
            pl.BlockSpec((tm, dh), lambda i: (i, 0)),
            pl.BlockSpec((tm, da), lambda i: (i, 0)),
            pl.BlockSpec((tm, dm), lambda i: (i, 0)),
            pl.BlockSpec((1, d), lambda i: (0, 0)),
            pl.BlockSpec((None, d, d), lambda i: (layer, 0, 0)),
            pl.BlockSpec((1, d), lambda i: (0, 0)),
            pl.BlockSpec((tm, d), lambda i: (i, 0)),
        ],
        out_specs=pl.BlockSpec((tm, d), lambda i: (i, 0)),
        compiler_params=_cparams(("parallel",)),
        name="mix_out",
    )(yh, ya, ym, g_grp, w_out, g_post, x)


def _rope_tables(seq_len):
    half = ROT_DIM // 2
    inv_freq = np.exp(-math.log(ROPE_THETA) * np.arange(0, ROT_DIM, 2, dtype=np.float64) / ROT_DIM)
    ang = np.arange(seq_len, dtype=np.float64)[:, None] * inv_freq.astype(np.float32).astype(np.float64)[None, :]
    c, s = np.cos(ang), np.sin(ang)
    cos_t = np.ones((seq_len, HEAD_DIM), np.float64)
    sin_t = np.zeros((seq_len, HEAD_DIM), np.float64)
    cos_t[:, :half] = c
    cos_t[:, half:ROT_DIM] = c
    sin_t[:, :half] = -s
    sin_t[:, half:ROT_DIM] = s
    return cos_t.astype(np.float32), sin_t.astype(np.float32)


def _dil_attn_body(q_ref, k_ref, v_ref, cos_ref, sin_ref, o_ref, qs_ref, ks_ref, vs_ref, m_ref, l_ref, *, seq_len):
    half = ROT_DIM // 2
    scale = 1.0 / math.sqrt(HEAD_DIM)
    chunk = min(512, seq_len)

    def rope_chunk(c, carry):
        rows = pl.ds(pl.multiple_of(c * chunk, chunk), chunk)
        cs = cos_ref[rows, :]
        sn = sin_ref[rows, :]
        lane = lax.broadcasted_iota(jnp.int32, (chunk, HEAD_DIM), 1)
        for src, dst, mul in ((q_ref, qs_ref, scale), (k_ref, ks_ref, None)):
            x = src[rows, :].astype(F32)
            partner = jnp.where(lane < half, pltpu.roll(x, HEAD_DIM - half, 1), pltpu.roll(x, half, 1))
            y = x * cs + partner * sn
            dst[rows, :] = y if mul is None else y * mul
        vs_ref[rows, :] = v_ref[rows, :].astype(F32)
        return carry

    lax.fori_loop(0, seq_len // chunk, rope_chunk, 0)

    n_br = len(DILATED_PATTERNS)
    for bi, (window, dil) in enumerate(DILATED_PATTERNS):
        radius = window // (2 * dil)
        n = seq_len // dil
        tq = min(128, n)
        kw = min(tq + 2 * radius, n)
        nblk = n // tq

        def one_block(idx, bi=bi, dil=dil, radius=radius, n=n, tq=tq, kw=kw, nblk=nblk):
            r = idx // nblk
            q0 = (idx % nblk) * tq
            k0 = jnp.clip(q0 - radius, 0, n - kw)
            if dil == 1:
                qsl = pl.ds(pl.multiple_of(q0, tq), tq)
                ksl = pl.ds(pl.multiple_of(k0, 8), kw)
            else:
                qsl = pl.ds(r + q0 * dil, tq, stride=dil)
                ksl = pl.ds(r + k0 * dil, kw, stride=dil)
            qb = qs_ref[qsl, :].astype(BF16)
            kb = ks_ref[ksl, :].astype(BF16)
            vb = vs_ref[ksl, :].astype(BF16)
            s = lax.dot_general(qb, kb, (((1,), (1,)), ((), ())), preferred_element_type=F32)
            rel = (k0 + lax.broadcasted_iota(jnp.int32, (tq, kw), 1)) - (q0 + lax.broadcasted_iota(jnp.int32, (tq, kw), 0))
            s = jnp.where(jnp.abs(rel) <= radius, s, NEG)
            m = jnp.max(s, axis=-1, keepdims=True)
            p = jnp.exp(s - m)
            l = jnp.sum(p, axis=-1, keepdims=True)
            acc = _dot(p.astype(BF16), vb)
            m = jnp.broadcast_to(m, (tq, HEAD_DIM))
            l = jnp.broadcast_to(l, (tq, HEAD_DIM))
            if bi > 0:
                m_old = m_ref[qsl, :]
                m_new = jnp.maximum(m_old, m)
                a_old = jnp.exp(m_old - m_new)
                a_new = jnp.exp(m - m_new)
                acc = o_ref[qsl, :] * a_old + acc * a_new
                l = l_ref[qsl, :] * a_old + l * a_new
                m = m_new
            return qsl, m, l, acc

        total = dil * nblk
        unroll = min(ATTN_BLOCKS_PER_STEP, total)

        def step(it, carry, bi=bi, unroll=unroll, one_block=one_block):
            done = [one_block(it * unroll + u) for u in range(unroll)]
            for qsl, m, l, acc in done:
                if bi == n_br - 1:
                    o_ref[qsl, :] = acc / l
                else:
                    o_ref[qsl, :] = acc
                    m_ref[qsl, :] = m
                    l_ref[qsl, :] = l
            return carry

        lax.fori_loop(0, total // unroll, step, 0)


def dil_attn(proj, batch, seq_len, q_col, k_col, v_col, n_heads):
    cos_t, sin_t = _rope_tables(seq_len)
    nb = 1 if seq_len * HEAD_DIM * 4 > (2 << 20) else 2

    def col_spec(col):
        return pl.BlockSpec((seq_len, HEAD_DIM), lambda b, h: (b, col + h))

    tab_spec = pl.BlockSpec((seq_len, HEAD_DIM), lambda b, h: (0, 0), pipeline_mode=pl.Buffered(1))
    body = functools.partial(_dil_attn_body, seq_len=seq_len)
    return pl.pallas_call(
        body,
        out_shape=jax.ShapeDtypeStruct((batch * seq_len, n_heads * HEAD_DIM), F32),
        grid=(batch, n_heads),
        in_specs=[col_spec(q_col), col_spec(k_col), col_spec(v_col), tab_spec, tab_spec],
        out_specs=pl.BlockSpec((seq_len, HEAD_DIM), lambda b, h: (b, h), pipeline_mode=pl.Buffered(nb)),
        scratch_shapes=[pltpu.VMEM((seq_len, HEAD_DIM), F32)] * 5,
        compiler_params=_cparams(("parallel", "parallel")),
        name="dil_attn",
    )(proj, proj, proj, cos_t, sin_t)


def _mem_attn_body(q_ref, k_ref, v_ref, o_ref):
    s = lax.dot_general(q_ref[...].astype(BF16), k_ref[...].astype(BF16), (((1,), (1,)), ((), ())),
                        preferred_element_type=F32) * (1.0 / math.sqrt(HEAD_DIM))
    m = jnp.max(s, axis=-1, keepdims=True)
    p = jnp.exp(s - m)
    l = jnp.sum(p, axis=-1, keepdims=True)
    o_ref[...] = (_dot(p.astype(BF16), v_ref[...].astype(BF16)) / l).astype(o_ref.dtype)


def mem_attn(proj, kv, batch, seq_len, n_mem, q_col, n_heads, tq):
    nq = seq_len // tq
    return pl.pallas_call(
        _mem_attn_body,
        out_shape=jax.ShapeDtypeStruct((batch * seq_len, n_heads * HEAD_DIM), BF16),
        grid=(batch, nq, n_heads),
        in_specs=[
            pl.BlockSpec((tq, HEAD_DIM), lambda b, i, h: (b * nq + i, q_col + h)),
            pl.BlockSpec((n_mem, HEAD_DIM), lambda b, i, h: (b, h)),
            pl.BlockSpec((n_mem, HEAD_DIM), lambda b, i, h: (b, n_heads + h)),
        ],
        out_specs=pl.BlockSpec((tq, HEAD_DIM), lambda b, i, h: (b * nq + i, h)),
        compiler_params=_cparams(("parallel", "parallel", "parallel")),
        name="mem_attn",
    )(proj, kv, kv)


def _fft_split(seq_len):
    n = 2 * seq_len
    n2 = 128
    return n // n2, n2


@functools.lru_cache(maxsize=None)
def _fft_consts(seq_len):
    n1, n2 = _fft_split(seq_len)
    n = n1 * n2
    kh = n1 // 2 + 1
    a_out = n1 // 2
    a = np.arange(n1)
    k1 = np.arange(kh)
    th = 2.0 * np.pi * ((k1[:, None] * a[None, :]) % n1) / n1
    fwd = np.stack([np.cos(th), -np.sin(th)], axis=1).reshape(2 * kh, n1)
    alpha = np.where((k1 == 0) | (k1 == n1 // 2), 1.0, 2.0) / n
    th_i = th[:, :a_out].T
    inv = np.stack([np.cos(th_i) * alpha[None, :], -np.sin(th_i) * alpha[None, :]], axis=2)
    inv = inv.reshape(a_out, 2 * kh)
    b = np.arange(n2)
    k2 = np.arange(n2)
    kk = k1[:, None, None] + n1 * k2[None, :, None]
    ph = 2.0 * np.pi * ((kk * b[None, None, :]) % n) / n
    wr, wi = np.cos(ph), -np.sin(ph)
    vr, vi = np.transpose(wr, (0, 2, 1)), -np.transpose(wi, (0, 2, 1))

    def block(re, im):
        blk = np.concatenate([np.concatenate([re, -im], axis=2), np.concatenate([im, re], axis=2)], axis=1)
        return np.asarray(blk, np.float32).astype(BF16)

    def kron_bf16(m, sub):
        return np.asarray(np.kron(m, np.eye(sub)), np.float32).astype(BF16)

    return dict(n1=n1, n2=n2, kh=kh, a_out=a_out,
                a_fwd=kron_bf16(fwd, SUB),
                a_fwd_half=kron_bf16(fwd[:, :a_out], SUB_BF16),
                a_inv=kron_bf16(inv, SUB_BF16),
                w_fwd=block(wr, wi), w_inv=block(vr, vi))


def _filter_tables(seq_len, channels):
    n = 2 * seq_len
    bands = (FILTER_EMB - 1) // 2
    pos = np.concatenate([np.arange(seq_len), [0], np.arange(seq_len - 1, 0, -1)])
    t = np.linspace(0.0, 1.0, seq_len)[pos]
    w = 2.0 * np.pi * pos / seq_len
    f = np.linspace(1e-4, bands - 1, bands)
    ang = w[:, None] * f[None, :]
    z = np.zeros((n, FILTER_HIDDEN), np.float64)
    z[:, 0] = t
    z[:, 1:1 + bands] = np.cos(ang)
    z[:, 1 + bands:1 + 2 * bands] = -np.sin(ang)
    deltas = np.abs(np.linspace(math.log(DECAY_TARGET) / DECAY_SLOW, math.log(DECAY_TARGET) / DECAY_FAST, channels))
    return z.astype(np.float32), deltas[None, :].astype(np.float32)


def _filter_body(z_ref, w1_ref, b1_ref, w2_ref, b2_ref, w3_ref, b3_ref, fr_ref, w4_ref, dl_ref,
                 k_ref, s_ref, *, tr, seq_len, channels):
    i = pl.program_id(0)
    fr = fr_ref[...]
    z = z_ref[...]

    def layer(h, w_ref, b_ref):
        w_hi, w_lo = _split(w_ref[...])
        h_hi, h_lo = _split(h)
        pre = _dot(h_hi, w_hi) + _dot(h_lo, w_hi) + _dot(h_hi, w_lo)
        return jnp.sin(fr * (pre + b_ref[...]))

    h = layer(z, w1_ref, b1_ref)
    h = layer(h, w2_ref, b2_ref)
    h = layer(h, w3_ref, b3_ref)
    w_hi, w_lo = _split(w4_ref[...])
    h_hi, h_lo = _split(h)
    k = _dot(h_hi, w_hi) + _dot(h_lo, w_hi) + _dot(h_hi, w_lo)
    decay = jnp.exp(-z[:, 0:1] * dl_ref[...]) + DECAY_SHIFT
    row = i * tr + lax.broadcasted_iota(jnp.int32, (tr, 1), 0)
    decay = jnp.where(row == seq_len, 0.0, decay)
    k = k * jnp.concatenate([decay, decay], axis=1)
    k_ref[...] = k

    @pl.when(i == 0)
    def _():
        s_ref[...] = jnp.zeros_like(s_ref)

    s_ref[...] += jnp.sum(jnp.abs(k).reshape(tr // 8, 8, 2 * channels), axis=0)


def filter_gen(seq_len, channels, w1p, b1, w2, b2, w3, b3, freq, w4d):
    n = 2 * seq_len
    tr = min(512, seq_len)
    z, deltas = _filter_tables(seq_len, channels)
    hid = FILTER_HIDDEN
    full = lambda shape: pl.BlockSpec(shape, lambda i: (0,) * len(shape))
    body = functools.partial(_filter_body, tr=tr, seq_len=seq_len, channels=channels)
    return pl.pallas_call(
        body,
        out_shape=(jax.ShapeDtypeStruct((n, 2 * channels), F32), jax.ShapeDtypeStruct((8, 2 * channels), F32)),
        grid=(n // tr,),
        in_specs=[
            pl.BlockSpec((tr, hid), lambda i: (i, 0)),
            full((hid, hid)), full((1, hid)), full((hid, hid)), full((1, hid)), full((hid, hid)), full((1, hid)),
            full((1, hid)),
            pl.BlockSpec((None, hid, 2 * channels), lambda i: ((i * tr) // seq_len, 0, 0)),
            full((1, channels)),
        ],
        out_specs=(pl.BlockSpec((tr, 2 * channels), lambda i: (i, 0)), full((8, 2 * channels))),
        compiler_params=_cparams(("arbitrary",)),
        name="filter_gen",
    )(z, w1p, b1, w2, b2, w3, b3, freq, w4d, deltas)


def _major_fwd_body(u_ref, a_ref, y_ref, *, groups, sub):
    a_in, _, ct = u_ref.shape
    kh = y_ref.shape[0]
    for q in range(groups):
        rows = slice(q * sub, (q + 1) * sub)
        u = u_ref[:, rows, :].reshape(a_in * sub, ct)
        y = _dot(a_ref[...], u.astype(BF16))
        y_ref[:, :, rows, :] = y.reshape(kh, 2, sub, ct).astype(y_ref.dtype)


def major_fwd(u, col, channels, a_fwd, kh, ct, groups, sub, out_dtype):
    bsz, a_in, n2, _ = u.shape
    nc = channels // ct
    rows = groups * sub
    body = functools.partial(_major_fwd_body, groups=groups, sub=sub)
    return pl.pallas_call(
        body,
        out_shape=jax.ShapeDtypeStruct((bsz, kh, 2, n2, channels), out_dtype),
        grid=(bsz, n2 // rows, nc),
        in_specs=[
            pl.BlockSpec((None, a_in, rows, ct), lambda b, r, j: (b, 0, r, col * nc + j)),
            pl.BlockSpec(a_fwd.shape, lambda b, r, j: (0, 0)),
        ],
        out_specs=pl.BlockSpec((None, kh, 2, rows, ct), lambda b, r, j: (b, 0, 0, r, j)),
        compiler_params=_cparams(("parallel", "parallel", "parallel")),
        name="major_fwd",
    )(u, a_fwd)


def _long_conv_body(u_ref, af_ref, ai_ref, wf_ref, wi_ref, k_ref, m_ref, o_ref, yg_ref, *,
                    nb, nk, kg, groups, sub, n2):
    s = pl.program_id(2)
    bb, a_len, rows, ct = u_ref.shape
    kh = yg_ref.shape[1]

    @pl.when(s < nb)
    def _():
        for i in range(bb):
            for q in range(groups):
                u = u_ref[i, :, q * sub:(q + 1) * sub, :].reshape(a_len * sub, ct)
                y = _dot(af_ref[...], u.astype(BF16)).reshape(kh, 2, sub, ct)
                dst = pl.ds(pl.multiple_of(s * rows + q * sub, sub), sub)
                yg_ref[i, :, :, dst, :] = y.astype(BF16)

    @pl.when(jnp.logical_and(s >= nb, s < nb + nk))
    def _():
        for kk in range(kg):
            k1 = (s - nb) * kg + kk
            kr = k_ref[kk, 0]
            ki = k_ref[kk, 1]
            for i in range(bb):
                z = _dot(wf_ref[kk], jnp.concatenate([yg_ref[i, k1, 0], yg_ref[i, k1, 1]], axis=0))
                zr, zi = z[:n2], z[n2:]
                pr = (zr * kr - zi * ki).astype(BF16)
                pi = (zr * ki + zi * kr).astype(BF16)
                g = _dot(wi_ref[kk], jnp.concatenate([pr, pi], axis=0))
                yg_ref[i, k1, 0] = g[:n2].astype(BF16)
                yg_ref[i, k1, 1] = g[n2:].astype(BF16)

    @pl.when(s >= nb + nk)
    def _():
        r = s - nb - nk
        for i in range(bb):
            for q in range(groups):
                src = pl.ds(pl.multiple_of(r * rows + q * sub, sub), sub)
                g = yg_ref[i, :, :, src, :].reshape(kh * 2 * sub, ct)
                y = _dot(ai_ref[...], g).reshape(a_len, sub, ct)
                blk = slice(q * sub, (q + 1) * sub)
                o_ref[i, :, blk, :] = (y * m_ref[i, :, blk, :].astype(F32)).astype(o_ref.dtype)


def _k1_group(kh):
    return max(g for g in range(1, 18) if kh % g == 0)


def long_conv_gated(u, u_col, mult, mult_col, spectra, order, seq_len, channels, out_dtype):
    cst = _fft_consts(seq_len)
    kh, n2 = cst["kh"], cst["n2"]
    bsz, a_len = u.shape[0], u.shape[1]
    sub = SUB_BF16
    ct = 256
    nc = channels // ct
    groups = _major_groups(a_len, sub, n2)
    rows = groups * sub
    nb = n2 // rows
    kg = _k1_group(kh)
    nk = kh // kg
    bb = 2 if bsz % 2 == 0 else 1
    a_fwd, a_inv = cst["a_fwd_half"], cst["a_inv"]

    def row_blk(s, first):
        return jnp.clip(s - first, 0, nb - 1)

    def k_blk(s):
        return jnp.clip(s - nb, 0, nk - 1)

    const = lambda shape: pl.BlockSpec(shape, lambda b, j, s: (0,) * len(shape), pipeline_mode=pl.Buffered(1))
    body = functools.partial(_long_conv_body, nb=nb, nk=nk, kg=kg, groups=groups, sub=sub, n2=n2)
    return pl.pallas_call(
        body,
        out_shape=jax.ShapeDtypeStruct((bsz, a_len, n2, channels), out_dtype),
        grid=(bsz // bb, nc, nb + nk + nb),
        in_specs=[
            pl.BlockSpec((bb, a_len, rows, ct), lambda b, j, s: (b, 0, row_blk(s, 0), u_col * nc + j)),
            const(a_fwd.shape),
            const(a_inv.shape),
            pl.BlockSpec((kg, 2 * n2, 2 * n2), lambda b, j, s: (k_blk(s), 0, 0)),
            pl.BlockSpec((kg, 2 * n2, 2 * n2), lambda b, j, s: (k_blk(s), 0, 0)),
            pl.BlockSpec((kg, 2, n2, ct), lambda b, j, s: (k_blk(s), 0, 0, order * nc + j)),
            pl.BlockSpec((bb, a_len, rows, ct), lambda b, j, s: (b, 0, row_blk(s, nb + nk), mult_col * nc + j)),
        ],
        out_specs=pl.BlockSpec((bb, a_len, rows, ct), lambda b, j, s: (b, 0, row_blk(s, nb + nk), j)),
        scratch_shapes=[pltpu.VMEM((bb, kh, 2, n2, ct), BF16)],
        compiler_params=_cparams(("parallel", "parallel", "arbitrary")),
        name="long_conv",
    )(u, a_fwd, a_inv, cst["w_fwd"], cst["w_inv"], spectra, mult)


def _minor_filter_body(y_ref, wf_ref, inv_ref, bias_ref, k_ref, *, n2):
    z = _dot(wf_ref[...], jnp.concatenate([y_ref[0].astype(BF16), y_ref[1].astype(BF16)], axis=0))
    inv = inv_ref[...]
    k_ref[0] = z[:n2] * inv + bias_ref[...]
    k_ref[1] = z[n2:] * inv


def minor_filter(y, w_fwd, inv_norm, bias, ct):
    kh, _, n2, c2 = y.shape
    y_spec = pl.BlockSpec((None, 2, n2, ct), lambda k, j: (k, 0, 0, j))
    v_spec = pl.BlockSpec((1, ct), lambda k, j: (0, j))
    body = functools.partial(_minor_filter_body, n2=n2)
    return pl.pallas_call(
        body,
        out_shape=jax.ShapeDtypeStruct(y.shape, F32),
        grid=(kh, c2 // ct),
        in_specs=[y_spec, pl.BlockSpec((None, 2 * n2, 2 * n2), lambda k, j: (k, 0, 0)), v_spec, v_spec],
        out_specs=y_spec,
        compiler_params=_cparams(("parallel", "parallel")),
        name="minor_filter",
    )(y, w_fwd, inv_norm, bias)


def _major_groups(a_in, sub, n2):
    return min(n2 // sub, max(1, 2048 // (a_in * sub)))


def hyena_filter_spectra(seq_len, channels, fw, f_bias):
    cst = _fft_consts(seq_len)
    n1, n2, kh = cst["n1"], cst["n2"], cst["kh"]
    kern, sums = filter_gen(seq_len, channels, *fw)
    inv_norm = 1.0 / jnp.sum(sums, axis=0, keepdims=True)
    u = kern.reshape(1, n1, n2, 2 * channels)
    y = major_fwd(u, 0, 2 * channels, cst["a_fwd"], kh, ct=512, groups=_major_groups(n1, SUB, n2), sub=SUB,
                  out_dtype=F32)
    return minor_filter(y[0], cst["w_fwd"], inv_norm, f_bias.reshape(1, 2 * channels), ct=1024)


def _row_tile(seq_len, want):
    return min(want, seq_len)


def _layer(x, mem, lw, ws, layer, batch, seq_len, dims):
    d, dh, da, dm = dims
    n_heads = da // HEAD_DIM
    n_mem = mem.shape[0] // batch

    proj = norm_matmul_conv(x, lw["g_pre_mix"], ws["w_in"], layer, lw["conv_w"], lw["conv_b"], seq_len,
                            tm=_row_tile(seq_len, 1024), tn=512)

    cst = _fft_consts(seq_len)
    p4 = proj.reshape(batch, cst["a_out"], cst["n2"], proj.shape[1])
    spectra = lw["spectra"][seq_len]
    z = long_conv_gated(p4, 0, p4, 1, spectra, 0, seq_len, dh, BF16)
    y_h = long_conv_gated(z, 0, p4, 2, spectra, 1, seq_len, dh, BF16).reshape(batch * seq_len, dh)

    qcol = 3 * dh // HEAD_DIM
    y_a = dil_attn(proj, batch, seq_len, qcol, qcol + n_heads, qcol + 2 * n_heads, n_heads)
    kv = norm_matmul(mem, lw["g_mem"], ws["w_mem_kv"], layer, tm=n_mem, tn=512)
    y_m = mem_attn(proj, kv, batch, seq_len, n_mem, qcol + 3 * n_heads, dm // HEAD_DIM, tq=_row_tile(seq_len, 1024))

    x = mix_out(y_h, y_a, y_m, lw["g_grp"], ws["w_out"], layer, lw["g_post_mix"], x, _row_tile(seq_len, 512))
    return ffn(x, lw["g_pre_ffn"], ws["w_up"], lw["ffn_conv_w"], lw["ffn_conv_b"], ws["w_down"],
               lw["g_post_ffn"], layer, seq_len, _row_tile(seq_len, 1024), tf=512)


def kernel(x_prompt, x_sample, mem_prompt, mem_sample, g_pre_mix, w_in, conv_w, conv_b, f_w1, f_b1, f_w2, f_b2, f_w3, f_b3, f_w4, f_freq, f_bias, g_mem, w_mem_kv, g_grp, w_out, g_post_mix, g_pre_ffn, w_up, ffn_conv_w, ffn_conv_b, w_down, g_post_ffn):
    depth, d, d_in = w_in.shape
    dh = conv_w.shape[2] // 3
    dm = w_mem_kv.shape[2] // 2
    da = (d_in - 3 * dh - dm) // 3
    dims = (d, dh, da, dm)
    groups = [(x_prompt, mem_prompt), (x_sample, mem_sample)]
    seq_lens = sorted({g[0].shape[1] for g in groups})

    xs = [g[0].reshape(-1, d) for g in groups]
    mems = [g[1].reshape(-1, d) for g in groups]
    hid = FILTER_HIDDEN
    ws = dict(w_in=w_in.astype(BF16), w_mem_kv=w_mem_kv.astype(BF16), w_out=w_out.astype(BF16),
              w_up=w_up.astype(BF16), w_down=w_down.astype(BF16))
    for i in range(depth):
        row = lambda v: v[i][None, :]
        w1p = jnp.zeros((hid, hid), F32).at[:FILTER_EMB].set(f_w1[i])
        w4d = f_w4[i].reshape(hid, 2, 2, dh).transpose(2, 0, 1, 3).reshape(2, hid, 2 * dh)
        fw = (w1p, row(f_b1), f_w2[i], row(f_b2), f_w3[i], row(f_b3), row(f_freq), w4d)
        lw = dict(
            g_pre_mix=row(g_pre_mix), conv_w=conv_w[i], conv_b=row(conv_b),
            g_mem=row(g_mem), g_grp=row(g_grp), g_post_mix=row(g_post_mix), g_pre_ffn=row(g_pre_ffn),
            ffn_conv_w=ffn_conv_w[i], ffn_conv_b=row(ffn_conv_b), g_post_ffn=row(g_post_ffn),
            spectra={sl: hyena_filter_spectra(sl, dh, fw, f_bias[i]) for sl in seq_lens},
        )
        for gi, (xg, _) in enumerate(groups):
            xs[gi] = _layer(xs[gi], mems[gi], lw, ws, i, xg.shape[0], xg.shape[1], dims)
    return tuple(x.reshape(g[0].shape) for x, g in zip(xs, groups))
```

```python
import functools
import math

import numpy as np
import jax
import jax.numpy as jnp
from jax import lax
from jax.experimental import pallas as pl
from jax.experimental.pallas import tpu as pltpu

F32 = jnp.float32
BF16 = jnp.bfloat16

HEAD_DIM = 128
DILATED_PATTERNS = ((128, 1), (512, 4), (2048, 16))
ROPE_THETA = 500000.0
ROT_DIM = HEAD_DIM // 4
FILTER_EMB = 33
FILTER_HIDDEN = 64
DECAY_FAST = 0.3
DECAY_SLOW = 1.5
DECAY_TARGET = 1e-2
DECAY_SHIFT = 0.05
EPS = 1e-6
NEG = -1e30

HALO = 16
ATTN_BLOCKS_PER_STEP = 8
SUB = 8
SUB_BF16 = 16
MXU_DEPTH = 256
VMEM_LIMIT = 56 * 1024 * 1024


def _cparams(sem):
    return pltpu.CompilerParams(dimension_semantics=sem, vmem_limit_bytes=VMEM_LIMIT)


def _split(x):
    hi = x.astype(BF16)
    lo = (x - hi.astype(F32)).astype(BF16)
    return hi, lo


def _dot(a, b):
    return jnp.dot(a, b, preferred_element_type=F32)


def _rms(v, g):
    return v * lax.rsqrt(jnp.mean(v * v, axis=-1, keepdims=True) + EPS) * g


def _fill_normed(hb_ref, xp_ref, x_ref, xn_ref, g_ref, i, tm, blocks_per_seq):
    g = g_ref[...]
    pos = i % blocks_per_seq
    keep_p = (pos != 0).astype(F32)
    keep_n = (pos != blocks_per_seq - 1).astype(F32)
    hb_ref[0:HALO, :] = (_rms(xp_ref[...], g) * keep_p).astype(BF16)
    hb_ref[HALO:HALO + tm, :] = _rms(x_ref[...], g).astype(BF16)
    hb_ref[HALO + tm:, :] = (_rms(xn_ref[...], g) * keep_n).astype(BF16)


def _conv3(u, cw, cb, tm):
    rows = tm + 2 * HALO
    up = pltpu.roll(u, 1, 0)[HALO:HALO + tm]
    un = pltpu.roll(u, rows - 1, 0)[HALO:HALO + tm]
    uc = u[HALO:HALO + tm]
    return up * cw[0:1] + uc * cw[1:2] + un * cw[2:3] + cb


def _nm_conv_body(xp_ref, x_ref, xn_ref, g_ref, w_ref, cw_ref, cb_ref, o_ref, hb_ref, *, tm, blocks_per_seq,
                  n_conv):
    i = pl.program_id(0)
    j = pl.program_id(1)

    @pl.when(j == 0)
    def _():
        _fill_normed(hb_ref, xp_ref, x_ref, xn_ref, g_ref, i, tm, blocks_per_seq)

    u = _dot(hb_ref[...], w_ref[...])

    @pl.when(j < n_conv)
    def _():
        o_ref[...] = _conv3(u, cw_ref[...], cb_ref[...], tm).astype(o_ref.dtype)

    @pl.when(j >= n_conv)
    def _():
        o_ref[...] = u[HALO:HALO + tm].astype(o_ref.dtype)


def _nm_body(x_ref, g_ref, w_ref, o_ref, hb_ref):
    @pl.when(pl.program_id(1) == 0)
    def _():
        hb_ref[...] = _rms(x_ref[...], g_ref[...]).astype(BF16)

    o_ref[...] = _dot(hb_ref[...], w_ref[...]).astype(o_ref.dtype)


def _halo_specs(tm, d, n_rows, x_buffers=2):
    hb = tm // HALO
    last = n_rows // HALO - 1
    return [
        pl.BlockSpec((HALO, d), lambda i, j: (jnp.maximum(i * hb - 1, 0), 0)),
        pl.BlockSpec((tm, d), lambda i, j: (i, 0), pipeline_mode=pl.Buffered(x_buffers)),
        pl.BlockSpec((HALO, d), lambda i, j: (jnp.minimum((i + 1) * hb, last), 0)),
    ]


def norm_matmul_conv(x, g, w, layer, cw, cb, seq_len, tm, tn):
    t, d = x.shape
    n = w.shape[2]
    n_conv = cw.shape[1] // tn
    body = functools.partial(_nm_conv_body, tm=tm, blocks_per_seq=seq_len // tm, n_conv=n_conv)
    return pl.pallas_call(
        body,
        out_shape=jax.ShapeDtypeStruct((t, n), BF16),
        grid=(t // tm, n // tn),
        in_specs=_halo_specs(tm, d, t) + [
            pl.BlockSpec((1, d), lambda i, j: (0, 0)),
            pl.BlockSpec((None, d, tn), lambda i, j: (layer, 0, j)),
            pl.BlockSpec((3, tn), lambda i, j: (0, jnp.minimum(j, n_conv - 1))),
            pl.BlockSpec((1, tn), lambda i, j: (0, jnp.minimum(j, n_conv - 1))),
        ],
        out_specs=pl.BlockSpec((tm, tn), lambda i, j: (i, j)),
        scratch_shapes=[pltpu.VMEM((tm + 2 * HALO, d), BF16)],
        compiler_params=_cparams(("parallel", "arbitrary")),
        name="norm_matmul_conv",
    )(x, x, x, g, w, cw, cb)


def norm_matmul(x, g, w, layer, tm, tn):
    t, d = x.shape
    n = w.shape[2]
    return pl.pallas_call(
        _nm_body,
        out_shape=jax.ShapeDtypeStruct((t, n), BF16),
        grid=(t // tm, n // tn),
        in_specs=[
            pl.BlockSpec((tm, d), lambda i, j: (i, 0)),
            pl.BlockSpec((1, d), lambda i, j: (0, 0)),
            pl.BlockSpec((None, d, tn), lambda i, j: (layer, 0, j)),
        ],
        out_specs=pl.BlockSpec((tm, tn), lambda i, j: (i, j)),
        scratch_shapes=[pltpu.VMEM((tm, d), BF16)],
        compiler_params=_cparams(("parallel", "arbitrary")),
        name="norm_matmul",
    )(x, g, w)


def _ffn_body(xp_ref, x_ref, xn_ref, g_ref, wg_ref, wv_ref, cw_ref, cb_ref, wd_ref, gp_ref, o_ref,
              hb_ref, *, tm, blocks_per_seq, nj):
    i = pl.program_id(0)
    j = pl.program_id(1)

    @pl.when(j == 0)
    def _():
        _fill_normed(hb_ref, xp_ref, x_ref, xn_ref, g_ref, i, tm, blocks_per_seq)
        o_ref[...] = jnp.zeros_like(o_ref)

    ug = _dot(hb_ref[...], wg_ref[...])
    val = _dot(hb_ref[HALO:HALO + tm, :], wv_ref[...])
    gate = _conv3(ug, cw_ref[...], cb_ref[...], tm)
    ff = jax.nn.gelu(gate, approximate=True) * val
    o_ref[...] += _dot(ff.astype(BF16), wd_ref[...])

    @pl.when(j == nj - 1)
    def _():
        o_ref[...] = x_ref[...] + _rms(o_ref[...], gp_ref[...])


def ffn(x, g_pre, w_up, cw, cb, w_down, g_post, layer, seq_len, tm, tf):
    t, d = x.shape
    d_ff = w_down.shape[1]
    nj = d_ff // tf
    body = functools.partial(_ffn_body, tm=tm, blocks_per_seq=seq_len // tm, nj=nj)
    return pl.pallas_call(
        body,
        out_shape=jax.ShapeDtypeStruct((t, d), F32),
        grid=(t // tm, nj),
        in_specs=_halo_specs(tm, d, t, x_buffers=1) + [
            pl.BlockSpec((1, d), lambda i, j: (0, 0)),
            pl.BlockSpec((None, d, tf), lambda i, j: (layer, 0, j)),
            pl.BlockSpec((None, d, tf), lambda i, j: (layer, 0, j + nj)),
            pl.BlockSpec((3, tf), lambda i, j: (0, j)),
            pl.BlockSpec((1, tf), lambda i, j: (0, j)),
            pl.BlockSpec((None, tf, d), lambda i, j: (layer, j, 0)),
            pl.BlockSpec((1, d), lambda i, j: (0, 0)),
        ],
        out_specs=pl.BlockSpec((tm, d), lambda i, j: (i, 0)),
        scratch_shapes=[pltpu.VMEM((tm + 2 * HALO, d), BF16)],
        compiler_params=_cparams(("parallel", "arbitrary")),
        name="ffn",
    )(x, x, x, g_pre, w_up, w_up, cw, cb, w_down, g_post)


def _mix_out_body(yh_ref, ya_ref, ym_ref, gg_ref, w_ref, gp_ref, x_ref, o_ref, *, dh, da):
    gg = gg_ref[...]
    acc = _dot(_rms(yh_ref[...].astype(F32), gg[:, :dh]).astype(BF16), w_ref[0:dh, :])
    acc += _dot(_rms(ya_ref[...].astype(F32), gg[:, dh:dh + da]).astype(BF16), w_ref[dh:dh + da, :])
    acc += _dot(_rms(ym_ref[...].astype(F32), gg[:, dh + da:]).astype(BF16), w_ref[dh + da:, :])
    o_ref[...] = x_ref[...] + _rms(acc, gp_ref[...])


def mix_out(yh, ya, ym, g_grp, w_out, layer, g_post, x, tm):
    t, d = x.shape
    dh, da, dm = yh.shape[1], ya.shape[1], ym.shape[1]
    body = functools.partial(_mix_out_body, dh=dh, da=da)
    return pl.pallas_call(
        body,
        out_shape=jax.ShapeDtypeStruct((t, d), F32),
        grid=(t // tm,),
        in_specs=[
            pl.BlockSpec((tm, dh), lambda i: (i, 0)),
            pl.BlockSpec((tm, da), lambda i: (i, 0)),
            pl.BlockSpec((tm, dm), lambda i: (i, 0)),
            pl.BlockSpec((1, d), lambda i: (0, 0)),
            pl.BlockSpec((None, d, d), lambda i: (layer, 0, 0)),
            pl.BlockSpec((1, d), lambda i: (0, 0)),
            pl.BlockSpec((tm, d), lambda i: (i, 0)),
        ],
        out_specs=pl.BlockSpec((tm, d), lambda i: (i, 0)),
        compiler_params=_cparams(("parallel",)),
        name="mix_out",
    )(yh, ya, ym, g_grp, w_out, g_post, x)


def _rope_tables(seq_len):
    half = ROT_DIM // 2
    inv_freq = np.exp(-math.log(ROPE_THETA) * np.arange(0, ROT_DIM, 2, dtype=np.float64) / ROT_DIM)
    ang = np.arange(seq_len, dtype=np.float64)[:, None] * inv_freq.astype(np.float32).astype(np.float64)[None, :]
    c, s = np.cos(ang), np.sin(ang)
    cos_t = np.ones((seq_len, HEAD_DIM), np.float64)
    sin_t = np.zeros((seq_len, HEAD_DIM), np.float64)
    cos_t[:, :half] = c
    cos_t[:, half:ROT_DIM] = c
    sin_t[:, :half] = -s
    sin_t[:, half:ROT_DIM] = s
    return cos_t.astype(np.float32), sin_t.astype(np.float32)


def _dil_attn_body(q_ref, k_ref, v_ref, cos_ref, sin_ref, o_ref, qs_ref, ks_ref, vs_ref, m_ref, l_ref, *, seq_len):
    half = ROT_DIM // 2
    scale = 1.0 / math.sqrt(HEAD_DIM)
    chunk = min(512, seq_len)

    def rope_chunk(c, carry):
        rows = pl.ds(pl.multiple_of(c * chunk, chunk), chunk)
        cs = cos_ref[rows, :]
        sn = sin_ref[rows, :]
        lane = lax.broadcasted_iota(jnp.int32, (chunk, HEAD_DIM), 1)
        for src, dst, mul in ((q_ref, qs_ref, scale), (k_ref, ks_ref, None)):
            x = src[rows, :].astype(F32)
            partner = jnp.where(lane < half, pltpu.roll(x, HEAD_DIM - half, 1), pltpu.roll(x, half, 1))
            y = x * cs + partner * sn
            dst[rows, :] = y if mul is None else y * mul
        vs_ref[rows, :] = v_ref[rows, :].astype(F32)
        return carry

    lax.fori_loop(0, seq_len // chunk, rope_chunk, 0)

    n_br = len(DILATED_PATTERNS)
    for bi, (window, dil) in enumerate(DILATED_PATTERNS):
        radius = window // (2 * dil)
        n = seq_len // dil
        tq = min(128, n)
        kw = min(tq + 2 * radius, n)
        nblk = n // tq

        def one_block(idx, bi=bi, dil=dil, radius=radius, n=n, tq=tq, kw=kw, nblk=nblk):
            r = idx // nblk
            q0 = (idx % nblk) * tq
            k0 = jnp.clip(q0 - radius, 0, n - kw)
            if dil == 1:
                qsl = pl.ds(pl.multiple_of(q0, tq), tq)
                ksl = pl.ds(pl.multiple_of(k0, 8), kw)
            else:
                qsl = pl.ds(r + q0 * dil, tq, stride=dil)
                ksl = pl.ds(r + k0 * dil, kw, stride=dil)
            qb = qs_ref[qsl, :].astype(BF16)
            kb = ks_ref[ksl, :].astype(BF16)
            vb = vs_ref[ksl, :].astype(BF16)
            s = lax.dot_general(qb, kb, (((1,), (1,)), ((), ())), preferred_element_type=F32)
            rel = (k0 + lax.broadcasted_iota(jnp.int32, (tq, kw), 1)) - (q0 + lax.broadcasted_iota(jnp.int32, (tq, kw), 0))
            s = jnp.where(jnp.abs(rel) <= radius, s, NEG)
            m = jnp.max(s, axis=-1, keepdims=True)
            p = jnp.exp(s - m)
            l = jnp.sum(p, axis=-1, keepdims=True)
            acc = _dot(p.astype(BF16), vb)
            m = jnp.broadcast_to(m, (tq, HEAD_DIM))
            l = jnp.broadcast_to(l, (tq, HEAD_DIM))
            if bi > 0:
                m_old = m_ref[qsl, :]
                m_new = jnp.maximum(m_old, m)
                a_old = jnp.exp(m_old - m_new)
                a_new = jnp.exp(m - m_new)
                acc = o_ref[qsl, :] * a_old + acc * a_new
                l = l_ref[qsl, :] * a_old + l * a_new
                m = m_new
            return qsl, m, l, acc

        total = dil * nblk
        unroll = min(ATTN_BLOCKS_PER_STEP, total)

        def step(it, carry, bi=bi, unroll=unroll, one_block=one_block):
            done = [one_block(it * unroll + u) for u in range(unroll)]
            for qsl, m, l, acc in done:
                if bi == n_br - 1:
                    o_ref[qsl, :] = acc / l
                else:
                    o_ref[qsl, :] = acc
                    m_ref[qsl, :] = m
                    l_ref[qsl, :] = l
            return carry

        lax.fori_loop(0, total // unroll, step, 0)


def dil_attn(proj, batch, seq_len, q_col, k_col, v_col, n_heads):
    cos_t, sin_t = _rope_tables(seq_len)
    nb = 1 if seq_len * HEAD_DIM * 4 > (2 << 20) else 2

    def col_spec(col):
        return pl.BlockSpec((seq_len, HEAD_DIM), lambda b, h: (b, col + h))

    tab_spec = pl.BlockSpec((seq_len, HEAD_DIM), lambda b, h: (0, 0), pipeline_mode=pl.Buffered(1))
    body = functools.partial(_dil_attn_body, seq_len=seq_len)
    return pl.pallas_call(
        body,
        out_shape=jax.ShapeDtypeStruct((batch * seq_len, n_heads * HEAD_DIM), F32),
        grid=(batch, n_heads),
        in_specs=[col_spec(q_col), col_spec(k_col), col_spec(v_col), tab_spec, tab_spec],
        out_specs=pl.BlockSpec((seq_len, HEAD_DIM), lambda b, h: (b, h), pipeline_mode=pl.Buffered(nb)),
        scratch_shapes=[pltpu.VMEM((seq_len, HEAD_DIM), F32)] * 5,
        compiler_params=_cparams(("parallel", "parallel")),
        name="dil_attn",
    )(proj, proj, proj, cos_t, sin_t)


def _mem_attn_body(q_ref, k_ref, v_ref, o_ref):
    s = lax.dot_general(q_ref[...].astype(BF16), k_ref[...].astype(BF16), (((1,), (1,)), ((), ())),
                        preferred_element_type=F32) * (1.0 / math.sqrt(HEAD_DIM))
    m = jnp.max(s, axis=-1, keepdims=True)
    p = jnp.exp(s - m)
    l = jnp.sum(p, axis=-1, keepdims=True)
    o_ref[...] = (_dot(p.astype(BF16), v_ref[...].astype(BF16)) / l).astype(o_ref.dtype)


def mem_attn(proj, kv, batch, seq_len, n_mem, q_col, n_heads, tq):
    nq = seq_len // tq
    return pl.pallas_call(
        _mem_attn_body,
        out_shape=jax.ShapeDtypeStruct((batch * seq_len, n_heads * HEAD_DIM), BF16),
        grid=(batch, nq, n_heads),
        in_specs=[
            pl.BlockSpec((tq, HEAD_DIM), lambda b, i, h: (b * nq + i, q_col + h)),
            pl.BlockSpec((n_mem, HEAD_DIM), lambda b, i, h: (b, h)),
            pl.BlockSpec((n_mem, HEAD_DIM), lambda b, i, h: (b, n_heads + h)),
        ],
        out_specs=pl.BlockSpec((tq, HEAD_DIM), lambda b, i, h: (b * nq + i, h)),
        compiler_params=_cparams(("parallel", "parallel", "parallel")),
        name="mem_attn",
    )(proj, kv, kv)


def _fft_split(seq_len):
    n = 2 * seq_len
    n2 = 128
    return n // n2, n2


@functools.lru_cache(maxsize=None)
def _fft_consts(seq_len):
    n1, n2 = _fft_split(seq_len)
    n = n1 * n2
    kh = n1 // 2 + 1
    a_out = n1 // 2
    a = np.arange(n1)
    k1 = np.arange(kh)
    th = 2.0 * np.pi * ((k1[:, None] * a[None, :]) % n1) / n1
    fwd = np.stack([np.cos(th), -np.sin(th)], axis=1).reshape(2 * kh, n1)
    alpha = np.where((k1 == 0) | (k1 == n1 // 2), 1.0, 2.0) / n
    th_i = th[:, :a_out].T
    inv = np.stack([np.cos(th_i) * alpha[None, :], -np.sin(th_i) * alpha[None, :]], axis=2)
    inv = inv.reshape(a_out, 2 * kh)
    b = np.arange(n2)
    k2 = np.arange(n2)
    kk = k1[:, None, None] + n1 * k2[None, :, None]
    ph = 2.0 * np.pi * ((kk * b[None, None, :]) % n) / n
    wr, wi = np.cos(ph), -np.sin(ph)
    vr, vi = np.transpose(wr, (0, 2, 1)), -np.transpose(wi, (0, 2, 1))

    def block(re, im):
        blk = np.concatenate([np.concatenate([re, -im], axis=2), np.concatenate([im, re], axis=2)], axis=1)
        return np.asarray(blk, np.float32).astype(BF16)

    def kron_bf16(m, sub):
        return np.asarray(np.kron(m, np.eye(sub)), np.float32).astype(BF16)

    sub_sig = SUB if a_out * SUB >= MXU_DEPTH else SUB_BF16
    return dict(n1=n1, n2=n2, kh=kh, a_out=a_out, sub_sig=sub_sig,
                a_fwd=kron_bf16(fwd, SUB),
                a_fwd_half=kron_bf16(fwd[:, :a_out], sub_sig),
                a_inv=kron_bf16(inv, sub_sig),
                w_fwd=block(wr, wi), w_inv=block(vr, vi))


def _filter_tables(seq_len, channels):
    n = 2 * seq_len
    bands = (FILTER_EMB - 1) // 2
    pos = np.concatenate([np.arange(seq_len), [0], np.arange(seq_len - 1, 0, -1)])
    t = np.linspace(0.0, 1.0, seq_len)[pos]
    w = 2.0 * np.pi * pos / seq_len
    f = np.linspace(1e-4, bands - 1, bands)
    ang = w[:, None] * f[None, :]
    z = np.zeros((n, FILTER_HIDDEN), np.float64)
    z[:, 0] = t
    z[:, 1:1 + bands] = np.cos(ang)
    z[:, 1 + bands:1 + 2 * bands] = -np.sin(ang)
    deltas = np.abs(np.linspace(math.log(DECAY_TARGET) / DECAY_SLOW, math.log(DECAY_TARGET) / DECAY_FAST, channels))
    return z.astype(np.float32), deltas[None, :].astype(np.float32)


def _filter_body(z_ref, w1_ref, b1_ref, w2_ref, b2_ref, w3_ref, b3_ref, fr_ref, w4_ref, dl_ref,
                 k_ref, s_ref, *, tr, seq_len, channels):
    i = pl.program_id(0)
    fr = fr_ref[...]
    z = z_ref[...]

    def layer(h, w_ref, b_ref):
        w_hi, w_lo = _split(w_ref[...])
        h_hi, h_lo = _split(h)
        pre = _dot(h_hi, w_hi) + _dot(h_lo, w_hi) + _dot(h_hi, w_lo)
        return jnp.sin(fr * (pre + b_ref[...]))

    h = layer(z, w1_ref, b1_ref)
    h = layer(h, w2_ref, b2_ref)
    h = layer(h, w3_ref, b3_ref)
    w_hi, w_lo = _split(w4_ref[...])
    h_hi, h_lo = _split(h)
    k = _dot(h_hi, w_hi) + _dot(h_lo, w_hi) + _dot(h_hi, w_lo)
    decay = jnp.exp(-z[:, 0:1] * dl_ref[...]) + DECAY_SHIFT
    row = i * tr + lax.broadcasted_iota(jnp.int32, (tr, 1), 0)
    decay = jnp.where(row == seq_len, 0.0, decay)
    k = k * jnp.concatenate([decay, decay], axis=1)
    k_ref[...] = k

    @pl.when(i == 0)
    def _():
        s_ref[...] = jnp.zeros_like(s_ref)

    s_ref[...] += jnp.sum(jnp.abs(k).reshape(tr // 8, 8, 2 * channels), axis=0)


def filter_gen(seq_len, channels, w1p, b1, w2, b2, w3, b3, freq, w4d):
    n = 2 * seq_len
    tr = min(512, seq_len)
    z, deltas = _filter_tables(seq_len, channels)
    hid = FILTER_HIDDEN
    full = lambda shape: pl.BlockSpec(shape, lambda i: (0,) * len(shape))
    body = functools.partial(_filter_body, tr=tr, seq_len=seq_len, channels=channels)
    return pl.pallas_call(
        body,
        out_shape=(jax.ShapeDtypeStruct((n, 2 * channels), F32), jax.ShapeDtypeStruct((8, 2 * channels), F32)),
        grid=(n // tr,),
        in_specs=[
            pl.BlockSpec((tr, hid), lambda i: (i, 0)),
            full((hid, hid)), full((1, hid)), full((hid, hid)), full((1, hid)), full((hid, hid)), full((1, hid)),
            full((1, hid)),
            pl.BlockSpec((None, hid, 2 * channels), lambda i: ((i * tr) // seq_len, 0, 0)),
            full((1, channels)),
        ],
        out_specs=(pl.BlockSpec((tr, 2 * channels), lambda i: (i, 0)), full((8, 2 * channels))),
        compiler_params=_cparams(("arbitrary",)),
        name="filter_gen",
    )(z, w1p, b1, w2, b2, w3, b3, freq, w4d, deltas)


def _major_fwd_body(u_ref, a_ref, y_ref, *, groups, sub):
    a_in, _, ct = u_ref.shape
    kh = y_ref.shape[0]
    ys = []
    for q in range(groups):
        u = u_ref[:, q * sub:(q + 1) * sub, :].reshape(a_in * sub, ct)
        ys.append(_dot(a_ref[...], u.astype(BF16)).reshape(kh, 2, sub, ct))
    y_ref[...] = jnp.concatenate(ys, axis=2).astype(y_ref.dtype)


def major_fwd(u, col, channels, a_fwd, kh, ct, groups, sub, out_dtype):
    bsz, a_in, n2, _ = u.shape
    nc = channels // ct
    rows = groups * sub
    body = functools.partial(_major_fwd_body, groups=groups, sub=sub)
    return pl.pallas_call(
        body,
        out_shape=jax.ShapeDtypeStruct((bsz, kh, 2, n2, channels), out_dtype),
        grid=(bsz, n2 // rows, nc),
        in_specs=[
            pl.BlockSpec((None, a_in, rows, ct), lambda b, r, j: (b, 0, r, col * nc + j)),
            pl.BlockSpec(a_fwd.shape, lambda b, r, j: (0, 0)),
        ],
        out_specs=pl.BlockSpec((None, kh, 2, rows, ct), lambda b, r, j: (b, 0, 0, r, j)),
        compiler_params=_cparams(("parallel", "parallel", "parallel")),
        name="major_fwd",
    )(u, a_fwd)


def _long_conv_body(u_ref, af_ref, ai_ref, wf_ref, wi_ref, k_ref, m_ref, o_ref, yg_ref, *,
                    nb, nk, kg, groups, sub, n2):
    s = pl.program_id(2)
    bb, a_len, rows, ct = u_ref.shape
    kh = yg_ref.shape[1]
    sdt = yg_ref.dtype
    wide = F32 if sub == SUB else u_ref.dtype

    @pl.when(s < nb)
    def _():
        for i in range(bb):
            ub = u_ref[i].astype(wide)
            for q in range(groups):
                u = ub[:, q * sub:(q + 1) * sub, :].reshape(a_len * sub, ct)
                y = _dot(af_ref[...], u.astype(BF16)).reshape(kh, 2, sub, ct)
                dst = pl.ds(pl.multiple_of(s * rows + q * sub, sub), sub)
                yg_ref[i, :, :, dst, :] = y.astype(sdt)

    @pl.when(jnp.logical_and(s >= nb, s < nb + nk))
    def _():
        for kk in range(kg):
            k1 = (s - nb) * kg + kk
            kr = k_ref[kk, 0].astype(F32)
            ki = k_ref[kk, 1].astype(F32)
            for i in range(bb):
                yv = jnp.concatenate([yg_ref[i, k1, 0].astype(BF16), yg_ref[i, k1, 1].astype(BF16)], axis=0)
                z = _dot(wf_ref[kk], yv)
                zr, zi = z[:n2], z[n2:]
                pr = (zr * kr - zi * ki).astype(BF16)
                pi = (zr * ki + zi * kr).astype(BF16)
                g = _dot(wi_ref[kk], jnp.concatenate([pr, pi], axis=0))
                yg_ref[i, k1, 0] = g[:n2].astype(sdt)
                yg_ref[i, k1, 1] = g[n2:].astype(sdt)

    @pl.when(s >= nb + nk)
    def _():
        r = s - nb - nk
        for i in range(bb):
            ys = []
            for q in range(groups):
                src = pl.ds(pl.multiple_of(r * rows + q * sub, sub), sub)
                g = yg_ref[i, :, :, src, :].reshape(kh * 2 * sub, ct)
                ys.append(_dot(ai_ref[...], g.astype(BF16)).reshape(a_len, sub, ct))
            y = jnp.concatenate(ys, axis=1)
            o_ref[i] = (y * m_ref[i].astype(F32)).astype(o_ref.dtype)


def _k1_group(kh):
    return max(g for g in range(1, 18) if kh % g == 0)


def long_conv_gated(u, u_col, mult, mult_col, spectra, order, seq_len, channels, out_dtype):
    cst = _fft_consts(seq_len)
    kh, n2 = cst["kh"], cst["n2"]
    bsz, a_len = u.shape[0], u.shape[1]
    sub = cst["sub_sig"]
    ct = 256
    nc = channels // ct
    groups = _major_groups(a_len, sub, n2)
    rows = groups * sub
    nb = n2 // rows
    kg = _k1_group(kh)
    nk = kh // kg
    bb = max(g for g in (1, 2) if bsz % g == 0)
    a_fwd, a_inv = cst["a_fwd_half"], cst["a_inv"]

    def row_blk(s, first):
        return jnp.clip(s - first, 0, nb - 1)

    def k_blk(s):
        return jnp.clip(s - nb, 0, nk - 1)

    const = lambda shape: pl.BlockSpec(shape, lambda b, j, s: (0,) * len(shape), pipeline_mode=pl.Buffered(1))
    body = functools.partial(_long_conv_body, nb=nb, nk=nk, kg=kg, groups=groups, sub=sub, n2=n2)
    return pl.pallas_call(
        body,
        out_shape=jax.ShapeDtypeStruct((bsz, a_len, n2, channels), out_dtype),
        grid=(bsz // bb, nc, nb + nk + nb),
        in_specs=[
            pl.BlockSpec((bb, a_len, rows, ct), lambda b, j, s: (b, 0, row_blk(s, 0), u_col * nc + j)),
            const(a_fwd.shape),
            const(a_inv.shape),
            pl.BlockSpec((kg, 2 * n2, 2 * n2), lambda b, j, s: (k_blk(s), 0, 0)),
            pl.BlockSpec((kg, 2 * n2, 2 * n2), lambda b, j, s: (k_blk(s), 0, 0)),
            pl.BlockSpec((kg, 2, n2, ct), lambda b, j, s: (k_blk(s), 0, 0, order * nc + j)),
            pl.BlockSpec((bb, a_len, rows, ct), lambda b, j, s: (b, 0, row_blk(s, nb + nk), mult_col * nc + j)),
        ],
        out_specs=pl.BlockSpec((bb, a_len, rows, ct), lambda b, j, s: (b, 0, row_blk(s, nb + nk), j)),
        scratch_shapes=[pltpu.VMEM((bb, kh, 2, n2, ct), F32 if sub == SUB else BF16)],
        compiler_params=_cparams(("parallel", "parallel", "arbitrary")),
        name="long_conv",
    )(u, a_fwd, a_inv, cst["w_fwd"], cst["w_inv"], spectra, mult)


def _minor_filter_body(y_ref, wf_ref, inv_ref, bias_ref, k_ref, *, n2):
    z = _dot(wf_ref[...], jnp.concatenate([y_ref[0].astype(BF16), y_ref[1].astype(BF16)], axis=0))
    inv = inv_ref[...]
    k_ref[0] = (z[:n2] * inv + bias_ref[...]).astype(k_ref.dtype)
    k_ref[1] = (z[n2:] * inv).astype(k_ref.dtype)


def minor_filter(y, w_fwd, inv_norm, bias, ct):
    kh, _, n2, c2 = y.shape
    y_spec = pl.BlockSpec((None, 2, n2, ct), lambda k, j: (k, 0, 0, j))
    v_spec = pl.BlockSpec((1, ct), lambda k, j: (0, j))
    body = functools.partial(_minor_filter_body, n2=n2)
    return pl.pallas_call(
        body,
        out_shape=jax.ShapeDtypeStruct(y.shape, BF16),
        grid=(kh, c2 // ct),
        in_specs=[y_spec, pl.BlockSpec((None, 2 * n2, 2 * n2), lambda k, j: (k, 0, 0)), v_spec, v_spec],
        out_specs=y_spec,
        compiler_params=_cparams(("parallel", "parallel")),
        name="minor_filter",
    )(y, w_fwd, inv_norm, bias)


def _major_groups(a_in, sub, n2):
    return min(n2 // sub, max(1, 2048 // (a_in * sub)))


def hyena_filter_spectra(seq_len, channels, fw, f_bias):
    cst = _fft_consts(seq_len)
    n1, n2, kh = cst["n1"], cst["n2"], cst["kh"]
    kern, sums = filter_gen(seq_len, channels, *fw)
    inv_norm = 1.0 / jnp.sum(sums, axis=0, keepdims=True)
    u = kern.reshape(1, n1, n2, 2 * channels)
    y = major_fwd(u, 0, 2 * channels, cst["a_fwd"], kh, ct=512, groups=_major_groups(n1, SUB, n2), sub=SUB,
                  out_dtype=BF16)
    return minor_filter(y[0], cst["w_fwd"], inv_norm, f_bias.reshape(1, 2 * channels), ct=1024)


def _row_tile(seq_len, want):
    return min(want, seq_len)


def _layer(x, mem, lw, ws, layer, batch, seq_len, dims):
    d, dh, da, dm = dims
    n_heads = da // HEAD_DIM
    n_mem = mem.shape[0] // batch

    proj = norm_matmul_conv(x, lw["g_pre_mix"], ws["w_in"], layer, lw["conv_w"], lw["conv_b"], seq_len,
                            tm=_row_tile(seq_len, 1024), tn=512)

    cst = _fft_consts(seq_len)
    p4 = proj.reshape(batch, cst["a_out"], cst["n2"], proj.shape[1])
    spectra = lw["spectra"][seq_len]
    z = long_conv_gated(p4, 0, p4, 1, spectra, 0, seq_len, dh, BF16)
    y_h = long_conv_gated(z, 0, p4, 2, spectra, 1, seq_len, dh, BF16).reshape(batch * seq_len, dh)

    qcol = 3 * dh // HEAD_DIM
    y_a = dil_attn(proj, batch, seq_len, qcol, qcol + n_heads, qcol + 2 * n_heads, n_heads)
    kv = norm_matmul(mem, lw["g_mem"], ws["w_mem_kv"], layer, tm=n_mem, tn=512)
    y_m = mem_attn(proj, kv, batch, seq_len, n_mem, qcol + 3 * n_heads, dm // HEAD_DIM, tq=_row_tile(seq_len, 1024))

    x = mix_out(y_h, y_a, y_m, lw["g_grp"], ws["w_out"], layer, lw["g_post_mix"], x, _row_tile(seq_len, 512))
    return ffn(x, lw["g_pre_ffn"], ws["w_up"], lw["ffn_conv_w"], lw["ffn_conv_b"], ws["w_down"],
               lw["g_post_ffn"], layer, seq_len, _row_tile(seq_len, 1024), tf=512)


def kernel(x_prompt, x_sample, mem_prompt, mem_sample, g_pre_mix, w_in, conv_w, conv_b, f_w1, f_b1, f_w2, f_b2, f_w3, f_b3, f_w4, f_freq, f_bias, g_mem, w_mem_kv, g_grp, w_out, g_post_mix, g_pre_ffn, w_up, ffn_conv_w, ffn_conv_b, w_down, g_post_ffn):
    depth, d, d_in = w_in.shape
    dh = conv_w.shape[2] // 3
    dm = w_mem_kv.shape[2] // 2
    da = (d_in - 3 * dh - dm) // 3
    dims = (d, dh, da, dm)
    groups = [(x_prompt, mem_prompt), (x_sample, mem_sample)]
    seq_lens = sorted({g[0].shape[1] for g in groups})

    xs = [g[0].reshape(-1, d) for g in groups]
    mems = [g[1].reshape(-1, d) for g in groups]
    hid = FILTER_HIDDEN
    ws = dict(w_in=w_in.astype(BF16), w_mem_kv=w_mem_kv.astype(BF16), w_out=w_out.astype(BF16),
              w_up=w_up.astype(BF16), w_down=w_down.astype(BF16))
    for i in range(depth):
        row = lambda v: v[i][None, :]
        w1p = jnp.zeros((hid, hid), F32).at[:FILTER_EMB].set(f_w1[i])
        w4d = f_w4[i].reshape(hid, 2, 2, dh).transpose(2, 0, 1, 3).reshape(2, hid, 2 * dh)
        fw = (w1p, row(f_b1), f_w2[i], row(f_b2), f_w3[i], row(f_b3), row(f_freq), w4d)
        lw = dict(
            g_pre_mix=row(g_pre_mix), conv_w=conv_w[i], conv_b=row(conv_b),
            g_mem=row(g_mem), g_grp=row(g_grp), g_post_mix=row(g_post_mix), g_pre_ffn=row(g_pre_ffn),
            ffn_conv_w=ffn_conv_w[i], ffn_conv_b=row(ffn_conv_b), g_post_ffn=row(g_post_ffn),
            spectra={sl: hyena_filter_spectra(sl, dh, fw, f_bias[i]) for sl in seq_lens},
        )
        for gi, (xg, _) in enumerate(groups):
            xs[gi] = _layer(xs[gi], mems[gi], lw, ws, i, xg.shape[0], xg.shape[1], dims)
    return tuple(x.reshape(g[0].shape) for x, g in zip(xs, groups))
```

```python
import functools
import math

import numpy as np
import jax
import jax.numpy as jnp
from jax import lax
from jax.experimental import pallas as pl
from jax.experimental.pallas import tpu as pltpu

F32 = jnp.float32
BF16 = jnp.bfloat16

HEAD_DIM = 128
DILATED_PATTERNS = ((128, 1), (512, 4), (2048, 16))
ROPE_THETA = 500000.0
ROT_DIM = HEAD_DIM // 4
FILTER_EMB = 33
FILTER_HIDDEN = 64
DECAY_FAST = 0.3
DECAY_SLOW = 1.5
DECAY_TARGET = 1e-2
DECAY_SHIFT = 0.05
EPS = 1e-6
NEG = -1e30

HALO = 16
ATTN_BLOCKS_PER_STEP = 8
SUB = 8
SUB_BF16 = 16
MXU_DEPTH = 256
VMEM_LIMIT = 56 * 1024 * 1024


def _cparams(sem):
    return pltpu.CompilerParams(dimension_semantics=sem, vmem_limit_bytes=VMEM_LIMIT)


def _split(x):
    hi = x.astype(BF16)
    lo = (x - hi.astype(F32)).astype(BF16)
    return hi, lo


def _dot(a, b):
    return jnp.dot(a, b, preferred_element_type=F32)


def _rms(v, g):
    return v * lax.rsqrt(jnp.mean(v * v, axis=-1, keepdims=True) + EPS) * g


def _fill_normed(hb_ref, xp_ref, x_ref, xn_ref, g_ref, i, tm, blocks_per_seq):
    g = g_ref[...]
    pos = i % blocks_per_seq
    keep_p = (pos != 0).astype(F32)
    keep_n = (pos != blocks_per_seq - 1).astype(F32)
    hb_ref[0:HALO, :] = (_rms(xp_ref[...], g) * keep_p).astype(BF16)
    hb_ref[HALO:HALO + tm, :] = _rms(x_ref[...], g).astype(BF16)
    hb_ref[HALO + tm:, :] = (_rms(xn_ref[...], g) * keep_n).astype(BF16)


def _conv3(u, cw, cb, tm):
    rows = tm + 2 * HALO
    up = pltpu.roll(u, 1, 0)[HALO:HALO + tm]
    un = pltpu.roll(u, rows - 1, 0)[HALO:HALO + tm]
    uc = u[HALO:HALO + tm]
    return up * cw[0:1] + uc * cw[1:2] + un * cw[2:3] + cb


def _nm_conv_body(xp_ref, x_ref, xn_ref, g_ref, w_ref, cw_ref, cb_ref, o_ref, hb_ref, *, tm, blocks_per_seq,
                  n_conv):
    i = pl.program_id(0)
    j = pl.program_id(1)

    @pl.when(j == 0)
    def _():
        _fill_normed(hb_ref, xp_ref, x_ref, xn_ref, g_ref, i, tm, blocks_per_seq)

    u = _dot(hb_ref[...], w_ref[...])

    @pl.when(j < n_conv)
    def _():
        o_ref[...] = _conv3(u, cw_ref[...], cb_ref[...], tm).astype(o_ref.dtype)

    @pl.when(j >= n_conv)
    def _():
        o_ref[...] = u[HALO:HALO + tm].astype(o_ref.dtype)


def _nm_body(x_ref, g_ref, w_ref, o_ref, hb_ref):
    @pl.when(pl.program_id(1) == 0)
    def _():
        hb_ref[...] = _rms(x_ref[...], g_ref[...]).astype(BF16)

    o_ref[...] = _dot(hb_ref[...], w_ref[...]).astype(o_ref.dtype)


def _halo_specs(tm, d, n_rows, x_buffers=2):
    hb = tm // HALO
    last = n_rows // HALO - 1
    return [
        pl.BlockSpec((HALO, d), lambda i, j: (jnp.maximum(i * hb - 1, 0), 0)),
        pl.BlockSpec((tm, d), lambda i, j: (i, 0), pipeline_mode=pl.Buffered(x_buffers)),
        pl.BlockSpec((HALO, d), lambda i, j: (jnp.minimum((i + 1) * hb, last), 0)),
    ]


def norm_matmul_conv(x, g, w, layer, cw, cb, seq_len, tm, tn):
    t, d = x.shape
    n = w.shape[2]
    n_conv = cw.shape[1] // tn
    body = functools.partial(_nm_conv_body, tm=tm, blocks_per_seq=seq_len // tm, n_conv=n_conv)
    return pl.pallas_call(
        body,
        out_shape=jax.ShapeDtypeStruct((t, n), BF16),
        grid=(t // tm, n // tn),
        in_specs=_halo_specs(tm, d, t) + [
            pl.BlockSpec((1, d), lambda i, j: (0, 0)),
            pl.BlockSpec((None, d, tn), lambda i, j: (layer, 0, j)),
            pl.BlockSpec((3, tn), lambda i, j: (0, jnp.minimum(j, n_conv - 1))),
            pl.BlockSpec((1, tn), lambda i, j: (0, jnp.minimum(j, n_conv - 1))),
        ],
        out_specs=pl.BlockSpec((tm, tn), lambda i, j: (i, j)),
        scratch_shapes=[pltpu.VMEM((tm + 2 * HALO, d), BF16)],
        compiler_params=_cparams(("parallel", "arbitrary")),
        name="norm_matmul_conv",
    )(x, x, x, g, w, cw, cb)


def norm_matmul(x, g, w, layer, tm, tn):
    t, d = x.shape
    n = w.shape[2]
    return pl.pallas_call(
        _nm_body,
        out_shape=jax.ShapeDtypeStruct((t, n), BF16),
        grid=(t // tm, n // tn),
        in_specs=[
            pl.BlockSpec((tm, d), lambda i, j: (i, 0)),
            pl.BlockSpec((1, d), lambda i, j: (0, 0)),
            pl.BlockSpec((None, d, tn), lambda i, j: (layer, 0, j)),
        ],
        out_specs=pl.BlockSpec((tm, tn), lambda i, j: (i, j)),
        scratch_shapes=[pltpu.VMEM((tm, d), BF16)],
        compiler_params=_cparams(("parallel", "arbitrary")),
        name="norm_matmul",
    )(x, g, w)


def _ffn_body(xp_ref, x_ref, xn_ref, g_ref, wg_ref, wv_ref, cw_ref, cb_ref, wd_ref, gp_ref, o_ref,
              hb_ref, *, tm, blocks_per_seq, nj):
    i = pl.program_id(0)
    j = pl.program_id(1)

    @pl.when(j == 0)
    def _():
        _fill_normed(hb_ref, xp_ref, x_ref, xn_ref, g_ref, i, tm, blocks_per_seq)
        o_ref[...] = jnp.zeros_like(o_ref)

    ug = _dot(hb_ref[...], wg_ref[...])
    val = _dot(hb_ref[HALO:HALO + tm, :], wv_ref[...])
    gate = _conv3(ug, cw_ref[...], cb_ref[...], tm)
    ff = jax.nn.gelu(gate, approximate=True) * val
    o_ref[...] += _dot(ff.astype(BF16), wd_ref[...])

    @pl.when(j == nj - 1)
    def _():
        o_ref[...] = x_ref[...] + _rms(o_ref[...], gp_ref[...])


def ffn(x, g_pre, w_up, cw, cb, w_down, g_post, layer, seq_len, tm, tf):
    t, d = x.shape
    d_ff = w_down.shape[1]
    nj = d_ff // tf
    body = functools.partial(_ffn_body, tm=tm, blocks_per_seq=seq_len // tm, nj=nj)
    return pl.pallas_call(
        body,
        out_shape=jax.ShapeDtypeStruct((t, d), F32),
        grid=(t // tm, nj),
        in_specs=_halo_specs(tm, d, t, x_buffers=1) + [
            pl.BlockSpec((1, d), lambda i, j: (0, 0)),
            pl.BlockSpec((None, d, tf), lambda i, j: (layer, 0, j)),
            pl.BlockSpec((None, d, tf), lambda i, j: (layer, 0, j + nj)),
            pl.BlockSpec((3, tf), lambda i, j: (0, j)),
            pl.BlockSpec((1, tf), lambda i, j: (0, j)),
            pl.BlockSpec((None, tf, d), lambda i, j: (layer, j, 0)),
            pl.BlockSpec((1, d), lambda i, j: (0, 0)),
        ],
        out_specs=pl.BlockSpec((tm, d), lambda i, j: (i, 0)),
        scratch_shapes=[pltpu.VMEM((tm + 2 * HALO, d), BF16)],
        compiler_params=_cparams(("parallel", "arbitrary")),
        name="ffn",
    )(x, x, x, g_pre, w_up, w_up, cw, cb, w_down, g_post)


def _mix_out_body(yh_ref, ya_ref, ym_ref, gg_ref, w_ref, gp_ref, x_ref, o_ref, *, dh, da):
    gg = gg_ref[...]
    tm = x_ref.shape[0]
    for rows in (slice(0, tm // 2), slice(tm // 2, tm)):
        acc = _dot(_rms(yh_ref[rows, :].astype(F32), gg[:, :dh]).astype(BF16), w_ref[0:dh, :])
        acc += _dot(_rms(ya_ref[rows, :].astype(F32), gg[:, dh:dh + da]).astype(BF16), w_ref[dh:dh + da, :])
        acc += _dot(_rms(ym_ref[rows, :].astype(F32), gg[:, dh + da:]).astype(BF16), w_ref[dh + da:, :])
        o_ref[rows, :] = x_ref[rows, :] + _rms(acc, gp_ref[...])


def mix_out(yh, ya, ym, g_grp, w_out, layer, g_post, x, tm):
    t, d = x.shape
    dh, da, dm = yh.shape[1], ya.shape[1], ym.shape[1]
    body = functools.partial(_mix_out_body, dh=dh, da=da)
    return pl.pallas_call(
        body,
        out_shape=jax.ShapeDtypeStruct((t, d), F32),
        grid=(t // tm,),
        in_specs=[
            pl.BlockSpec((tm, dh), lambda i: (i, 0)),
            pl.BlockSpec((tm, da), lambda i: (i, 0)),
            pl.BlockSpec((tm, dm), lambda i: (i, 0)),
            pl.BlockSpec((1, d), lambda i: (0, 0)),
            pl.BlockSpec((None, d, d), lambda i: (layer, 0, 0)),
            pl.BlockSpec((1, d), lambda i: (0, 0)),
            pl.BlockSpec((tm, d), lambda i: (i, 0)),
        ],
        out_specs=pl.BlockSpec((tm, d), lambda i: (i, 0)),
        compiler_params=_cparams(("parallel",)),
        name="mix_out",
    )(yh, ya, ym, g_grp, w_out, g_post, x)


def _rope_tables(seq_len):
    half = ROT_DIM // 2
    inv_freq = np.exp(-math.log(ROPE_THETA) * np.arange(0, ROT_DIM, 2, dtype=np.float64) / ROT_DIM)
    ang = np.arange(seq_len, dtype=np.float64)[:, None] * inv_freq.astype(np.float32).astype(np.float64)[None, :]
    c, s = np.cos(ang), np.sin(ang)
    cos_t = np.ones((seq_len, HEAD_DIM), np.float64)
    sin_t = np.zeros((seq_len, HEAD_DIM), np.float64)
    cos_t[:, :half] = c
    cos_t[:, half:ROT_DIM] = c
    sin_t[:, :half] = -s
    sin_t[:, half:ROT_DIM] = s
    return cos_t.astype(np.float32), sin_t.astype(np.float32)


def _dil_attn_body(q_ref, k_ref, v_ref, cos_ref, sin_ref, o_ref, qs_ref, ks_ref, vs_ref, m_ref, l_ref, *, seq_len):
    half = ROT_DIM // 2
    scale = 1.0 / math.sqrt(HEAD_DIM)
    chunk = min(512, seq_len)

    def rope_chunk(c, carry):
        rows = pl.ds(pl.multiple_of(c * chunk, chunk), chunk)
        cs = cos_ref[rows, :]
        sn = sin_ref[rows, :]
        lane = lax.broadcasted_iota(jnp.int32, (chunk, HEAD_DIM), 1)
        for src, dst, mul in ((q_ref, qs_ref, scale), (k_ref, ks_ref, None)):
            x = src[rows, :].astype(F32)
            partner = jnp.where(lane < half, pltpu.roll(x, HEAD_DIM - half, 1), pltpu.roll(x, half, 1))
            y = x * cs + partner * sn
            dst[rows, :] = y if mul is None else y * mul
        vs_ref[rows, :] = v_ref[rows, :].astype(F32)
        return carry

    lax.fori_loop(0, seq_len // chunk, rope_chunk, 0)

    n_br = len(DILATED_PATTERNS)
    for bi, (window, dil) in enumerate(DILATED_PATTERNS):
        radius = window // (2 * dil)
        n = seq_len // dil
        tq = min(128, n)
        kw = min(tq + 2 * radius, n)
        nblk = n // tq

        def one_block(idx, bi=bi, dil=dil, radius=radius, n=n, tq=tq, kw=kw, nblk=nblk):
            r = idx // nblk
            q0 = (idx % nblk) * tq
            k0 = jnp.clip(q0 - radius, 0, n - kw)
            if dil == 1:
                qsl = pl.ds(pl.multiple_of(q0, tq), tq)
                ksl = pl.ds(pl.multiple_of(k0, 8), kw)
            else:
                qsl = pl.ds(r + q0 * dil, tq, stride=dil)
                ksl = pl.ds(r + k0 * dil, kw, stride=dil)
            qb = qs_ref[qsl, :].astype(BF16)
            kb = ks_ref[ksl, :].astype(BF16)
            vb = vs_ref[ksl, :].astype(BF16)
            s = lax.dot_general(qb, kb, (((1,), (1,)), ((), ())), preferred_element_type=F32)
            rel = (k0 + lax.broadcasted_iota(jnp.int32, (tq, kw), 1)) - (q0 + lax.broadcasted_iota(jnp.int32, (tq, kw), 0))
            s = jnp.where(jnp.abs(rel) <= radius, s, NEG)
            m = jnp.max(s, axis=-1, keepdims=True)
            p = jnp.exp(s - m)
            l = jnp.sum(p, axis=-1, keepdims=True)
            acc = _dot(p.astype(BF16), vb)
            m = jnp.broadcast_to(m, (tq, HEAD_DIM))
            l = jnp.broadcast_to(l, (tq, HEAD_DIM))
            if bi > 0:
                m_old = m_ref[qsl, :]
                m_new = jnp.maximum(m_old, m)
                a_old = jnp.exp(m_old - m_new)
                a_new = jnp.exp(m - m_new)
                acc = o_ref[qsl, :] * a_old + acc * a_new
                l = l_ref[qsl, :] * a_old + l * a_new
                m = m_new
            return qsl, m, l, acc

        total = dil * nblk
        unroll = min(ATTN_BLOCKS_PER_STEP, total)

        def step(it, carry, bi=bi, unroll=unroll, one_block=one_block):
            done = [one_block(it * unroll + u) for u in range(unroll)]
            for qsl, m, l, acc in done:
                if bi == n_br - 1:
                    o_ref[qsl, :] = acc / l
                else:
                    o_ref[qsl, :] = acc
                    m_ref[qsl, :] = m
                    l_ref[qsl, :] = l
            return carry

        lax.fori_loop(0, total // unroll, step, 0)


def dil_attn(proj, batch, seq_len, q_col, k_col, v_col, n_heads):
    cos_t, sin_t = _rope_tables(seq_len)
    nb = 1 if seq_len * HEAD_DIM * 4 > (2 << 20) else 2

    def col_spec(col):
        return pl.BlockSpec((seq_len, HEAD_DIM), lambda b, h: (b, col + h))

    tab_spec = pl.BlockSpec((seq_len, HEAD_DIM), lambda b, h: (0, 0), pipeline_mode=pl.Buffered(1))
    body = functools.partial(_dil_attn_body, seq_len=seq_len)
    return pl.pallas_call(
        body,
        out_shape=jax.ShapeDtypeStruct((batch * seq_len, n_heads * HEAD_DIM), F32),
        grid=(batch, n_heads),
        in_specs=[col_spec(q_col), col_spec(k_col), col_spec(v_col), tab_spec, tab_spec],
        out_specs=pl.BlockSpec((seq_len, HEAD_DIM), lambda b, h: (b, h), pipeline_mode=pl.Buffered(nb)),
        scratch_shapes=[pltpu.VMEM((seq_len, HEAD_DIM), F32)] * 5,
        compiler_params=_cparams(("parallel", "parallel")),
        name="dil_attn",
    )(proj, proj, proj, cos_t, sin_t)


def _mem_attn_body(q_ref, k_ref, v_ref, o_ref):
    s = lax.dot_general(q_ref[...].astype(BF16), k_ref[...].astype(BF16), (((1,), (1,)), ((), ())),
                        preferred_element_type=F32) * (1.0 / math.sqrt(HEAD_DIM))
    m = jnp.max(s, axis=-1, keepdims=True)
    p = jnp.exp(s - m)
    l = jnp.sum(p, axis=-1, keepdims=True)
    o_ref[...] = (_dot(p.astype(BF16), v_ref[...].astype(BF16)) / l).astype(o_ref.dtype)


def mem_attn(proj, kv, batch, seq_len, n_mem, q_col, n_heads, tq):
    nq = seq_len // tq
    return pl.pallas_call(
        _mem_attn_body,
        out_shape=jax.ShapeDtypeStruct((batch * seq_len, n_heads * HEAD_DIM), BF16),
        grid=(batch, nq, n_heads),
        in_specs=[
            pl.BlockSpec((tq, HEAD_DIM), lambda b, i, h: (b * nq + i, q_col + h)),
            pl.BlockSpec((n_mem, HEAD_DIM), lambda b, i, h: (b, h)),
            pl.BlockSpec((n_mem, HEAD_DIM), lambda b, i, h: (b, n_heads + h)),
        ],
        out_specs=pl.BlockSpec((tq, HEAD_DIM), lambda b, i, h: (b * nq + i, h)),
        compiler_params=_cparams(("parallel", "parallel", "parallel")),
        name="mem_attn",
    )(proj, kv, kv)


def _fft_split(seq_len):
    n = 2 * seq_len
    n2 = 128
    return n // n2, n2


@functools.lru_cache(maxsize=None)
def _fft_consts(seq_len):
    n1, n2 = _fft_split(seq_len)
    n = n1 * n2
    kh = n1 // 2 + 1
    a_out = n1 // 2
    a = np.arange(n1)
    k1 = np.arange(kh)
    th = 2.0 * np.pi * ((k1[:, None] * a[None, :]) % n1) / n1
    fwd = np.stack([np.cos(th), -np.sin(th)], axis=1).reshape(2 * kh, n1)
    alpha = np.where((k1 == 0) | (k1 == n1 // 2), 1.0, 2.0) / n
    th_i = th[:, :a_out].T
    inv = np.stack([np.cos(th_i) * alpha[None, :], -np.sin(th_i) * alpha[None, :]], axis=2)
    inv = inv.reshape(a_out, 2 * kh)
    b = np.arange(n2)
    k2 = np.arange(n2)
    kk = k1[:, None, None] + n1 * k2[None, :, None]
    ph = 2.0 * np.pi * ((kk * b[None, None, :]) % n) / n
    wr, wi = np.cos(ph), -np.sin(ph)
    vr, vi = np.transpose(wr, (0, 2, 1)), -np.transpose(wi, (0, 2, 1))

    def block(re, im):
        blk = np.concatenate([np.concatenate([re, -im], axis=2), np.concatenate([im, re], axis=2)], axis=1)
        return np.asarray(blk, np.float32).astype(BF16)

    def kron_bf16(m, sub):
        return np.asarray(np.kron(m, np.eye(sub)), np.float32).astype(BF16)

    sub_sig = SUB if a_out * SUB >= MXU_DEPTH else SUB_BF16
    return dict(n1=n1, n2=n2, kh=kh, a_out=a_out, sub_sig=sub_sig,
                a_fwd=kron_bf16(fwd, SUB),
                a_fwd_half=kron_bf16(fwd[:, :a_out], sub_sig),
                a_inv=kron_bf16(inv, sub_sig),
                w_fwd=block(wr, wi), w_inv=block(vr, vi))


def _filter_tables(seq_len, channels):
    n = 2 * seq_len
    bands = (FILTER_EMB - 1) // 2
    pos = np.concatenate([np.arange(seq_len), [0], np.arange(seq_len - 1, 0, -1)])
    t = np.linspace(0.0, 1.0, seq_len)[pos]
    w = 2.0 * np.pi * pos / seq_len
    f = np.linspace(1e-4, bands - 1, bands)
    ang = w[:, None] * f[None, :]
    z = np.zeros((n, FILTER_HIDDEN), np.float64)
    z[:, 0] = t
    z[:, 1:1 + bands] = np.cos(ang)
    z[:, 1 + bands:1 + 2 * bands] = -np.sin(ang)
    deltas = np.abs(np.linspace(math.log(DECAY_TARGET) / DECAY_SLOW, math.log(DECAY_TARGET) / DECAY_FAST, channels))
    return z.astype(np.float32), deltas[None, :].astype(np.float32)


def _filter_body(z_ref, w1_ref, b1_ref, w2_ref, b2_ref, w3_ref, b3_ref, fr_ref, w4_ref, dl_ref,
                 k_ref, s_ref, *, tr, seq_len, channels):
    i = pl.program_id(0)
    fr = fr_ref[...]
    z = z_ref[...]

    def layer(h, w_ref, b_ref):
        w_hi, w_lo = _split(w_ref[...])
        h_hi, h_lo = _split(h)
        pre = _dot(h_hi, w_hi) + _dot(h_lo, w_hi) + _dot(h_hi, w_lo)
        return jnp.sin(fr * (pre + b_ref[...]))

    h = layer(z, w1_ref, b1_ref)
    h = layer(h, w2_ref, b2_ref)
    h = layer(h, w3_ref, b3_ref)
    w_hi, w_lo = _split(w4_ref[...])
    h_hi, h_lo = _split(h)
    k = _dot(h_hi, w_hi) + _dot(h_lo, w_hi) + _dot(h_hi, w_lo)
    decay = jnp.exp(-z[:, 0:1] * dl_ref[...]) + DECAY_SHIFT
    row = i * tr + lax.broadcasted_iota(jnp.int32, (tr, 1), 0)
    decay = jnp.where(row == seq_len, 0.0, decay)
    k = k * jnp.concatenate([decay, decay], axis=1)
    k_ref[...] = k.astype(k_ref.dtype)

    @pl.when(i == 0)
    def _():
        s_ref[...] = jnp.zeros_like(s_ref)

    s_ref[...] += jnp.sum(jnp.abs(k).reshape(tr // 8, 8, 2 * channels), axis=0)


def filter_gen(seq_len, channels, w1p, b1, w2, b2, w3, b3, freq, w4d):
    n = 2 * seq_len
    tr = min(512, seq_len)
    z, deltas = _filter_tables(seq_len, channels)
    hid = FILTER_HIDDEN
    full = lambda shape: pl.BlockSpec(shape, lambda i: (0,) * len(shape))
    body = functools.partial(_filter_body, tr=tr, seq_len=seq_len, channels=channels)
    return pl.pallas_call(
        body,
        out_shape=(jax.ShapeDtypeStruct((n, 2 * channels), BF16), jax.ShapeDtypeStruct((8, 2 * channels), F32)),
        grid=(n // tr,),
        in_specs=[
            pl.BlockSpec((tr, hid), lambda i: (i, 0)),
            full((hid, hid)), full((1, hid)), full((hid, hid)), full((1, hid)), full((hid, hid)), full((1, hid)),
            full((1, hid)),
            pl.BlockSpec((None, hid, 2 * channels), lambda i: ((i * tr) // seq_len, 0, 0)),
            full((1, channels)),
        ],
        out_specs=(pl.BlockSpec((tr, 2 * channels), lambda i: (i, 0)), full((8, 2 * channels))),
        compiler_params=_cparams(("arbitrary",)),
        name="filter_gen",
    )(z, w1p, b1, w2, b2, w3, b3, freq, w4d, deltas)


def _major_fwd_body(u_ref, a_ref, y_ref, *, groups, sub):
    a_in, _, ct = u_ref.shape
    kh = y_ref.shape[0]
    ub = u_ref[...].astype(F32)
    ys = []
    for q in range(groups):
        u = ub[:, q * sub:(q + 1) * sub, :].reshape(a_in * sub, ct)
        ys.append(_dot(a_ref[...], u.astype(BF16)).reshape(kh, 2, sub, ct))
    y_ref[...] = jnp.concatenate(ys, axis=2).astype(y_ref.dtype)


def major_fwd(u, col, channels, a_fwd, kh, ct, groups, sub, out_dtype):
    bsz, a_in, n2, _ = u.shape
    nc = channels // ct
    rows = groups * sub
    body = functools.partial(_major_fwd_body, groups=groups, sub=sub)
    return pl.pallas_call(
        body,
        out_shape=jax.ShapeDtypeStruct((bsz, kh, 2, n2, channels), out_dtype),
        grid=(bsz, n2 // rows, nc),
        in_specs=[
            pl.BlockSpec((None, a_in, rows, ct), lambda b, r, j: (b, 0, r, col * nc + j)),
            pl.BlockSpec(a_fwd.shape, lambda b, r, j: (0, 0)),
        ],
        out_specs=pl.BlockSpec((None, kh, 2, rows, ct), lambda b, r, j: (b, 0, 0, r, j)),
        compiler_params=_cparams(("parallel", "parallel", "parallel")),
        name="major_fwd",
    )(u, a_fwd)


def _long_conv_body(u_ref, af_ref, ai_ref, wf_ref, wi_ref, k_ref, m_ref, o_ref, yg_ref, *,
                    nb, nk, kg, groups, sub, n2):
    s = pl.program_id(2)
    bb, a_len, rows, ct = u_ref.shape
    kh = yg_ref.shape[1]
    sdt = yg_ref.dtype
    wide = F32 if sub == SUB else u_ref.dtype

    @pl.when(s < nb)
    def _():
        for i in range(bb):
            ub = u_ref[i].astype(wide)
            for q in range(groups):
                u = ub[:, q * sub:(q + 1) * sub, :].reshape(a_len * sub, ct)
                y = _dot(af_ref[...], u.astype(BF16)).reshape(kh, 2, sub, ct)
                dst = pl.ds(pl.multiple_of(s * rows + q * sub, sub), sub)
                yg_ref[i, :, :, dst, :] = y.astype(sdt)

    @pl.when(jnp.logical_and(s >= nb, s < nb + nk))
    def _():
        for kk in range(kg):
            k1 = (s - nb) * kg + kk
            kr = k_ref[kk, 0].astype(F32)
            ki = k_ref[kk, 1].astype(F32)
            for i in range(bb):
                yv = jnp.concatenate([yg_ref[i, k1, 0].astype(BF16), yg_ref[i, k1, 1].astype(BF16)], axis=0)
                z = _dot(wf_ref[kk], yv)
                zr, zi = z[:n2], z[n2:]
                pr = (zr * kr - zi * ki).astype(BF16)
                pi = (zr * ki + zi * kr).astype(BF16)
                g = _dot(wi_ref[kk], jnp.concatenate([pr, pi], axis=0))
                yg_ref[i, k1, 0] = g[:n2].astype(sdt)
                yg_ref[i, k1, 1] = g[n2:].astype(sdt)

    @pl.when(s >= nb + nk)
    def _():
        r = s - nb - nk
        for i in range(bb):
            ys = []
            for q in range(groups):
                src = pl.ds(pl.multiple_of(r * rows + q * sub, sub), sub)
                g = yg_ref[i, :, :, src, :].reshape(kh * 2 * sub, ct)
                ys.append(_dot(ai_ref[...], g.astype(BF16)).reshape(a_len, sub, ct))
            y = jnp.concatenate(ys, axis=1)
            o_ref[i] = (y * m_ref[i].astype(F32)).astype(o_ref.dtype)


def _k1_group(kh):
    return max(g for g in range(1, 18) if kh % g == 0)


def long_conv_gated(u, u_col, mult, mult_col, spectra, order, seq_len, channels, out_dtype):
    cst = _fft_consts(seq_len)
    kh, n2 = cst["kh"], cst["n2"]
    bsz, a_len = u.shape[0], u.shape[1]
    sub = cst["sub_sig"]
    ct = 256
    nc = channels // ct
    groups = _major_groups(a_len, sub, n2)
    rows = groups * sub
    nb = n2 // rows
    kg = _k1_group(kh)
    nk = kh // kg
    bb = max(g for g in (1, 2) if bsz % g == 0)
    a_fwd, a_inv = cst["a_fwd_half"], cst["a_inv"]

    def row_blk(s, first):
        return jnp.clip(s - first, 0, nb - 1)

    def k_blk(s):
        return jnp.clip(s - nb, 0, nk - 1)

    const = lambda shape: pl.BlockSpec(shape, lambda b, j, s: (0,) * len(shape), pipeline_mode=pl.Buffered(1))
    body = functools.partial(_long_conv_body, nb=nb, nk=nk, kg=kg, groups=groups, sub=sub, n2=n2)
    return pl.pallas_call(
        body,
        out_shape=jax.ShapeDtypeStruct((bsz, a_len, n2, channels), out_dtype),
        grid=(bsz // bb, nc, nb + nk + nb),
        in_specs=[
            pl.BlockSpec((bb, a_len, rows, ct), lambda b, j, s: (b, 0, row_blk(s, 0), u_col * nc + j)),
            const(a_fwd.shape),
            const(a_inv.shape),
            pl.BlockSpec((kg, 2 * n2, 2 * n2), lambda b, j, s: (k_blk(s), 0, 0)),
            pl.BlockSpec((kg, 2 * n2, 2 * n2), lambda b, j, s: (k_blk(s), 0, 0)),
            pl.BlockSpec((kg, 2, n2, ct), lambda b, j, s: (k_blk(s), 0, 0, order * nc + j)),
            pl.BlockSpec((bb, a_len, rows, ct), lambda b, j, s: (b, 0, row_blk(s, nb + nk), mult_col * nc + j)),
        ],
        out_specs=pl.BlockSpec((bb, a_len, rows, ct), lambda b, j, s: (b, 0, row_blk(s, nb + nk), j)),
        scratch_shapes=[pltpu.VMEM((bb, kh, 2, n2, ct), F32 if sub == SUB else BF16)],
        compiler_params=_cparams(("parallel", "parallel", "arbitrary")),
        name="long_conv",
    )(u, a_fwd, a_inv, cst["w_fwd"], cst["w_inv"], spectra, mult)


def _minor_filter_body(y_ref, wf_ref, inv_ref, bias_ref, k_ref, *, n2):
    z = _dot(wf_ref[...], jnp.concatenate([y_ref[0].astype(BF16), y_ref[1].astype(BF16)], axis=0))
    inv = inv_ref[...]
    k_ref[0] = (z[:n2] * inv + bias_ref[...]).astype(k_ref.dtype)
    k_ref[1] = (z[n2:] * inv).astype(k_ref.dtype)


def minor_filter(y, w_fwd, inv_norm, bias, ct):
    kh, _, n2, c2 = y.shape
    y_spec = pl.BlockSpec((None, 2, n2, ct), lambda k, j: (k, 0, 0, j))
    v_spec = pl.BlockSpec((1, ct), lambda k, j: (0, j))
    body = functools.partial(_minor_filter_body, n2=n2)
    return pl.pallas_call(
        body,
        out_shape=jax.ShapeDtypeStruct(y.shape, BF16),
        grid=(kh, c2 // ct),
        in_specs=[y_spec, pl.BlockSpec((None, 2 * n2, 2 * n2), lambda k, j: (k, 0, 0)), v_spec, v_spec],
        out_specs=y_spec,
        compiler_params=_cparams(("parallel", "parallel")),
        name="minor_filter",
    )(y, w_fwd, inv_norm, bias)


def _major_groups(a_in, sub, n2):
    return min(n2 // sub, max(1, 2048 // (a_in * sub)))


def hyena_filter_spectra(seq_len, channels, fw, f_bias):
    cst = _fft_consts(seq_len)
    n1, n2, kh = cst["n1"], cst["n2"], cst["kh"]
    kern, sums = filter_gen(seq_len, channels, *fw)
    inv_norm = 1.0 / jnp.sum(sums, axis=0, keepdims=True)
    u = kern.reshape(1, n1, n2, 2 * channels)
    y = major_fwd(u, 0, 2 * channels, cst["a_fwd"], kh, ct=512, groups=_major_groups(n1, SUB, n2), sub=SUB,
                  out_dtype=BF16)
    return minor_filter(y[0], cst["w_fwd"], inv_norm, f_bias.reshape(1, 2 * channels), ct=1024)


def _row_tile(seq_len, want):
    return min(want, seq_len)


def _layer(x, mem, lw, ws, layer, batch, seq_len, dims):
    d, dh, da, dm = dims
    n_heads = da // HEAD_DIM
    n_mem = mem.shape[0] // batch

    proj = norm_matmul_conv(x, lw["g_pre_mix"], ws["w_in"], layer, lw["conv_w"], lw["conv_b"], seq_len,
                            tm=_row_tile(seq_len, 1024), tn=512)

    cst = _fft_consts(seq_len)
    p4 = proj.reshape(batch, cst["a_out"], cst["n2"], proj.shape[1])
    spectra = lw["spectra"][seq_len]
    z = long_conv_gated(p4, 0, p4, 1, spectra, 0, seq_len, dh, BF16)
    y_h = long_conv_gated(z, 0, p4, 2, spectra, 1, seq_len, dh, BF16).reshape(batch * seq_len, dh)

    qcol = 3 * dh // HEAD_DIM
    y_a = dil_attn(proj, batch, seq_len, qcol, qcol + n_heads, qcol + 2 * n_heads, n_heads)
    kv = norm_matmul(mem, lw["g_mem"], ws["w_mem_kv"], layer, tm=n_mem, tn=512)
    y_m = mem_attn(proj, kv, batch, seq_len, n_mem, qcol + 3 * n_heads, dm // HEAD_DIM, tq=_row_tile(seq_len, 1024))

    x = mix_out(y_h, y_a, y_m, lw["g_grp"], ws["w_out"], layer, lw["g_post_mix"], x, _row_tile(seq_len, 512))
    return ffn(x, lw["g_pre_ffn"], ws["w_up"], lw["ffn_conv_w"], lw["ffn_conv_b"], ws["w_down"],
               lw["g_post_ffn"], layer, seq_len, _row_tile(seq_len, 1024), tf=512)


def kernel(x_prompt, x_sample, mem_prompt, mem_sample, g_pre_mix, w_in, conv_w, conv_b, f_w1, f_b1, f_w2, f_b2, f_w3, f_b3, f_w4, f_freq, f_bias, g_mem, w_mem_kv, g_grp, w_out, g_post_mix, g_pre_ffn, w_up, ffn_conv_w, ffn_conv_b, w_down, g_post_ffn):
    depth, d, d_in = w_in.shape
    dh = conv_w.shape[2] // 3
    dm = w_mem_kv.shape[2] // 2
    da = (d_in - 3 * dh - dm) // 3
    dims = (d, dh, da, dm)
    groups = [(x_prompt, mem_prompt), (x_sample, mem_sample)]
    seq_lens = sorted({g[0].shape[1] for g in groups})

    xs = [g[0].reshape(-1, d) for g in groups]
    mems = [g[1].reshape(-1, d) for g in groups]
    hid = FILTER_HIDDEN
    ws = dict(w_in=w_in.astype(BF16), w_mem_kv=w_mem_kv.astype(BF16), w_out=w_out.astype(BF16),
              w_up=w_up.astype(BF16), w_down=w_down.astype(BF16))
    for i in range(depth):
        row = lambda v: v[i][None, :]
        w1p = jnp.zeros((hid, hid), F32).at[:FILTER_EMB].set(f_w1[i])
        w4d = f_w4[i].reshape(hid, 2, 2, dh).transpose(2, 0, 1, 3).reshape(2, hid, 2 * dh)
        fw = (w1p, row(f_b1), f_w2[i], row(f_b2), f_w3[i], row(f_b3), row(f_freq), w4d)
        lw = dict(
            g_pre_mix=row(g_pre_mix), conv_w=conv_w[i], conv_b=row(conv_b),
            g_mem=row(g_mem), g_grp=row(g_grp), g_post_mix=row(g_post_mix), g_pre_ffn=row(g_pre_ffn),
            ffn_conv_w=ffn_conv_w[i], ffn_conv_b=row(ffn_conv_b), g_post_ffn=row(g_post_ffn),
            spectra={sl: hyena_filter_spectra(sl, dh, fw, f_bias[i]) for sl in seq_lens},
        )
        for gi, (xg, _) in enumerate(groups):
            xs[gi] = _layer(xs[gi], mems[gi], lw, ws, i, xg.shape[0], xg.shape[1], dims)
    return tuple(x.reshape(g[0].shape) for x, g in zip(xs, groups))
```

```python
import functools
import math

import numpy as np
import jax
import jax.numpy as jnp
from jax import lax
from jax.experimental import pallas as pl
from jax.experimental.pallas import tpu as pltpu

F32 = jnp.float32
BF16 = jnp.bfloat16

HEAD_DIM = 128
DILATED_PATTERNS = ((128, 1), (512, 4), (2048, 16))
ROPE_THETA = 500000.0
ROT_DIM = HEAD_DIM // 4
FILTER_EMB = 33
FILTER_HIDDEN = 64
DECAY_FAST = 0.3
DECAY_SLOW = 1.5
DECAY_TARGET = 1e-2
DECAY_SHIFT = 0.05
EPS = 1e-6
NEG = -1e30

HALO = 16
ATTN_BLOCKS_PER_STEP = 8
SUB = 8
SUB_BF16 = 16
MXU_DEPTH = 256
VMEM_LIMIT = 60 * 1024 * 1024


def _cparams(sem):
    return pltpu.CompilerParams(dimension_semantics=sem, vmem_limit_bytes=VMEM_LIMIT)


def _split(x):
    hi = x.astype(BF16)
    lo = (x - hi.astype(F32)).astype(BF16)
    return hi, lo


def _dot(a, b):
    return jnp.dot(a, b, preferred_element_type=F32)


def _rms(v, g):
    return v * lax.rsqrt(jnp.mean(v * v, axis=-1, keepdims=True) + EPS) * g


def _fill_normed(hb_ref, xp_ref, x_ref, xn_ref, g_ref, i, tm, blocks_per_seq):
    g = g_ref[...]
    pos = i % blocks_per_seq
    keep_p = (pos != 0).astype(F32)
    keep_n = (pos != blocks_per_seq - 1).astype(F32)
    hb_ref[0:HALO, :] = (_rms(xp_ref[...], g) * keep_p).astype(BF16)
    hb_ref[HALO:HALO + tm, :] = _rms(x_ref[...], g).astype(BF16)
    hb_ref[HALO + tm:, :] = (_rms(xn_ref[...], g) * keep_n).astype(BF16)


def _conv3(u, cw, cb, tm):
    rows = tm + 2 * HALO
    up = pltpu.roll(u, 1, 0)[HALO:HALO + tm]
    un = pltpu.roll(u, rows - 1, 0)[HALO:HALO + tm]
    uc = u[HALO:HALO + tm]
    return up * cw[0:1] + uc * cw[1:2] + un * cw[2:3] + cb


def _nm_conv_body(xp_ref, x_ref, xn_ref, g_ref, w_ref, cw_ref, cb_ref, o_ref, hb_ref, *, tm, blocks_per_seq,
                  n_conv):
    i = pl.program_id(0)
    j = pl.program_id(1)

    @pl.when(j == 0)
    def _():
        _fill_normed(hb_ref, xp_ref, x_ref, xn_ref, g_ref, i, tm, blocks_per_seq)

    u = _dot(hb_ref[...], w_ref[...])

    @pl.when(j < n_conv)
    def _():
        o_ref[...] = _conv3(u, cw_ref[...], cb_ref[...], tm).astype(o_ref.dtype)

    @pl.when(j >= n_conv)
    def _():
        o_ref[...] = u[HALO:HALO + tm].astype(o_ref.dtype)


def _nm_body(x_ref, g_ref, w_ref, o_ref, hb_ref):
    @pl.when(pl.program_id(1) == 0)
    def _():
        hb_ref[...] = _rms(x_ref[...], g_ref[...]).astype(BF16)

    o_ref[...] = _dot(hb_ref[...], w_ref[...]).astype(o_ref.dtype)


def _halo_specs(tm, d, n_rows, x_buffers=2):
    hb = tm // HALO
    last = n_rows // HALO - 1
    return [
        pl.BlockSpec((HALO, d), lambda i, j: (jnp.maximum(i * hb - 1, 0), 0)),
        pl.BlockSpec((tm, d), lambda i, j: (i, 0), pipeline_mode=pl.Buffered(x_buffers)),
        pl.BlockSpec((HALO, d), lambda i, j: (jnp.minimum((i + 1) * hb, last), 0)),
    ]


def norm_matmul_conv(x, g, w, layer, cw, cb, seq_len, tm, tn):
    t, d = x.shape
    n = w.shape[2]
    n_conv = cw.shape[1] // tn
    body = functools.partial(_nm_conv_body, tm=tm, blocks_per_seq=seq_len // tm, n_conv=n_conv)
    return pl.pallas_call(
        body,
        out_shape=jax.ShapeDtypeStruct((t, n), BF16),
        grid=(t // tm, n // tn),
        in_specs=_halo_specs(tm, d, t) + [
            pl.BlockSpec((1, d), lambda i, j: (0, 0)),
            pl.BlockSpec((None, d, tn), lambda i, j: (layer, 0, j)),
            pl.BlockSpec((3, tn), lambda i, j: (0, jnp.minimum(j, n_conv - 1))),
            pl.BlockSpec((1, tn), lambda i, j: (0, jnp.minimum(j, n_conv - 1))),
        ],
        out_specs=pl.BlockSpec((tm, tn), lambda i, j: (i, j)),
        scratch_shapes=[pltpu.VMEM((tm + 2 * HALO, d), BF16)],
        compiler_params=_cparams(("parallel", "arbitrary")),
        name="norm_matmul_conv",
    )(x, x, x, g, w, cw, cb)


def norm_matmul(x, g, w, layer, tm, tn):
    t, d = x.shape
    n = w.shape[2]
    return pl.pallas_call(
        _nm_body,
        out_shape=jax.ShapeDtypeStruct((t, n), BF16),
        grid=(t // tm, n // tn),
        in_specs=[
            pl.BlockSpec((tm, d), lambda i, j: (i, 0)),
            pl.BlockSpec((1, d), lambda i, j: (0, 0)),
            pl.BlockSpec((None, d, tn), lambda i, j: (layer, 0, j)),
        ],
        out_specs=pl.BlockSpec((tm, tn), lambda i, j: (i, j)),
        scratch_shapes=[pltpu.VMEM((tm, d), BF16)],
        compiler_params=_cparams(("parallel", "arbitrary")),
        name="norm_matmul",
    )(x, g, w)


def _ffn_body(xp_ref, x_ref, xn_ref, g_ref, wg_ref, wv_ref, cw_ref, cb_ref, wd_ref, gp_ref, o_ref,
              hb_ref, *, tm, blocks_per_seq, nj):
    i = pl.program_id(0)
    j = pl.program_id(1)

    @pl.when(j == 0)
    def _():
        _fill_normed(hb_ref, xp_ref, x_ref, xn_ref, g_ref, i, tm, blocks_per_seq)
        o_ref[...] = jnp.zeros_like(o_ref)

    ug = _dot(hb_ref[...], wg_ref[...])
    val = _dot(hb_ref[HALO:HALO + tm, :], wv_ref[...])
    gate = _conv3(ug, cw_ref[...], cb_ref[...], tm)
    ff = jax.nn.gelu(gate, approximate=True) * val
    o_ref[...] += _dot(ff.astype(BF16), wd_ref[...])

    @pl.when(j == nj - 1)
    def _():
        o_ref[...] = x_ref[...] + _rms(o_ref[...], gp_ref[...])


def ffn(x, g_pre, w_up, cw, cb, w_down, g_post, layer, seq_len, tm, tf):
    t, d = x.shape
    d_ff = w_down.shape[1]
    nj = d_ff // tf
    body = functools.partial(_ffn_body, tm=tm, blocks_per_seq=seq_len // tm, nj=nj)
    return pl.pallas_call(
        body,
        out_shape=jax.ShapeDtypeStruct((t, d), F32),
        grid=(t // tm, nj),
        in_specs=_halo_specs(tm, d, t) + [
            pl.BlockSpec((1, d), lambda i, j: (0, 0)),
            pl.BlockSpec((None, d, tf), lambda i, j: (layer, 0, j)),
            pl.BlockSpec((None, d, tf), lambda i, j: (layer, 0, j + nj)),
            pl.BlockSpec((3, tf), lambda i, j: (0, j)),
            pl.BlockSpec((1, tf), lambda i, j: (0, j)),
            pl.BlockSpec((None, tf, d), lambda i, j: (layer, j, 0)),
            pl.BlockSpec((1, d), lambda i, j: (0, 0)),
        ],
        out_specs=pl.BlockSpec((tm, d), lambda i, j: (i, 0)),
        scratch_shapes=[pltpu.VMEM((tm + 2 * HALO, d), BF16)],
        compiler_params=_cparams(("parallel", "arbitrary")),
        name="ffn",
    )(x, x, x, g_pre, w_up, w_up, cw, cb, w_down, g_post)


def _mix_out_body(yh_ref, ya_ref, ym_ref, gg_ref, w_ref, gp_ref, x_ref, o_ref, *, dh, da):
    gg = gg_ref[...]
    tm = x_ref.shape[0]
    for rows in (slice(0, tm // 2), slice(tm // 2, tm)):
        acc = _dot(_rms(yh_ref[rows, :].astype(F32), gg[:, :dh]).astype(BF16), w_ref[0:dh, :])
        acc += _dot(_rms(ya_ref[rows, :].astype(F32), gg[:, dh:dh + da]).astype(BF16), w_ref[dh:dh + da, :])
        acc += _dot(_rms(ym_ref[rows, :].astype(F32), gg[:, dh + da:]).astype(BF16), w_ref[dh + da:, :])
        o_ref[rows, :] = x_ref[rows, :] + _rms(acc, gp_ref[...])


def mix_out(yh, ya, ym, g_grp, w_out, layer, g_post, x, tm):
    t, d = x.shape
    dh, da, dm = yh.shape[1], ya.shape[1], ym.shape[1]
    body = functools.partial(_mix_out_body, dh=dh, da=da)
    return pl.pallas_call(
        body,
        out_shape=jax.ShapeDtypeStruct((t, d), F32),
        grid=(t // tm,),
        in_specs=[
            pl.BlockSpec((tm, dh), lambda i: (i, 0)),
            pl.BlockSpec((tm, da), lambda i: (i, 0)),
            pl.BlockSpec((tm, dm), lambda i: (i, 0)),
            pl.BlockSpec((1, d), lambda i: (0, 0)),
            pl.BlockSpec((None, d, d), lambda i: (layer, 0, 0)),
            pl.BlockSpec((1, d), lambda i: (0, 0)),
            pl.BlockSpec((tm, d), lambda i: (i, 0)),
        ],
        out_specs=pl.BlockSpec((tm, d), lambda i: (i, 0)),
        compiler_params=_cparams(("parallel",)),
        name="mix_out",
    )(yh, ya, ym, g_grp, w_out, g_post, x)


def _rope_tables(seq_len):
    half = ROT_DIM // 2
    inv_freq = np.exp(-math.log(ROPE_THETA) * np.arange(0, ROT_DIM, 2, dtype=np.float64) / ROT_DIM)
    ang = np.arange(seq_len, dtype=np.float64)[:, None] * inv_freq.astype(np.float32).astype(np.float64)[None, :]
    c, s = np.cos(ang), np.sin(ang)
    cos_t = np.ones((seq_len, HEAD_DIM), np.float64)
    sin_t = np.zeros((seq_len, HEAD_DIM), np.float64)
    cos_t[:, :half] = c
    cos_t[:, half:ROT_DIM] = c
    sin_t[:, :half] = -s
    sin_t[:, half:ROT_DIM] = s
    return cos_t.astype(np.float32), sin_t.astype(np.float32)


def _dil_attn_body(q_ref, k_ref, v_ref, cos_ref, sin_ref, o_ref, qs_ref, ks_ref, vs_ref, m_ref, l_ref, *, seq_len):
    half = ROT_DIM // 2
    scale = 1.0 / math.sqrt(HEAD_DIM)
    chunk = min(512, seq_len)

    def rope_chunk(c, carry):
        rows = pl.ds(pl.multiple_of(c * chunk, chunk), chunk)
        cs = cos_ref[rows, :]
        sn = sin_ref[rows, :]
        lane = lax.broadcasted_iota(jnp.int32, (chunk, HEAD_DIM), 1)
        for src, dst, mul in ((q_ref, qs_ref, scale), (k_ref, ks_ref, None)):
            x = src[rows, :].astype(F32)
            partner = jnp.where(lane < half, pltpu.roll(x, HEAD_DIM - half, 1), pltpu.roll(x, half, 1))
            y = x * cs + partner * sn
            dst[rows, :] = y if mul is None else y * mul
        vs_ref[rows, :] = v_ref[rows, :].astype(F32)
        return carry

    lax.fori_loop(0, seq_len // chunk, rope_chunk, 0)

    n_br = len(DILATED_PATTERNS)
    for bi, (window, dil) in enumerate(DILATED_PATTERNS):
        radius = window // (2 * dil)
        n = seq_len // dil
        tq = min(128, n)
        kw = min(tq + 2 * radius, n)
        nblk = n // tq

        def one_block(idx, bi=bi, dil=dil, radius=radius, n=n, tq=tq, kw=kw, nblk=nblk):
            r = idx // nblk
            q0 = (idx % nblk) * tq
            k0 = jnp.clip(q0 - radius, 0, n - kw)
            if dil == 1:
                qsl = pl.ds(pl.multiple_of(q0, tq), tq)
                ksl = pl.ds(pl.multiple_of(k0, 8), kw)
            else:
                qsl = pl.ds(r + q0 * dil, tq, stride=dil)
                ksl = pl.ds(r + k0 * dil, kw, stride=dil)
            qb = qs_ref[qsl, :].astype(BF16)
            kb = ks_ref[ksl, :].astype(BF16)
            vb = vs_ref[ksl, :].astype(BF16)
            s = lax.dot_general(qb, kb, (((1,), (1,)), ((), ())), preferred_element_type=F32)
            rel = (k0 + lax.broadcasted_iota(jnp.int32, (tq, kw), 1)) - (q0 + lax.broadcasted_iota(jnp.int32, (tq, kw), 0))
            s = jnp.where(jnp.abs(rel) <= radius, s, NEG)
            m = jnp.max(s, axis=-1, keepdims=True)
            p = jnp.exp(s - m)
            l = jnp.sum(p, axis=-1, keepdims=True)
            acc = _dot(p.astype(BF16), vb)
            m = jnp.broadcast_to(m, (tq, HEAD_DIM))
            l = jnp.broadcast_to(l, (tq, HEAD_DIM))
            if bi > 0:
                m_old = m_ref[qsl, :]
                m_new = jnp.maximum(m_old, m)
                a_old = jnp.exp(m_old - m_new)
                a_new = jnp.exp(m - m_new)
                acc = o_ref[qsl, :] * a_old + acc * a_new
                l = l_ref[qsl, :] * a_old + l * a_new
                m = m_new
            return qsl, m, l, acc

        total = dil * nblk
        unroll = min(ATTN_BLOCKS_PER_STEP, total)

        def step(it, carry, bi=bi, unroll=unroll, one_block=one_block):
            done = [one_block(it * unroll + u) for u in range(unroll)]
            for qsl, m, l, acc in done:
                if bi == n_br - 1:
                    o_ref[qsl, :] = acc / l
                else:
                    o_ref[qsl, :] = acc
                    m_ref[qsl, :] = m
                    l_ref[qsl, :] = l
            return carry

        lax.fori_loop(0, total // unroll, step, 0)


def dil_attn(proj, batch, seq_len, q_col, k_col, v_col, n_heads):
    cos_t, sin_t = _rope_tables(seq_len)
    nb = 1 if seq_len * HEAD_DIM * 4 > (2 << 20) else 2

    def col_spec(col):
        return pl.BlockSpec((seq_len, HEAD_DIM), lambda b, h: (b, col + h))

    tab_spec = pl.BlockSpec((seq_len, HEAD_DIM), lambda b, h: (0, 0), pipeline_mode=pl.Buffered(1))
    body = functools.partial(_dil_attn_body, seq_len=seq_len)
    return pl.pallas_call(
        body,
        out_shape=jax.ShapeDtypeStruct((batch * seq_len, n_heads * HEAD_DIM), F32),
        grid=(batch, n_heads),
        in_specs=[col_spec(q_col), col_spec(k_col), col_spec(v_col), tab_spec, tab_spec],
        out_specs=pl.BlockSpec((seq_len, HEAD_DIM), lambda b, h: (b, h), pipeline_mode=pl.Buffered(nb)),
        scratch_shapes=[pltpu.VMEM((seq_len, HEAD_DIM), F32)] * 5,
        compiler_params=_cparams(("parallel", "parallel")),
        name="dil_attn",
    )(proj, proj, proj, cos_t, sin_t)


def _mem_attn_body(q_ref, k_ref, v_ref, o_ref):
    s = lax.dot_general(q_ref[...].astype(BF16), k_ref[...].astype(BF16), (((1,), (1,)), ((), ())),
                        preferred_element_type=F32) * (1.0 / math.sqrt(HEAD_DIM))
    m = jnp.max(s, axis=-1, keepdims=True)
    p = jnp.exp(s - m)
    l = jnp.sum(p, axis=-1, keepdims=True)
    o_ref[...] = (_dot(p.astype(BF16), v_ref[...].astype(BF16)) / l).astype(o_ref.dtype)


def mem_attn(proj, kv, batch, seq_len, n_mem, q_col, n_heads, tq):
    nq = seq_len // tq
    return pl.pallas_call(
        _mem_attn_body,
        out_shape=jax.ShapeDtypeStruct((batch * seq_len, n_heads * HEAD_DIM), BF16),
        grid=(batch, nq, n_heads),
        in_specs=[
            pl.BlockSpec((tq, HEAD_DIM), lambda b, i, h: (b * nq + i, q_col + h)),
            pl.BlockSpec((n_mem, HEAD_DIM), lambda b, i, h: (b, h)),
            pl.BlockSpec((n_mem, HEAD_DIM), lambda b, i, h: (b, n_heads + h)),
        ],
        out_specs=pl.BlockSpec((tq, HEAD_DIM), lambda b, i, h: (b * nq + i, h)),
        compiler_params=_cparams(("parallel", "parallel", "parallel")),
        name="mem_attn",
    )(proj, kv, kv)


def _fft_split(seq_len):
    n = 2 * seq_len
    n2 = 128
    return n // n2, n2


@functools.lru_cache(maxsize=None)
def _fft_consts(seq_len):
    n1, n2 = _fft_split(seq_len)
    n = n1 * n2
    kh = n1 // 2 + 1
    a_out = n1 // 2
    a = np.arange(n1)
    k1 = np.arange(kh)
    th = 2.0 * np.pi * ((k1[:, None] * a[None, :]) % n1) / n1
    fwd = np.stack([np.cos(th), -np.sin(th)], axis=1).reshape(2 * kh, n1)
    alpha = np.where((k1 == 0) | (k1 == n1 // 2), 1.0, 2.0) / n
    th_i = th[:, :a_out].T
    inv = np.stack([np.cos(th_i) * alpha[None, :], -np.sin(th_i) * alpha[None, :]], axis=2)
    inv = inv.reshape(a_out, 2 * kh)
    b = np.arange(n2)
    k2 = np.arange(n2)
    kk = k1[:, None, None] + n1 * k2[None, :, None]
    ph = 2.0 * np.pi * ((kk * b[None, None, :]) % n) / n
    wr, wi = np.cos(ph), -np.sin(ph)
    vr, vi = np.transpose(wr, (0, 2, 1)), -np.transpose(wi, (0, 2, 1))

    def block(re, im):
        blk = np.concatenate([np.concatenate([re, -im], axis=2), np.concatenate([im, re], axis=2)], axis=1)
        return np.asarray(blk, np.float32).astype(BF16)

    def kron_bf16(m, sub):
        return np.asarray(np.kron(m, np.eye(sub)), np.float32).astype(BF16)

    sub_sig = SUB if a_out * SUB >= MXU_DEPTH else SUB_BF16
    return dict(n1=n1, n2=n2, kh=kh, a_out=a_out, sub_sig=sub_sig,
                a_fwd=kron_bf16(fwd, SUB),
                a_fwd_half=kron_bf16(fwd[:, :a_out], sub_sig),
                a_inv=kron_bf16(inv, sub_sig),
                w_fwd=block(wr, wi), w_inv=block(vr, vi))


def _filter_tables(seq_len, channels):
    n = 2 * seq_len
    bands = (FILTER_EMB - 1) // 2
    pos = np.concatenate([np.arange(seq_len), [0], np.arange(seq_len - 1, 0, -1)])
    t = np.linspace(0.0, 1.0, seq_len)[pos]
    w = 2.0 * np.pi * pos / seq_len
    f = np.linspace(1e-4, bands - 1, bands)
    ang = w[:, None] * f[None, :]
    z = np.zeros((n, FILTER_HIDDEN), np.float64)
    z[:, 0] = t
    z[:, 1:1 + bands] = np.cos(ang)
    z[:, 1 + bands:1 + 2 * bands] = -np.sin(ang)
    deltas = np.abs(np.linspace(math.log(DECAY_TARGET) / DECAY_SLOW, math.log(DECAY_TARGET) / DECAY_FAST, channels))
    return z.astype(np.float32), deltas[None, :].astype(np.float32)


def _filter_body(zt_ref, t_ref, w1_ref, b1_ref, w2_ref, b2_ref, w3_ref, b3_ref, fr_ref, w4_ref, dl_ref,
                 k_ref, s_ref, *, tr, seq_len, channels):
    i = pl.program_id(0)
    fr = fr_ref[...]

    def layer(ht, wt_ref, b_ref):
        w_hi, w_lo = _split(wt_ref[...])
        h_hi, h_lo = _split(ht)
        pre = _dot(w_hi, h_hi) + _dot(w_hi, h_lo) + _dot(w_lo, h_hi)
        return jnp.sin(fr * (pre + b_ref[...]))

    ht = layer(zt_ref[...], w1_ref, b1_ref)
    ht = layer(ht, w2_ref, b2_ref)
    ht = layer(ht, w3_ref, b3_ref)
    h = ht.T
    w_hi, w_lo = _split(w4_ref[...])
    h_hi, h_lo = _split(h)
    k = _dot(h_hi, w_hi) + _dot(h_lo, w_hi) + _dot(h_hi, w_lo)
    decay = jnp.exp(-t_ref[...] * dl_ref[...]) + DECAY_SHIFT
    row = i * tr + lax.broadcasted_iota(jnp.int32, (tr, 1), 0)
    decay = jnp.where(row == seq_len, 0.0, decay)
    k = k * jnp.concatenate([decay, decay], axis=1)
    k_ref[...] = k.astype(k_ref.dtype)

    @pl.when(i == 0)
    def _():
        s_ref[...] = jnp.zeros_like(s_ref)

    s_ref[...] += jnp.sum(jnp.abs(k).reshape(tr // 8, 8, 2 * channels), axis=0)


def filter_gen(seq_len, channels, w1p, b1, w2, b2, w3, b3, freq, w4d):
    n = 2 * seq_len
    tr = min(512, seq_len)
    z, deltas = _filter_tables(seq_len, channels)
    zt = np.ascontiguousarray(z.T)
    tcol = np.ascontiguousarray(z[:, 0:1])
    hid = FILTER_HIDDEN
    full = lambda shape: pl.BlockSpec(shape, lambda i: (0,) * len(shape))
    body = functools.partial(_filter_body, tr=tr, seq_len=seq_len, channels=channels)
    return pl.pallas_call(
        body,
        out_shape=(jax.ShapeDtypeStruct((n, 2 * channels), BF16), jax.ShapeDtypeStruct((8, 2 * channels), F32)),
        grid=(n // tr,),
        in_specs=[
            pl.BlockSpec((hid, tr), lambda i: (0, i)),
            pl.BlockSpec((tr, 1), lambda i: (i, 0)),
            full((hid, hid)), full((hid, 1)), full((hid, hid)), full((hid, 1)), full((hid, hid)), full((hid, 1)),
            full((hid, 1)),
            pl.BlockSpec((None, hid, 2 * channels), lambda i: ((i * tr) // seq_len, 0, 0)),
            full((1, channels)),
        ],
        out_specs=(pl.BlockSpec((tr, 2 * channels), lambda i: (i, 0)), full((8, 2 * channels))),
        compiler_params=_cparams(("arbitrary",)),
        name="filter_gen",
    )(zt, tcol, w1p.T, b1.T, w2.T, b2.T, w3.T, b3.T, freq.T, w4d, deltas)


def _major_fwd_body(u_ref, a_ref, y_ref, *, groups, sub):
    a_in, _, ct = u_ref.shape
    kh = y_ref.shape[0]
    ub = u_ref[...].astype(F32)
    ys = []
    for q in range(groups):
        u = ub[:, q * sub:(q + 1) * sub, :].reshape(a_in * sub, ct)
        ys.append(_dot(a_ref[...], u.astype(BF16)).reshape(kh, 2, sub, ct))
    y_ref[...] = jnp.concatenate(ys, axis=2).astype(y_ref.dtype)


def major_fwd(u, col, channels, a_fwd, kh, ct, groups, sub, out_dtype):
    bsz, a_in, n2, _ = u.shape
    nc = channels // ct
    rows = groups * sub
    body = functools.partial(_major_fwd_body, groups=groups, sub=sub)
    return pl.pallas_call(
        body,
        out_shape=jax.ShapeDtypeStruct((bsz, kh, 2, n2, channels), out_dtype),
        grid=(bsz, n2 // rows, nc),
        in_specs=[
            pl.BlockSpec((None, a_in, rows, ct), lambda b, r, j: (b, 0, r, col * nc + j)),
            pl.BlockSpec(a_fwd.shape, lambda b, r, j: (0, 0)),
        ],
        out_specs=pl.BlockSpec((None, kh, 2, rows, ct), lambda b, r, j: (b, 0, 0, r, j)),
        compiler_params=_cparams(("parallel", "parallel", "parallel")),
        name="major_fwd",
    )(u, a_fwd)


def _long_conv_body(u_ref, af_ref, ai_ref, wf_ref, wi_ref, k_ref, m_ref, o_ref, yg_ref, *,
                    nb, nk, kg, groups, sub, n2):
    s = pl.program_id(2)
    bb, a_len, rows, ct = u_ref.shape
    kh = yg_ref.shape[1]
    sdt = yg_ref.dtype
    wide = F32 if sub == SUB else u_ref.dtype

    @pl.when(s < nb)
    def _():
        for i in range(bb):
            ub = u_ref[i].astype(wide)
            for q in range(groups):
                u = ub[:, q * sub:(q + 1) * sub, :].reshape(a_len * sub, ct)
                y = _dot(af_ref[...], u.astype(BF16)).reshape(kh, 2, sub, ct)
                dst = pl.ds(pl.multiple_of(s * rows + q * sub, sub), sub)
                yg_ref[i, :, :, dst, :] = y.astype(sdt)

    @pl.when(jnp.logical_and(s >= nb, s < nb + nk))
    def _():
        for kk in range(kg):
            k1 = (s - nb) * kg + kk
            kr = k_ref[kk, 0].astype(F32)
            ki = k_ref[kk, 1].astype(F32)
            for i in range(bb):
                yv = jnp.concatenate([yg_ref[i, k1, 0].astype(BF16), yg_ref[i, k1, 1].astype(BF16)], axis=0)
                z = _dot(wf_ref[kk], yv)
                zr, zi = z[:n2], z[n2:]
                pr = (zr * kr - zi * ki).astype(BF16)
                pi = (zr * ki + zi * kr).astype(BF16)
                g = _dot(wi_ref[kk], jnp.concatenate([pr, pi], axis=0))
                yg_ref[i, k1, 0] = g[:n2].astype(sdt)
                yg_ref[i, k1, 1] = g[n2:].astype(sdt)

    @pl.when(s >= nb + nk)
    def _():
        r = s - nb - nk
        for i in range(bb):
            ys = []
            for q in range(groups):
                src = pl.ds(pl.multiple_of(r * rows + q * sub, sub), sub)
                g = yg_ref[i, :, :, src, :].reshape(kh * 2 * sub, ct)
                ys.append(_dot(ai_ref[...], g.astype(BF16)).reshape(a_len, sub, ct))
            y = jnp.concatenate(ys, axis=1)
            o_ref[i] = (y * m_ref[i].astype(F32)).astype(o_ref.dtype)


def _k1_group(kh):
    return max(g for g in range(1, 18) if kh % g == 0)


def long_conv_gated(u, u_col, mult, mult_col, spectra, order, seq_len, channels, out_dtype):
    cst = _fft_consts(seq_len)
    kh, n2 = cst["kh"], cst["n2"]
    bsz, a_len = u.shape[0], u.shape[1]
    sub = cst["sub_sig"]
    ct = 256
    nc = channels // ct
    groups = _major_groups(a_len, sub, n2)
    rows = groups * sub
    nb = n2 // rows
    kg = _k1_group(kh)
    nk = kh // kg
    bb = max(g for g in (1, 2) if bsz % g == 0)
    a_fwd, a_inv = cst["a_fwd_half"], cst["a_inv"]

    def row_blk(s, first):
        return jnp.clip(s - first, 0, nb - 1)

    def k_blk(s):
        return jnp.clip(s - nb, 0, nk - 1)

    const = lambda shape: pl.BlockSpec(shape, lambda b, j, s: (0,) * len(shape), pipeline_mode=pl.Buffered(1))
    body = functools.partial(_long_conv_body, nb=nb, nk=nk, kg=kg, groups=groups, sub=sub, n2=n2)
    return pl.pallas_call(
        body,
        out_shape=jax.ShapeDtypeStruct((bsz, a_len, n2, channels), out_dtype),
        grid=(bsz // bb, nc, nb + nk + nb),
        in_specs=[
            pl.BlockSpec((bb, a_len, rows, ct), lambda b, j, s: (b, 0, row_blk(s, 0), u_col * nc + j)),
            const(a_fwd.shape),
            const(a_inv.shape),
            pl.BlockSpec((kg, 2 * n2, 2 * n2), lambda b, j, s: (k_blk(s), 0, 0)),
            pl.BlockSpec((kg, 2 * n2, 2 * n2), lambda b, j, s: (k_blk(s), 0, 0)),
            pl.BlockSpec((kg, 2, n2, ct), lambda b, j, s: (k_blk(s), 0, 0, order * nc + j)),
            pl.BlockSpec((bb, a_len, rows, ct), lambda b, j, s: (b, 0, row_blk(s, nb + nk), mult_col * nc + j)),
        ],
        out_specs=pl.BlockSpec((bb, a_len, rows, ct), lambda b, j, s: (b, 0, row_blk(s, nb + nk), j)),
        scratch_shapes=[pltpu.VMEM((bb, kh, 2, n2, ct), F32 if sub == SUB else BF16)],
        compiler_params=_cparams(("parallel", "parallel", "arbitrary")),
        name="long_conv",
    )(u, a_fwd, a_inv, cst["w_fwd"], cst["w_inv"], spectra, mult)


def _minor_filter_body(y_ref, wf_ref, inv_ref, bias_ref, k_ref, *, n2):
    z = _dot(wf_ref[...], jnp.concatenate([y_ref[0].astype(BF16), y_ref[1].astype(BF16)], axis=0))
    inv = inv_ref[...]
    k_ref[0] = (z[:n2] * inv + bias_ref[...]).astype(k_ref.dtype)
    k_ref[1] = (z[n2:] * inv).astype(k_ref.dtype)


def minor_filter(y, w_fwd, inv_norm, bias, ct):
    kh, _, n2, c2 = y.shape
    y_spec = pl.BlockSpec((None, 2, n2, ct), lambda k, j: (k, 0, 0, j))
    v_spec = pl.BlockSpec((1, ct), lambda k, j: (0, j))
    body = functools.partial(_minor_filter_body, n2=n2)
    return pl.pallas_call(
        body,
        out_shape=jax.ShapeDtypeStruct(y.shape, BF16),
        grid=(kh, c2 // ct),
        in_specs=[y_spec, pl.BlockSpec((None, 2 * n2, 2 * n2), lambda k, j: (k, 0, 0)), v_spec, v_spec],
        out_specs=y_spec,
        compiler_params=_cparams(("parallel", "parallel")),
        name="minor_filter",
    )(y, w_fwd, inv_norm, bias)


def _major_groups(a_in, sub, n2):
    return min(n2 // sub, max(1, 2048 // (a_in * sub)))


def hyena_filter_spectra(seq_len, channels, fw, f_bias):
    cst = _fft_consts(seq_len)
    n1, n2, kh = cst["n1"], cst["n2"], cst["kh"]
    kern, sums = filter_gen(seq_len, channels, *fw)
    inv_norm = 1.0 / jnp.sum(sums, axis=0, keepdims=True)
    u = kern.reshape(1, n1, n2, 2 * channels)
    y = major_fwd(u, 0, 2 * channels, cst["a_fwd"], kh, ct=512, groups=_major_groups(n1, SUB, n2), sub=SUB,
                  out_dtype=BF16)
    return minor_filter(y[0], cst["w_fwd"], inv_norm, f_bias.reshape(1, 2 * channels), ct=1024)


def _row_tile(seq_len, want):
    return min(want, seq_len)


def _layer(x, mem, lw, ws, layer, batch, seq_len, dims):
    d, dh, da, dm = dims
    n_heads = da // HEAD_DIM
    n_mem = mem.shape[0] // batch

    proj = norm_matmul_conv(x, lw["g_pre_mix"], ws["w_in"], layer, lw["conv_w"], lw["conv_b"], seq_len,
                            tm=_row_tile(seq_len, 1024), tn=512)

    cst = _fft_consts(seq_len)
    p4 = proj.reshape(batch, cst["a_out"], cst["n2"], proj.shape[1])
    spectra = lw["spectra"][seq_len]
    z = long_conv_gated(p4, 0, p4, 1, spectra, 0, seq_len, dh, BF16)
    y_h = long_conv_gated(z, 0, p4, 2, spectra, 1, seq_len, dh, BF16).reshape(batch * seq_len, dh)

    qcol = 3 * dh // HEAD_DIM
    y_a = dil_attn(proj, batch, seq_len, qcol, qcol + n_heads, qcol + 2 * n_heads, n_heads)
    kv = norm_matmul(mem, lw["g_mem"], ws["w_mem_kv"], layer, tm=n_mem, tn=512)
    y_m = mem_attn(proj, kv, batch, seq_len, n_mem, qcol + 3 * n_heads, dm // HEAD_DIM, tq=_row_tile(seq_len, 1024))

    x = mix_out(y_h, y_a, y_m, lw["g_grp"], ws["w_out"], layer, lw["g_post_mix"], x, _row_tile(seq_len, 512))
    return ffn(x, lw["g_pre_ffn"], ws["w_up"], lw["ffn_conv_w"], lw["ffn_conv_b"], ws["w_down"],
               lw["g_post_ffn"], layer, seq_len, _row_tile(seq_len, 1024), tf=512)


def kernel(x_prompt, x_sample, mem_prompt, mem_sample, g_pre_mix, w_in, conv_w, conv_b, f_w1, f_b1, f_w2, f_b2, f_w3, f_b3, f_w4, f_freq, f_bias, g_mem, w_mem_kv, g_grp, w_out, g_post_mix, g_pre_ffn, w_up, ffn_conv_w, ffn_conv_b, w_down, g_post_ffn):
    depth, d, d_in = w_in.shape
    dh = conv_w.shape[2] // 3
    dm = w_mem_kv.shape[2] // 2
    da = (d_in - 3 * dh - dm) // 3
    dims = (d, dh, da, dm)
    groups = [(x_prompt, mem_prompt), (x_sample, mem_sample)]
    seq_lens = sorted({g[0].shape[1] for g in groups})

    xs = [g[0].reshape(-1, d) for g in groups]
    mems = [g[1].reshape(-1, d) for g in groups]
    hid = FILTER_HIDDEN
    ws = dict(w_in=w_in.astype(BF16), w_mem_kv=w_mem_kv.astype(BF16), w_out=w_out.astype(BF16),
              w_up=w_up.astype(BF16), w_down=w_down.astype(BF16))
    for i in range(depth):
        row = lambda v: v[i][None, :]
        w1p = jnp.zeros((hid, hid), F32).at[:FILTER_EMB].set(f_w1[i])
        w4d = f_w4[i].reshape(hid, 2, 2, dh).transpose(2, 0, 1, 3).reshape(2, hid, 2 * dh)
        fw = (w1p, row(f_b1), f_w2[i], row(f_b2), f_w3[i], row(f_b3), row(f_freq), w4d)
        lw = dict(
            g_pre_mix=row(g_pre_mix), conv_w=conv_w[i], conv_b=row(conv_b),
            g_mem=row(g_mem), g_grp=row(g_grp), g_post_mix=row(g_post_mix), g_pre_ffn=row(g_pre_ffn),
            ffn_conv_w=ffn_conv_w[i], ffn_conv_b=row(ffn_conv_b), g_post_ffn=row(g_post_ffn),
            spectra={sl: hyena_filter_spectra(sl, dh, fw, f_bias[i]) for sl in seq_lens},
        )
        for gi, (xg, _) in enumerate(groups):
            xs[gi] = _layer(xs[gi], mems[gi], lw, ws, i, xg.shape[0], xg.shape[1], dims)
    return tuple(x.reshape(g[0].shape) for x, g in zip(xs, groups))
```

```python
import functools
import math

import numpy as np
import jax
import jax.numpy as jnp
from jax import lax
from jax.experimental import pallas as pl
from jax.experimental.pallas import tpu as pltpu

F32 = jnp.float32
BF16 = jnp.bfloat16

HEAD_DIM = 128
DILATED_PATTERNS = ((128, 1), (512, 4), (2048, 16))
ROPE_THETA = 500000.0
ROT_DIM = HEAD_DIM // 4
FILTER_EMB = 33
FILTER_HIDDEN = 64
DECAY_FAST = 0.3
DECAY_SLOW = 1.5
DECAY_TARGET = 1e-2
DECAY_SHIFT = 0.05
EPS = 1e-6
NEG = -1e30

HALO = 16
ATTN_BLOCKS_PER_STEP = 8
SUB = 8
SUB_BF16 = 16
MXU_DEPTH = 256
VMEM_LIMIT = 60 * 1024 * 1024


def _cparams(sem):
    return pltpu.CompilerParams(dimension_semantics=sem, vmem_limit_bytes=VMEM_LIMIT)


def _split(x):
    hi = x.astype(BF16)
    lo = (x - hi.astype(F32)).astype(BF16)
    return hi, lo


def _dot(a, b):
    return jnp.dot(a, b, preferred_element_type=F32)


def _rms(v, g):
    return v * lax.rsqrt(jnp.mean(v * v, axis=-1, keepdims=True) + EPS) * g


def _fill_normed(hb_ref, xp_ref, x_ref, xn_ref, g_ref, i, tm, blocks_per_seq):
    g = g_ref[...]
    pos = i % blocks_per_seq
    keep_p = (pos != 0).astype(F32)
    keep_n = (pos != blocks_per_seq - 1).astype(F32)
    hb_ref[0:HALO, :] = (_rms(xp_ref[...], g) * keep_p).astype(BF16)
    hb_ref[HALO:HALO + tm, :] = _rms(x_ref[...], g).astype(BF16)
    hb_ref[HALO + tm:, :] = (_rms(xn_ref[...], g) * keep_n).astype(BF16)


def _conv3(u, cw, cb, tm):
    rows = tm + 2 * HALO
    up = pltpu.roll(u, 1, 0)[HALO:HALO + tm]
    un = pltpu.roll(u, rows - 1, 0)[HALO:HALO + tm]
    uc = u[HALO:HALO + tm]
    return up * cw[0:1] + uc * cw[1:2] + un * cw[2:3] + cb


def _nm_conv_body(xp_ref, x_ref, xn_ref, g_ref, w_ref, cw_ref, cb_ref, o_ref, hb_ref, *, tm, blocks_per_seq,
                  n_conv):
    i = pl.program_id(0)
    j = pl.program_id(1)

    @pl.when(j == 0)
    def _():
        _fill_normed(hb_ref, xp_ref, x_ref, xn_ref, g_ref, i, tm, blocks_per_seq)

    u = _dot(hb_ref[...], w_ref[...])

    @pl.when(j < n_conv)
    def _():
        o_ref[...] = _conv3(u, cw_ref[...], cb_ref[...], tm).astype(o_ref.dtype)

    @pl.when(j >= n_conv)
    def _():
        o_ref[...] = u[HALO:HALO + tm].astype(o_ref.dtype)


def _nm_body(x_ref, g_ref, w_ref, o_ref, hb_ref):
    @pl.when(pl.program_id(1) == 0)
    def _():
        hb_ref[...] = _rms(x_ref[...], g_ref[...]).astype(BF16)

    o_ref[...] = _dot(hb_ref[...], w_ref[...]).astype(o_ref.dtype)


def _halo_specs(tm, d, n_rows, x_buffers=2):
    hb = tm // HALO
    last = n_rows // HALO - 1
    return [
        pl.BlockSpec((HALO, d), lambda i, j: (jnp.maximum(i * hb - 1, 0), 0)),
        pl.BlockSpec((tm, d), lambda i, j: (i, 0), pipeline_mode=pl.Buffered(x_buffers)),
        pl.BlockSpec((HALO, d), lambda i, j: (jnp.minimum((i + 1) * hb, last), 0)),
    ]


def norm_matmul_conv(x, g, w, layer, cw, cb, seq_len, tm, tn):
    t, d = x.shape
    n = w.shape[2]
    n_conv = cw.shape[1] // tn
    body = functools.partial(_nm_conv_body, tm=tm, blocks_per_seq=seq_len // tm, n_conv=n_conv)
    return pl.pallas_call(
        body,
        out_shape=jax.ShapeDtypeStruct((t, n), BF16),
        grid=(t // tm, n // tn),
        in_specs=_halo_specs(tm, d, t) + [
            pl.BlockSpec((1, d), lambda i, j: (0, 0)),
            pl.BlockSpec((None, d, tn), lambda i, j: (layer, 0, j)),
            pl.BlockSpec((3, tn), lambda i, j: (0, jnp.minimum(j, n_conv - 1))),
            pl.BlockSpec((1, tn), lambda i, j: (0, jnp.minimum(j, n_conv - 1))),
        ],
        out_specs=pl.BlockSpec((tm, tn), lambda i, j: (i, j)),
        scratch_shapes=[pltpu.VMEM((tm + 2 * HALO, d), BF16)],
        compiler_params=_cparams(("parallel", "arbitrary")),
        name="norm_matmul_conv",
    )(x, x, x, g, w, cw, cb)


def norm_matmul(x, g, w, layer, tm, tn):
    t, d = x.shape
    n = w.shape[2]
    return pl.pallas_call(
        _nm_body,
        out_shape=jax.ShapeDtypeStruct((t, n), BF16),
        grid=(t // tm, n // tn),
        in_specs=[
            pl.BlockSpec((tm, d), lambda i, j: (i, 0)),
            pl.BlockSpec((1, d), lambda i, j: (0, 0)),
            pl.BlockSpec((None, d, tn), lambda i, j: (layer, 0, j)),
        ],
        out_specs=pl.BlockSpec((tm, tn), lambda i, j: (i, j)),
        scratch_shapes=[pltpu.VMEM((tm, d), BF16)],
        compiler_params=_cparams(("parallel", "arbitrary")),
        name="norm_matmul",
    )(x, g, w)


def _ffn_body(xp_ref, x_ref, xn_ref, g_ref, wg_ref, wv_ref, cw_ref, cb_ref, wd_ref, gp_ref, o_ref,
              hb_ref, *, tm, blocks_per_seq, nj):
    i = pl.program_id(0)
    j = pl.program_id(1)

    @pl.when(j == 0)
    def _():
        _fill_normed(hb_ref, xp_ref, x_ref, xn_ref, g_ref, i, tm, blocks_per_seq)
        o_ref[...] = jnp.zeros_like(o_ref)

    ug = _dot(hb_ref[...], wg_ref[...])
    val = _dot(hb_ref[HALO:HALO + tm, :], wv_ref[...])
    gate = _conv3(ug, cw_ref[...], cb_ref[...], tm)
    ff = jax.nn.gelu(gate, approximate=True) * val
    o_ref[...] += _dot(ff.astype(BF16), wd_ref[...])

    @pl.when(j == nj - 1)
    def _():
        o_ref[...] = x_ref[...] + _rms(o_ref[...], gp_ref[...])


def ffn(x, g_pre, w_up, cw, cb, w_down, g_post, layer, seq_len, tm, tf):
    t, d = x.shape
    d_ff = w_down.shape[1]
    nj = d_ff // tf
    body = functools.partial(_ffn_body, tm=tm, blocks_per_seq=seq_len // tm, nj=nj)
    return pl.pallas_call(
        body,
        out_shape=jax.ShapeDtypeStruct((t, d), F32),
        grid=(t // tm, nj),
        in_specs=_halo_specs(tm, d, t) + [
            pl.BlockSpec((1, d), lambda i, j: (0, 0)),
            pl.BlockSpec((None, d, tf), lambda i, j: (layer, 0, j)),
            pl.BlockSpec((None, d, tf), lambda i, j: (layer, 0, j + nj)),
            pl.BlockSpec((3, tf), lambda i, j: (0, j)),
            pl.BlockSpec((1, tf), lambda i, j: (0, j)),
            pl.BlockSpec((None, tf, d), lambda i, j: (layer, j, 0)),
            pl.BlockSpec((1, d), lambda i, j: (0, 0)),
        ],
        out_specs=pl.BlockSpec((tm, d), lambda i, j: (i, 0)),
        scratch_shapes=[pltpu.VMEM((tm + 2 * HALO, d), BF16)],
        compiler_params=_cparams(("parallel", "arbitrary")),
        name="ffn",
    )(x, x, x, g_pre, w_up, w_up, cw, cb, w_down, g_post)


def _mix_out_body(yh_ref, ya_ref, ym_ref, gg_ref, w_ref, gp_ref, x_ref, o_ref, *, dh, da):
    gg = gg_ref[...]
    tm = x_ref.shape[0]
    for rows in (slice(0, tm // 2), slice(tm // 2, tm)):
        acc = _dot(_rms(yh_ref[rows, :].astype(F32), gg[:, :dh]).astype(BF16), w_ref[0:dh, :])
        acc += _dot(_rms(ya_ref[rows, :].astype(F32), gg[:, dh:dh + da]).astype(BF16), w_ref[dh:dh + da, :])
        acc += _dot(_rms(ym_ref[rows, :].astype(F32), gg[:, dh + da:]).astype(BF16), w_ref[dh + da:, :])
        o_ref[rows, :] = x_ref[rows, :] + _rms(acc, gp_ref[...])


def mix_out(yh, ya, ym, g_grp, w_out, layer, g_post, x, tm):
    t, d = x.shape
    dh, da, dm = yh.shape[1], ya.shape[1], ym.shape[1]
    body = functools.partial(_mix_out_body, dh=dh, da=da)
    return pl.pallas_call(
        body,
        out_shape=jax.ShapeDtypeStruct((t, d), F32),
        grid=(t // tm,),
        in_specs=[
            pl.BlockSpec((tm, dh), lambda i: (i, 0)),
            pl.BlockSpec((tm, da), lambda i: (i, 0)),
            pl.BlockSpec((tm, dm), lambda i: (i, 0)),
            pl.BlockSpec((1, d), lambda i: (0, 0)),
            pl.BlockSpec((None, d, d), lambda i: (layer, 0, 0)),
            pl.BlockSpec((1, d), lambda i: (0, 0)),
            pl.BlockSpec((tm, d), lambda i: (i, 0)),
        ],
        out_specs=pl.BlockSpec((tm, d), lambda i: (i, 0)),
        compiler_params=_cparams(("parallel",)),
        name="mix_out",
    )(yh, ya, ym, g_grp, w_out, g_post, x)


def _rope_tables(seq_len):
    half = ROT_DIM // 2
    inv_freq = np.exp(-math.log(ROPE_THETA) * np.arange(0, ROT_DIM, 2, dtype=np.float64) / ROT_DIM)
    ang = np.arange(seq_len, dtype=np.float64)[:, None] * inv_freq.astype(np.float32).astype(np.float64)[None, :]
    c, s = np.cos(ang), np.sin(ang)
    cos_t = np.ones((seq_len, HEAD_DIM), np.float64)
    sin_t = np.zeros((seq_len, HEAD_DIM), np.float64)
    cos_t[:, :half] = c
    cos_t[:, half:ROT_DIM] = c
    sin_t[:, :half] = -s
    sin_t[:, half:ROT_DIM] = s
    return cos_t.astype(np.float32), sin_t.astype(np.float32)


def _dil_attn_body(q_ref, k_ref, v_ref, cos_ref, sin_ref, o_ref, qs_ref, ks_ref, vs_ref, m_ref, l_ref, *, seq_len):
    half = ROT_DIM // 2
    scale = 1.0 / math.sqrt(HEAD_DIM)
    chunk = min(512, seq_len)

    def rope_chunk(c, carry):
        rows = pl.ds(pl.multiple_of(c * chunk, chunk), chunk)
        cs = cos_ref[rows, :]
        sn = sin_ref[rows, :]
        lane = lax.broadcasted_iota(jnp.int32, (chunk, HEAD_DIM), 1)
        for src, dst, mul in ((q_ref, qs_ref, scale), (k_ref, ks_ref, None)):
            x = src[rows, :].astype(F32)
            partner = jnp.where(lane < half, pltpu.roll(x, HEAD_DIM - half, 1), pltpu.roll(x, half, 1))
            y = x * cs + partner * sn
            dst[rows, :] = y if mul is None else y * mul
        vs_ref[rows, :] = v_ref[rows, :].astype(F32)
        return carry

    lax.fori_loop(0, seq_len // chunk, rope_chunk, 0)

    n_br = len(DILATED_PATTERNS)
    for bi, (window, dil) in enumerate(DILATED_PATTERNS):
        radius = window // (2 * dil)
        n = seq_len // dil
        tq = min(128, n)
        kw = min(tq + 2 * radius, n)
        nblk = n // tq

        def one_block(idx, bi=bi, dil=dil, radius=radius, n=n, tq=tq, kw=kw, nblk=nblk):
            r = idx // nblk
            q0 = (idx % nblk) * tq
            k0 = jnp.clip(q0 - radius, 0, n - kw)
            if dil == 1:
                qsl = pl.ds(pl.multiple_of(q0, tq), tq)
                ksl = pl.ds(pl.multiple_of(k0, 8), kw)
            else:
                qsl = pl.ds(r + q0 * dil, tq, stride=dil)
                ksl = pl.ds(r + k0 * dil, kw, stride=dil)
            qb = qs_ref[qsl, :].astype(BF16)
            kb = ks_ref[ksl, :].astype(BF16)
            vb = vs_ref[ksl, :].astype(BF16)
            s = lax.dot_general(qb, kb, (((1,), (1,)), ((), ())), preferred_element_type=F32)
            rel = (k0 + lax.broadcasted_iota(jnp.int32, (tq, kw), 1)) - (q0 + lax.broadcasted_iota(jnp.int32, (tq, kw), 0))
            s = jnp.where(jnp.abs(rel) <= radius, s, NEG)
            m = jnp.max(s, axis=-1, keepdims=True)
            p = jnp.exp(s - m)
            l = jnp.sum(p, axis=-1, keepdims=True)
            acc = _dot(p.astype(BF16), vb)
            m = jnp.broadcast_to(m, (tq, HEAD_DIM))
            l = jnp.broadcast_to(l, (tq, HEAD_DIM))
            if bi > 0:
                m_old = m_ref[qsl, :]
                m_new = jnp.maximum(m_old, m)
                a_old = jnp.exp(m_old - m_new)
                a_new = jnp.exp(m - m_new)
                acc = o_ref[qsl, :] * a_old + acc * a_new
                l = l_ref[qsl, :] * a_old + l * a_new
                m = m_new
            return qsl, m, l, acc

        total = dil * nblk
        unroll = min(ATTN_BLOCKS_PER_STEP, total)

        def step(it, carry, bi=bi, unroll=unroll, one_block=one_block):
            done = [one_block(it * unroll + u) for u in range(unroll)]
            for qsl, m, l, acc in done:
                if bi == n_br - 1:
                    o_ref[qsl, :] = acc / l
                else:
                    o_ref[qsl, :] = acc
                    m_ref[qsl, :] = m
                    l_ref[qsl, :] = l
            return carry

        lax.fori_loop(0, total // unroll, step, 0)


def dil_attn(proj, batch, seq_len, q_col, k_col, v_col, n_heads):
    cos_t, sin_t = _rope_tables(seq_len)
    nb = 1 if seq_len * HEAD_DIM * 4 > (2 << 20) else 2

    def col_spec(col):
        return pl.BlockSpec((seq_len, HEAD_DIM), lambda b, h: (b, col + h))

    tab_spec = pl.BlockSpec((seq_len, HEAD_DIM), lambda b, h: (0, 0), pipeline_mode=pl.Buffered(1))
    body = functools.partial(_dil_attn_body, seq_len=seq_len)
    return pl.pallas_call(
        body,
        out_shape=jax.ShapeDtypeStruct((batch * seq_len, n_heads * HEAD_DIM), F32),
        grid=(batch, n_heads),
        in_specs=[col_spec(q_col), col_spec(k_col), col_spec(v_col), tab_spec, tab_spec],
        out_specs=pl.BlockSpec((seq_len, HEAD_DIM), lambda b, h: (b, h), pipeline_mode=pl.Buffered(nb)),
        scratch_shapes=[pltpu.VMEM((seq_len, HEAD_DIM), F32)] * 5,
        compiler_params=_cparams(("parallel", "parallel")),
        name="dil_attn",
    )(proj, proj, proj, cos_t, sin_t)


def _mem_attn_body(q_ref, k_ref, v_ref, o_ref, *, n_heads):
    for h in range(n_heads):
        cols = slice(h * HEAD_DIM, (h + 1) * HEAD_DIM)
        s = lax.dot_general(q_ref[:, cols].astype(BF16), k_ref[:, cols].astype(BF16), (((1,), (1,)), ((), ())),
                            preferred_element_type=F32) * (1.0 / math.sqrt(HEAD_DIM))
        m = jnp.max(s, axis=-1, keepdims=True)
        p = jnp.exp(s - m)
        l = jnp.sum(p, axis=-1, keepdims=True)
        o_ref[:, cols] = (_dot(p.astype(BF16), v_ref[:, cols].astype(BF16)) / l).astype(o_ref.dtype)


def mem_attn(proj, kv, batch, seq_len, n_mem, q_col, n_heads, tq):
    nq = seq_len // tq
    width = n_heads * HEAD_DIM
    assert q_col % n_heads == 0, "the query heads must start at a multiple of their total width"
    body = functools.partial(_mem_attn_body, n_heads=n_heads)
    return pl.pallas_call(
        body,
        out_shape=jax.ShapeDtypeStruct((batch * seq_len, width), BF16),
        grid=(batch, nq),
        in_specs=[
            pl.BlockSpec((tq, width), lambda b, i: (b * nq + i, q_col // n_heads)),
            pl.BlockSpec((n_mem, width), lambda b, i: (b, 0)),
            pl.BlockSpec((n_mem, width), lambda b, i: (b, 1)),
        ],
        out_specs=pl.BlockSpec((tq, width), lambda b, i: (b * nq + i, 0)),
        compiler_params=_cparams(("parallel", "parallel")),
        name="mem_attn",
    )(proj, kv, kv)


def _fft_split(seq_len):
    n = 2 * seq_len
    n2 = 128
    return n // n2, n2


@functools.lru_cache(maxsize=None)
def _fft_consts(seq_len):
    n1, n2 = _fft_split(seq_len)
    n = n1 * n2
    kh = n1 // 2 + 1
    a_out = n1 // 2
    a = np.arange(n1)
    k1 = np.arange(kh)
    th = 2.0 * np.pi * ((k1[:, None] * a[None, :]) % n1) / n1
    fwd = np.stack([np.cos(th), -np.sin(th)], axis=1).reshape(2 * kh, n1)
    alpha = np.where((k1 == 0) | (k1 == n1 // 2), 1.0, 2.0) / n
    th_i = th[:, :a_out].T
    inv = np.stack([np.cos(th_i) * alpha[None, :], -np.sin(th_i) * alpha[None, :]], axis=2)
    inv = inv.reshape(a_out, 2 * kh)
    b = np.arange(n2)
    k2 = np.arange(n2)
    kk = k1[:, None, None] + n1 * k2[None, :, None]
    ph = 2.0 * np.pi * ((kk * b[None, None, :]) % n) / n
    wr, wi = np.cos(ph), -np.sin(ph)
    w_blk = np.concatenate([np.concatenate([wr, -wi], axis=2), np.concatenate([wi, wr], axis=2)], axis=1)

    def kron_bf16(m, sub):
        return np.asarray(np.kron(m, np.eye(sub)), np.float32).astype(BF16)

    sub_sig = SUB if a_out * SUB >= MXU_DEPTH else SUB_BF16
    return dict(n1=n1, n2=n2, kh=kh, a_out=a_out, sub_sig=sub_sig,
                a_fwd=kron_bf16(fwd, SUB),
                a_fwd_half=kron_bf16(fwd[:, :a_out], sub_sig),
                a_inv=kron_bf16(inv, sub_sig),
                w_fwd=np.asarray(w_blk, np.float32).astype(BF16))


def _filter_tables(seq_len, channels):
    n = 2 * seq_len
    bands = (FILTER_EMB - 1) // 2
    pos = np.concatenate([np.arange(seq_len), [0], np.arange(seq_len - 1, 0, -1)])
    t = np.linspace(0.0, 1.0, seq_len)[pos]
    w = 2.0 * np.pi * pos / seq_len
    f = np.linspace(1e-4, bands - 1, bands)
    ang = w[:, None] * f[None, :]
    z = np.zeros((n, FILTER_HIDDEN), np.float64)
    z[:, 0] = t
    z[:, 1:1 + bands] = np.cos(ang)
    z[:, 1 + bands:1 + 2 * bands] = -np.sin(ang)
    deltas = np.abs(np.linspace(math.log(DECAY_TARGET) / DECAY_SLOW, math.log(DECAY_TARGET) / DECAY_FAST, channels))
    return z.astype(np.float32), deltas[None, :].astype(np.float32)


def _filter_body(zt_ref, t_ref, w1_ref, b1_ref, w2_ref, b2_ref, w3_ref, b3_ref, fr_ref, w4_ref, dl_ref,
                 k_ref, s_ref, *, tr, seq_len, channels):
    i = pl.program_id(0)
    fr = fr_ref[...]

    def layer(ht, wt_ref, b_ref):
        w_hi, w_lo = _split(wt_ref[...])
        h_hi, h_lo = _split(ht)
        pre = _dot(w_hi, h_hi) + _dot(w_hi, h_lo) + _dot(w_lo, h_hi)
        return jnp.sin(fr * (pre + b_ref[...]))

    ht = layer(zt_ref[...], w1_ref, b1_ref)
    ht = layer(ht, w2_ref, b2_ref)
    ht = layer(ht, w3_ref, b3_ref)
    h = ht.T
    w_hi, w_lo = _split(w4_ref[...])
    h_hi, h_lo = _split(h)
    k = _dot(h_hi, w_hi) + _dot(h_lo, w_hi) + _dot(h_hi, w_lo)
    decay = jnp.exp(-t_ref[...] * dl_ref[...]) + DECAY_SHIFT
    row = i * tr + lax.broadcasted_iota(jnp.int32, (tr, 1), 0)
    decay = jnp.where(row == seq_len, 0.0, decay)
    k = k * jnp.concatenate([decay, decay], axis=1)
    k_ref[...] = k.astype(k_ref.dtype)

    @pl.when(i == 0)
    def _():
        s_ref[...] = jnp.zeros_like(s_ref)

    s_ref[...] += jnp.sum(jnp.abs(k).reshape(tr // 8, 8, 2 * channels), axis=0)


def filter_gen(seq_len, channels, w1p, b1, w2, b2, w3, b3, freq, w4d):
    n = 2 * seq_len
    tr = min(512, seq_len)
    z, deltas = _filter_tables(seq_len, channels)
    zt = np.ascontiguousarray(z.T)
    tcol = np.ascontiguousarray(z[:, 0:1])
    hid = FILTER_HIDDEN
    full = lambda shape: pl.BlockSpec(shape, lambda i: (0,) * len(shape))
    body = functools.partial(_filter_body, tr=tr, seq_len=seq_len, channels=channels)
    return pl.pallas_call(
        body,
        out_shape=(jax.ShapeDtypeStruct((n, 2 * channels), BF16), jax.ShapeDtypeStruct((8, 2 * channels), F32)),
        grid=(n // tr,),
        in_specs=[
            pl.BlockSpec((hid, tr), lambda i: (0, i)),
            pl.BlockSpec((tr, 1), lambda i: (i, 0)),
            full((hid, hid)), full((hid, 1)), full((hid, hid)), full((hid, 1)), full((hid, hid)), full((hid, 1)),
            full((hid, 1)),
            pl.BlockSpec((None, hid, 2 * channels), lambda i: ((i * tr) // seq_len, 0, 0)),
            full((1, channels)),
        ],
        out_specs=(pl.BlockSpec((tr, 2 * channels), lambda i: (i, 0)), full((8, 2 * channels))),
        compiler_params=_cparams(("arbitrary",)),
        name="filter_gen",
    )(zt, tcol, w1p.T, b1.T, w2.T, b2.T, w3.T, b3.T, freq.T, w4d, deltas)


def _major_fwd_body(u_ref, a_ref, y_ref, *, groups, sub):
    a_in, _, ct = u_ref.shape
    kh = y_ref.shape[0]
    ub = u_ref[...].astype(F32)
    ys = []
    for q in range(groups):
        u = ub[:, q * sub:(q + 1) * sub, :].reshape(a_in * sub, ct)
        ys.append(_dot(a_ref[...], u.astype(BF16)).reshape(kh, 2, sub, ct))
    y_ref[...] = jnp.concatenate(ys, axis=2).astype(y_ref.dtype)


def major_fwd(u, col, channels, a_fwd, kh, ct, groups, sub, out_dtype):
    bsz, a_in, n2, _ = u.shape
    nc = channels // ct
    rows = groups * sub
    body = functools.partial(_major_fwd_body, groups=groups, sub=sub)
    return pl.pallas_call(
        body,
        out_shape=jax.ShapeDtypeStruct((bsz, kh, 2, n2, channels), out_dtype),
        grid=(bsz, n2 // rows, nc),
        in_specs=[
            pl.BlockSpec((None, a_in, rows, ct), lambda b, r, j: (b, 0, r, col * nc + j)),
            pl.BlockSpec(a_fwd.shape, lambda b, r, j: (0, 0)),
        ],
        out_specs=pl.BlockSpec((None, kh, 2, rows, ct), lambda b, r, j: (b, 0, 0, r, j)),
        compiler_params=_cparams(("parallel", "parallel", "parallel")),
        name="major_fwd",
    )(u, a_fwd)


def _long_conv_body(u_ref, af_ref, ai_ref, wf_ref, k_ref, m_ref, o_ref, yg_ref, *,
                    nb, nk, kg, groups, sub, n2):
    s = pl.program_id(2)
    bb, a_len, rows, ct = u_ref.shape
    kh = yg_ref.shape[1]
    sdt = yg_ref.dtype
    wide = F32 if sub == SUB else u_ref.dtype

    @pl.when(s < nb)
    def _():
        for i in range(bb):
            ub = u_ref[i].astype(wide)
            for q in range(groups):
                u = ub[:, q * sub:(q + 1) * sub, :].reshape(a_len * sub, ct)
                y = _dot(af_ref[...], u.astype(BF16)).reshape(kh, 2, sub, ct)
                dst = pl.ds(pl.multiple_of(s * rows + q * sub, sub), sub)
                yg_ref[i, :, :, dst, :] = y.astype(sdt)

    @pl.when(jnp.logical_and(s >= nb, s < nb + nk))
    def _():
        for kk in range(kg):
            k1 = (s - nb) * kg + kk
            kr = k_ref[kk, 0].astype(F32)
            ki = k_ref[kk, 1].astype(F32)
            for i in range(bb):
                yv = jnp.concatenate([yg_ref[i, k1, 0].astype(BF16), yg_ref[i, k1, 1].astype(BF16)], axis=0)
                z = _dot(wf_ref[kk], yv)
                zr, zi = z[:n2], z[n2:]
                pr = (zr * kr - zi * ki).astype(BF16)
                pi = (zr * ki + zi * kr).astype(BF16)
                g = lax.dot_general(wf_ref[kk], jnp.concatenate([pr, pi], axis=0), (((0,), (0,)), ((), ())),
                                    preferred_element_type=F32)
                yg_ref[i, k1, 0] = g[:n2].astype(sdt)
                yg_ref[i, k1, 1] = g[n2:].astype(sdt)

    @pl.when(s >= nb + nk)
    def _():
        r = s - nb - nk
        for i in range(bb):
            ys = []
            for q in range(groups):
                src = pl.ds(pl.multiple_of(r * rows + q * sub, sub), sub)
                g = yg_ref[i, :, :, src, :].reshape(kh * 2 * sub, ct)
                ys.append(_dot(ai_ref[...], g.astype(BF16)).reshape(a_len, sub, ct))
            y = jnp.concatenate(ys, axis=1)
            o_ref[i] = (y * m_ref[i].astype(F32)).astype(o_ref.dtype)


def _k1_group(kh):
    return max(g for g in range(1, 18) if kh % g == 0)


def long_conv_gated(u, u_col, mult, mult_col, spectra, order, seq_len, channels, out_dtype):
    cst = _fft_consts(seq_len)
    kh, n2 = cst["kh"], cst["n2"]
    bsz, a_len = u.shape[0], u.shape[1]
    sub = cst["sub_sig"]
    ct = 256
    nc = channels // ct
    groups = _major_groups(a_len, sub, n2)
    rows = groups * sub
    nb = n2 // rows
    kg = _k1_group(kh)
    nk = kh // kg
    bb = max(g for g in (1, 2) if bsz % g == 0)
    a_fwd, a_inv = cst["a_fwd_half"], cst["a_inv"]

    def row_blk(s, first):
        return jnp.clip(s - first, 0, nb - 1)

    def k_blk(s):
        return jnp.clip(s - nb, 0, nk - 1)

    const = lambda shape: pl.BlockSpec(shape, lambda b, j, s: (0,) * len(shape), pipeline_mode=pl.Buffered(1))
    body = functools.partial(_long_conv_body, nb=nb, nk=nk, kg=kg, groups=groups, sub=sub, n2=n2)
    return pl.pallas_call(
        body,
        out_shape=jax.ShapeDtypeStruct((bsz, a_len, n2, channels), out_dtype),
        grid=(bsz // bb, nc, nb + nk + nb),
        in_specs=[
            pl.BlockSpec((bb, a_len, rows, ct), lambda b, j, s: (b, 0, row_blk(s, 0), u_col * nc + j)),
            const(a_fwd.shape),
            const(a_inv.shape),
            pl.BlockSpec((kg, 2 * n2, 2 * n2), lambda b, j, s: (k_blk(s), 0, 0)),
            pl.BlockSpec((kg, 2, n2, ct), lambda b, j, s: (k_blk(s), 0, 0, order * nc + j)),
            pl.BlockSpec((bb, a_len, rows, ct), lambda b, j, s: (b, 0, row_blk(s, nb + nk), mult_col * nc + j)),
        ],
        out_specs=pl.BlockSpec((bb, a_len, rows, ct), lambda b, j, s: (b, 0, row_blk(s, nb + nk), j)),
        scratch_shapes=[pltpu.VMEM((bb, kh, 2, n2, ct), F32 if sub == SUB else BF16)],
        compiler_params=_cparams(("parallel", "parallel", "arbitrary")),
        name="long_conv",
    )(u, a_fwd, a_inv, cst["w_fwd"], spectra, mult)


def _minor_filter_body(y_ref, wf_ref, inv_ref, bias_ref, k_ref, *, n2):
    z = _dot(wf_ref[...], jnp.concatenate([y_ref[0].astype(BF16), y_ref[1].astype(BF16)], axis=0))
    inv = inv_ref[...]
    k_ref[0] = (z[:n2] * inv + bias_ref[...]).astype(k_ref.dtype)
    k_ref[1] = (z[n2:] * inv).astype(k_ref.dtype)


def minor_filter(y, w_fwd, inv_norm, bias, ct):
    kh, _, n2, c2 = y.shape
    y_spec = pl.BlockSpec((None, 2, n2, ct), lambda k, j: (k, 0, 0, j))
    v_spec = pl.BlockSpec((1, ct), lambda k, j: (0, j))
    body = functools.partial(_minor_filter_body, n2=n2)
    return pl.pallas_call(
        body,
        out_shape=jax.ShapeDtypeStruct(y.shape, BF16),
        grid=(kh, c2 // ct),
        in_specs=[y_spec, pl.BlockSpec((None, 2 * n2, 2 * n2), lambda k, j: (k, 0, 0)), v_spec, v_spec],
        out_specs=y_spec,
        compiler_params=_cparams(("parallel", "parallel")),
        name="minor_filter",
    )(y, w_fwd, inv_norm, bias)


def _major_groups(a_in, sub, n2):
    return min(n2 // sub, max(1, 2048 // (a_in * sub)))


def hyena_filter_spectra(seq_len, channels, fw, f_bias):
    cst = _fft_consts(seq_len)
    n1, n2, kh = cst["n1"], cst["n2"], cst["kh"]
    kern, sums = filter_gen(seq_len, channels, *fw)
    inv_norm = 1.0 / jnp.sum(sums, axis=0, keepdims=True)
    u = kern.reshape(1, n1, n2, 2 * channels)
    y = major_fwd(u, 0, 2 * channels, cst["a_fwd"], kh, ct=512, groups=_major_groups(n1, SUB, n2), sub=SUB,
                  out_dtype=BF16)
    return minor_filter(y[0], cst["w_fwd"], inv_norm, f_bias.reshape(1, 2 * channels), ct=1024)


def _row_tile(seq_len, want):
    return min(want, seq_len)


def _layer(x, mem, lw, ws, layer, batch, seq_len, dims):
    d, dh, da, dm = dims
    n_heads = da // HEAD_DIM
    n_mem = mem.shape[0] // batch

    proj = norm_matmul_conv(x, lw["g_pre_mix"], ws["w_in"], layer, lw["conv_w"], lw["conv_b"], seq_len,
                            tm=_row_tile(seq_len, 1024), tn=512)

    cst = _fft_consts(seq_len)
    p4 = proj.reshape(batch, cst["a_out"], cst["n2"], proj.shape[1])
    spectra = lw["spectra"][seq_len]
    z = long_conv_gated(p4, 0, p4, 1, spectra, 0, seq_len, dh, BF16)
    y_h = long_conv_gated(z, 0, p4, 2, spectra, 1, seq_len, dh, BF16).reshape(batch * seq_len, dh)

    qcol = 3 * dh // HEAD_DIM
    y_a = dil_attn(proj, batch, seq_len, qcol, qcol + n_heads, qcol + 2 * n_heads, n_heads)
    kv = norm_matmul(mem, lw["g_mem"], ws["w_mem_kv"], layer, tm=n_mem, tn=512)
    y_m = mem_attn(proj, kv, batch, seq_len, n_mem, qcol + 3 * n_heads, dm // HEAD_DIM, tq=_row_tile(seq_len, 1024))

    x = mix_out(y_h, y_a, y_m, lw["g_grp"], ws["w_out"], layer, lw["g_post_mix"], x, _row_tile(seq_len, 512))
    return ffn(x, lw["g_pre_ffn"], ws["w_up"], lw["ffn_conv_w"], lw["ffn_conv_b"], ws["w_down"],
               lw["g_post_ffn"], layer, seq_len, _row_tile(seq_len, 1024), tf=512)


def kernel(x_prompt, x_sample, mem_prompt, mem_sample, g_pre_mix, w_in, conv_w, conv_b, f_w1, f_b1, f_w2, f_b2, f_w3, f_b3, f_w4, f_freq, f_bias, g_mem, w_mem_kv, g_grp, w_out, g_post_mix, g_pre_ffn, w_up, ffn_conv_w, ffn_conv_b, w_down, g_post_ffn):
    depth, d, d_in = w_in.shape
    dh = conv_w.shape[2] // 3
    dm = w_mem_kv.shape[2] // 2
    da = (d_in - 3 * dh - dm) // 3
    dims = (d, dh, da, dm)
    groups = [(x_prompt, mem_prompt), (x_sample, mem_sample)]
    seq_lens = sorted({g[0].shape[1] for g in groups})

    xs = [g[0].reshape(-1, d) for g in groups]
    mems = [g[1].reshape(-1, d) for g in groups]
    hid = FILTER_HIDDEN
    ws = dict(w_in=w_in.astype(BF16), w_mem_kv=w_mem_kv.astype(BF16), w_out=w_out.astype(BF16),
              w_up=w_up.astype(BF16), w_down=w_down.astype(BF16))
    for i in range(depth):
        row = lambda v: v[i][None, :]
        w1p = jnp.zeros((hid, hid), F32).at[:FILTER_EMB].set(f_w1[i])
        w4d = f_w4[i].reshape(hid, 2, 2, dh).transpose(2, 0, 1, 3).reshape(2, hid, 2 * dh)
        fw = (w1p, row(f_b1), f_w2[i], row(f_b2), f_w3[i], row(f_b3), row(f_freq), w4d)
        lw = dict(
            g_pre_mix=row(g_pre_mix), conv_w=conv_w[i], conv_b=row(conv_b),
            g_mem=row(g_mem), g_grp=row(g_grp), g_post_mix=row(g_post_mix), g_pre_ffn=row(g_pre_ffn),
            ffn_conv_w=ffn_conv_w[i], ffn_conv_b=row(ffn_conv_b), g_post_ffn=row(g_post_ffn),
            spectra={sl: hyena_filter_spectra(sl, dh, fw, f_bias[i]) for sl in seq_lens},
        )
        for gi, (xg, _) in enumerate(groups):
            xs[gi] = _layer(xs[gi], mems[gi], lw, ws, i, xg.shape[0], xg.shape[1], dims)
    return tuple(x.reshape(g[0].shape) for x, g in zip(xs, groups))
```

```python
import functools
import math

import numpy as np
import jax
import jax.numpy as jnp
from jax import lax
from jax.experimental import pallas as pl
from jax.experimental.pallas import tpu as pltpu

F32 = jnp.float32
BF16 = jnp.bfloat16

HEAD_DIM = 128
DILATED_PATTERNS = ((128, 1), (512, 4), (2048, 16))
ROPE_THETA = 500000.0
ROT_DIM = HEAD_DIM // 4
FILTER_EMB = 33
FILTER_HIDDEN = 64
DECAY_FAST = 0.3
DECAY_SLOW = 1.5
DECAY_TARGET = 1e-2
DECAY_SHIFT = 0.05
EPS = 1e-6
NEG = -1e30

HALO = 16
ATTN_BLOCKS_PER_STEP = 8
SUB = 8
SUB_BF16 = 16
MXU_DEPTH = 256
LONG_CONV_CT = 256
VMEM_LIMIT = 60 * 1024 * 1024


def _cparams(sem):
    return pltpu.CompilerParams(dimension_semantics=sem, vmem_limit_bytes=VMEM_LIMIT)


def _split(x):
    hi = x.astype(BF16)
    lo = (x - hi.astype(F32)).astype(BF16)
    return hi, lo


def _dot(a, b):
    return jnp.dot(a, b, preferred_element_type=F32)


def _rms(v, g):
    return v * lax.rsqrt(jnp.mean(v * v, axis=-1, keepdims=True) + EPS) * g


def _fill_normed(hb_ref, xp_ref, x_ref, xn_ref, g_ref, i, tm, blocks_per_seq):
    g = g_ref[...]
    pos = i % blocks_per_seq
    keep_p = (pos != 0).astype(F32)
    keep_n = (pos != blocks_per_seq - 1).astype(F32)
    hb_ref[0:HALO, :] = (_rms(xp_ref[...], g) * keep_p).astype(BF16)
    hb_ref[HALO:HALO + tm, :] = _rms(x_ref[...], g).astype(BF16)
    hb_ref[HALO + tm:, :] = (_rms(xn_ref[...], g) * keep_n).astype(BF16)


def _conv3(u, cw, cb, tm):
    rows = tm + 2 * HALO
    up = pltpu.roll(u, 1, 0)[HALO:HALO + tm]
    un = pltpu.roll(u, rows - 1, 0)[HALO:HALO + tm]
    uc = u[HALO:HALO + tm]
    return up * cw[0:1] + uc * cw[1:2] + un * cw[2:3] + cb


def _nm_conv_body(xp_ref, x_ref, xn_ref, g_ref, w_ref, cw_ref, cb_ref, o_ref, hb_ref, *, tm, blocks_per_seq,
                  n_conv):
    i = pl.program_id(0)
    j = pl.program_id(1)

    @pl.when(j == 0)
    def _():
        _fill_normed(hb_ref, xp_ref, x_ref, xn_ref, g_ref, i, tm, blocks_per_seq)

    u = _dot(hb_ref[...], w_ref[...])

    @pl.when(j < n_conv)
    def _():
        o_ref[...] = _conv3(u, cw_ref[...], cb_ref[...], tm).astype(o_ref.dtype)

    @pl.when(j >= n_conv)
    def _():
        o_ref[...] = u[HALO:HALO + tm].astype(o_ref.dtype)


def _nm_body(x_ref, g_ref, w_ref, o_ref, hb_ref):
    @pl.when(pl.program_id(1) == 0)
    def _():
        hb_ref[...] = _rms(x_ref[...], g_ref[...]).astype(BF16)

    o_ref[...] = _dot(hb_ref[...], w_ref[...]).astype(o_ref.dtype)


def _halo_specs(tm, d, n_rows, x_buffers=2):
    hb = tm // HALO
    last = n_rows // HALO - 1
    return [
        pl.BlockSpec((HALO, d), lambda i, j: (jnp.maximum(i * hb - 1, 0), 0)),
        pl.BlockSpec((tm, d), lambda i, j: (i, 0), pipeline_mode=pl.Buffered(x_buffers)),
        pl.BlockSpec((HALO, d), lambda i, j: (jnp.minimum((i + 1) * hb, last), 0)),
    ]


def norm_matmul_conv(x, g, w, layer, cw, cb, seq_len, tm, tn):
    t, d = x.shape
    n = w.shape[2]
    n_conv = cw.shape[1] // tn
    body = functools.partial(_nm_conv_body, tm=tm, blocks_per_seq=seq_len // tm, n_conv=n_conv)
    return pl.pallas_call(
        body,
        out_shape=jax.ShapeDtypeStruct((t, n), BF16),
        grid=(t // tm, n // tn),
        in_specs=_halo_specs(tm, d, t) + [
            pl.BlockSpec((1, d), lambda i, j: (0, 0)),
            pl.BlockSpec((None, d, tn), lambda i, j: (layer, 0, j)),
            pl.BlockSpec((3, tn), lambda i, j: (0, jnp.minimum(j, n_conv - 1))),
            pl.BlockSpec((1, tn), lambda i, j: (0, jnp.minimum(j, n_conv - 1))),
        ],
        out_specs=pl.BlockSpec((tm, tn), lambda i, j: (i, j)),
        scratch_shapes=[pltpu.VMEM((tm + 2 * HALO, d), BF16)],
        compiler_params=_cparams(("parallel", "arbitrary")),
        name="norm_matmul_conv",
    )(x, x, x, g, w, cw, cb)


def norm_matmul(x, g, w, layer, tm, tn):
    t, d = x.shape
    n = w.shape[2]
    return pl.pallas_call(
        _nm_body,
        out_shape=jax.ShapeDtypeStruct((t, n), BF16),
        grid=(t // tm, n // tn),
        in_specs=[
            pl.BlockSpec((tm, d), lambda i, j: (i, 0)),
            pl.BlockSpec((1, d), lambda i, j: (0, 0)),
            pl.BlockSpec((None, d, tn), lambda i, j: (layer, 0, j)),
        ],
        out_specs=pl.BlockSpec((tm, tn), lambda i, j: (i, j)),
        scratch_shapes=[pltpu.VMEM((tm, d), BF16)],
        compiler_params=_cparams(("parallel", "arbitrary")),
        name="norm_matmul",
    )(x, g, w)


def _ffn_body(xp_ref, x_ref, xn_ref, g_ref, wg_ref, wv_ref, cw_ref, cb_ref, wd_ref, gp_ref, o_ref,
              hb_ref, *, tm, blocks_per_seq, nj):
    i = pl.program_id(0)
    j = pl.program_id(1)

    @pl.when(j == 0)
    def _():
        _fill_normed(hb_ref, xp_ref, x_ref, xn_ref, g_ref, i, tm, blocks_per_seq)
        o_ref[...] = jnp.zeros_like(o_ref)

    ug = _dot(hb_ref[...], wg_ref[...])
    val = _dot(hb_ref[HALO:HALO + tm, :], wv_ref[...])
    gate = _conv3(ug, cw_ref[...], cb_ref[...], tm)
    ff = jax.nn.gelu(gate, approximate=True) * val
    o_ref[...] += _dot(ff.astype(BF16), wd_ref[...])

    @pl.when(j == nj - 1)
    def _():
        o_ref[...] = x_ref[...] + _rms(o_ref[...], gp_ref[...])


def ffn(x, g_pre, w_up, cw, cb, w_down, g_post, layer, seq_len, tm, tf):
    t, d = x.shape
    d_ff = w_down.shape[1]
    nj = d_ff // tf
    body = functools.partial(_ffn_body, tm=tm, blocks_per_seq=seq_len // tm, nj=nj)
    return pl.pallas_call(
        body,
        out_shape=jax.ShapeDtypeStruct((t, d), F32),
        grid=(t // tm, nj),
        in_specs=_halo_specs(tm, d, t) + [
            pl.BlockSpec((1, d), lambda i, j: (0, 0)),
            pl.BlockSpec((None, d, tf), lambda i, j: (layer, 0, j)),
            pl.BlockSpec((None, d, tf), lambda i, j: (layer, 0, j + nj)),
            pl.BlockSpec((3, tf), lambda i, j: (0, j)),
            pl.BlockSpec((1, tf), lambda i, j: (0, j)),
            pl.BlockSpec((None, tf, d), lambda i, j: (layer, j, 0)),
            pl.BlockSpec((1, d), lambda i, j: (0, 0)),
        ],
        out_specs=pl.BlockSpec((tm, d), lambda i, j: (i, 0)),
        scratch_shapes=[pltpu.VMEM((tm + 2 * HALO, d), BF16)],
        compiler_params=_cparams(("parallel", "arbitrary")),
        name="ffn",
    )(x, x, x, g_pre, w_up, w_up, cw, cb, w_down, g_post)


def _mix_out_body(yh_ref, ya_ref, ym_ref, gg_ref, w_ref, gp_ref, x_ref, o_ref, *, dh, da):
    gg = gg_ref[...]
    tm = x_ref.shape[0]
    for rows in (slice(0, tm // 2), slice(tm // 2, tm)):
        acc = _dot(_rms(yh_ref[rows, :].astype(F32), gg[:, :dh]).astype(BF16), w_ref[0:dh, :])
        acc += _dot(_rms(ya_ref[rows, :].astype(F32), gg[:, dh:dh + da]).astype(BF16), w_ref[dh:dh + da, :])
        acc += _dot(_rms(ym_ref[rows, :].astype(F32), gg[:, dh + da:]).astype(BF16), w_ref[dh + da:, :])
        o_ref[rows, :] = x_ref[rows, :] + _rms(acc, gp_ref[...])


def mix_out(yh, ya, ym, g_grp, w_out, layer, g_post, x, tm):
    t, d = x.shape
    dh, da, dm = yh.shape[1], ya.shape[1], ym.shape[1]
    body = functools.partial(_mix_out_body, dh=dh, da=da)
    return pl.pallas_call(
        body,
        out_shape=jax.ShapeDtypeStruct((t, d), F32),
        grid=(t // tm,),
        in_specs=[
            pl.BlockSpec((tm, dh), lambda i: (i, 0)),
            pl.BlockSpec((tm, da), lambda i: (i, 0)),
            pl.BlockSpec((tm, dm), lambda i: (i, 0)),
            pl.BlockSpec((1, d), lambda i: (0, 0)),
            pl.BlockSpec((None, d, d), lambda i: (layer, 0, 0)),
            pl.BlockSpec((1, d), lambda i: (0, 0)),
            pl.BlockSpec((tm, d), lambda i: (i, 0)),
        ],
        out_specs=pl.BlockSpec((tm, d), lambda i: (i, 0)),
        compiler_params=_cparams(("parallel",)),
        name="mix_out",
    )(yh, ya, ym, g_grp, w_out, g_post, x)


def _rope_tables(seq_len):
    half = ROT_DIM // 2
    inv_freq = np.exp(-math.log(ROPE_THETA) * np.arange(0, ROT_DIM, 2, dtype=np.float64) / ROT_DIM)
    ang = np.arange(seq_len, dtype=np.float64)[:, None] * inv_freq.astype(np.float32).astype(np.float64)[None, :]
    c, s = np.cos(ang), np.sin(ang)
    cos_t = np.ones((seq_len, HEAD_DIM), np.float64)
    sin_t = np.zeros((seq_len, HEAD_DIM), np.float64)
    cos_t[:, :half] = c
    cos_t[:, half:ROT_DIM] = c
    sin_t[:, :half] = -s
    sin_t[:, half:ROT_DIM] = s
    return cos_t.astype(np.float32), sin_t.astype(np.float32)


def _dil_attn_body(q_ref, k_ref, v_ref, cos_ref, sin_ref, o_ref, qs_ref, ks_ref, vs_ref, m_ref, l_ref, *, seq_len):
    half = ROT_DIM // 2
    scale = 1.0 / math.sqrt(HEAD_DIM)
    chunk = min(512, seq_len)

    def rope_chunk(c, carry):
        rows = pl.ds(pl.multiple_of(c * chunk, chunk), chunk)
        cs = cos_ref[rows, :]
        sn = sin_ref[rows, :]
        lane = lax.broadcasted_iota(jnp.int32, (chunk, HEAD_DIM), 1)
        for src, dst, mul in ((q_ref, qs_ref, scale), (k_ref, ks_ref, None)):
            x = src[rows, :].astype(F32)
            partner = jnp.where(lane < half, pltpu.roll(x, HEAD_DIM - half, 1), pltpu.roll(x, half, 1))
            y = x * cs + partner * sn
            dst[rows, :] = y if mul is None else y * mul
        vs_ref[rows, :] = v_ref[rows, :].astype(F32)
        return carry

    lax.fori_loop(0, seq_len // chunk, rope_chunk, 0)

    n_br = len(DILATED_PATTERNS)
    for bi, (window, dil) in enumerate(DILATED_PATTERNS):
        radius = window // (2 * dil)
        n = seq_len // dil
        tq = min(128, n)
        kw = min(tq + 2 * radius, n)
        nblk = n // tq

        def one_block(idx, bi=bi, dil=dil, radius=radius, n=n, tq=tq, kw=kw, nblk=nblk):
            r = idx // nblk
            q0 = (idx % nblk) * tq
            k0 = jnp.clip(q0 - radius, 0, n - kw)
            if dil == 1:
                qsl = pl.ds(pl.multiple_of(q0, tq), tq)
                ksl = pl.ds(pl.multiple_of(k0, 8), kw)
            else:
                qsl = pl.ds(r + q0 * dil, tq, stride=dil)
                ksl = pl.ds(r + k0 * dil, kw, stride=dil)
            qb = qs_ref[qsl, :].astype(BF16)
            kb = ks_ref[ksl, :].astype(BF16)
            vb = vs_ref[ksl, :].astype(BF16)
            s = lax.dot_general(qb, kb, (((1,), (1,)), ((), ())), preferred_element_type=F32)
            rel = (k0 + lax.broadcasted_iota(jnp.int32, (tq, kw), 1)) - (q0 + lax.broadcasted_iota(jnp.int32, (tq, kw), 0))
            s = jnp.where(jnp.abs(rel) <= radius, s, NEG)
            m = jnp.max(s, axis=-1, keepdims=True)
            p = jnp.exp(s - m)
            l = jnp.sum(p, axis=-1, keepdims=True)
            acc = _dot(p.astype(BF16), vb)
            m = jnp.broadcast_to(m, (tq, HEAD_DIM))
            l = jnp.broadcast_to(l, (tq, HEAD_DIM))
            if bi > 0:
                m_old = m_ref[qsl, :]
                m_new = jnp.maximum(m_old, m)
                a_old = jnp.exp(m_old - m_new)
                a_new = jnp.exp(m - m_new)
                acc = o_ref[qsl, :] * a_old + acc * a_new
                l = l_ref[qsl, :] * a_old + l * a_new
                m = m_new
            return qsl, m, l, acc

        total = dil * nblk
        unroll = min(ATTN_BLOCKS_PER_STEP, total)

        def step(it, carry, bi=bi, unroll=unroll, one_block=one_block):
            done = [one_block(it * unroll + u) for u in range(unroll)]
            for qsl, m, l, acc in done:
                if bi == n_br - 1:
                    o_ref[qsl, :] = acc / l
                else:
                    o_ref[qsl, :] = acc
                    m_ref[qsl, :] = m
                    l_ref[qsl, :] = l
            return carry

        lax.fori_loop(0, total // unroll, step, 0)


def dil_attn(proj, batch, seq_len, q_col, k_col, v_col, n_heads):
    cos_t, sin_t = _rope_tables(seq_len)
    nb = 1 if seq_len * HEAD_DIM * 4 > (2 << 20) else 2

    def col_spec(col):
        return pl.BlockSpec((seq_len, HEAD_DIM), lambda b, h: (b, col + h))

    tab_spec = pl.BlockSpec((seq_len, HEAD_DIM), lambda b, h: (0, 0), pipeline_mode=pl.Buffered(1))
    body = functools.partial(_dil_attn_body, seq_len=seq_len)
    return pl.pallas_call(
        body,
        out_shape=jax.ShapeDtypeStruct((batch * seq_len, n_heads * HEAD_DIM), F32),
        grid=(batch, n_heads),
        in_specs=[col_spec(q_col), col_spec(k_col), col_spec(v_col), tab_spec, tab_spec],
        out_specs=pl.BlockSpec((seq_len, HEAD_DIM), lambda b, h: (b, h), pipeline_mode=pl.Buffered(nb)),
        scratch_shapes=[pltpu.VMEM((seq_len, HEAD_DIM), F32)] * 5,
        compiler_params=_cparams(("parallel", "parallel")),
        name="dil_attn",
    )(proj, proj, proj, cos_t, sin_t)


def _mem_attn_body(q_ref, k_ref, v_ref, o_ref, *, n_heads):
    for h in range(n_heads):
        cols = slice(h * HEAD_DIM, (h + 1) * HEAD_DIM)
        s = lax.dot_general(q_ref[:, cols].astype(BF16), k_ref[:, cols].astype(BF16), (((1,), (1,)), ((), ())),
                            preferred_element_type=F32) * (1.0 / math.sqrt(HEAD_DIM))
        m = jnp.max(s, axis=-1, keepdims=True)
        p = jnp.exp(s - m)
        l = jnp.sum(p, axis=-1, keepdims=True)
        o_ref[:, cols] = (_dot(p.astype(BF16), v_ref[:, cols].astype(BF16)) / l).astype(o_ref.dtype)


def mem_attn(proj, kv, batch, seq_len, n_mem, q_col, n_heads, tq):
    nq = seq_len // tq
    width = n_heads * HEAD_DIM
    assert q_col % n_heads == 0, "the query heads must start at a multiple of their total width"
    body = functools.partial(_mem_attn_body, n_heads=n_heads)
    return pl.pallas_call(
        body,
        out_shape=jax.ShapeDtypeStruct((batch * seq_len, width), BF16),
        grid=(batch, nq),
        in_specs=[
            pl.BlockSpec((tq, width), lambda b, i: (b * nq + i, q_col // n_heads)),
            pl.BlockSpec((n_mem, width), lambda b, i: (b, 0)),
            pl.BlockSpec((n_mem, width), lambda b, i: (b, 1)),
        ],
        out_specs=pl.BlockSpec((tq, width), lambda b, i: (b * nq + i, 0)),
        compiler_params=_cparams(("parallel", "parallel")),
        name="mem_attn",
    )(proj, kv, kv)


def _fft_split(seq_len):
    n = 2 * seq_len
    n2 = MXU_DEPTH // 2
    return n // n2, n2


@functools.lru_cache(maxsize=None)
def _fft_consts(seq_len):
    n1, n2 = _fft_split(seq_len)
    n = n1 * n2
    kh = n1 // 2 + 1
    a_out = n1 // 2
    a = np.arange(n1)
    k1 = np.arange(kh)
    th = 2.0 * np.pi * ((k1[:, None] * a[None, :]) % n1) / n1
    fwd = np.stack([np.cos(th), -np.sin(th)], axis=1).reshape(2 * kh, n1)
    alpha = np.where((k1 == 0) | (k1 == n1 // 2), 1.0, 2.0) / n
    th_i = th[:, :a_out].T
    inv = np.stack([np.cos(th_i) * alpha[None, :], -np.sin(th_i) * alpha[None, :]], axis=2)
    inv = inv.reshape(a_out, 2 * kh)
    b = np.arange(n2)
    k2 = np.arange(n2)
    kk = k1[:, None, None] + n1 * k2[None, :, None]
    ph = 2.0 * np.pi * ((kk * b[None, None, :]) % n) / n
    wr, wi = np.cos(ph), -np.sin(ph)
    w_blk = np.concatenate([np.concatenate([wr, -wi], axis=2), np.concatenate([wi, wr], axis=2)], axis=1)

    def kron_bf16(m, sub):
        return np.asarray(np.kron(m, np.eye(sub)), np.float32).astype(BF16)

    sub_sig = SUB if a_out * SUB >= MXU_DEPTH else SUB_BF16
    return dict(n1=n1, n2=n2, kh=kh, a_out=a_out, sub_sig=sub_sig,
                a_fwd=kron_bf16(fwd, SUB),
                a_fwd_half=kron_bf16(fwd[:, :a_out], sub_sig),
                a_inv=kron_bf16(inv, sub_sig),
                w_fwd=np.asarray(w_blk, np.float32).astype(BF16))


def _filter_tables(seq_len, channels):
    n = 2 * seq_len
    bands = (FILTER_EMB - 1) // 2
    pos = np.concatenate([np.arange(seq_len), [0], np.arange(seq_len - 1, 0, -1)])
    t = np.linspace(0.0, 1.0, seq_len)[pos]
    w = 2.0 * np.pi * pos / seq_len
    f = np.linspace(1e-4, bands - 1, bands)
    ang = w[:, None] * f[None, :]
    z = np.zeros((n, FILTER_HIDDEN), np.float64)
    z[:, 0] = t
    z[:, 1:1 + bands] = np.cos(ang)
    z[:, 1 + bands:1 + 2 * bands] = -np.sin(ang)
    deltas = np.abs(np.linspace(math.log(DECAY_TARGET) / DECAY_SLOW, math.log(DECAY_TARGET) / DECAY_FAST, channels))
    return z.astype(np.float32), deltas[None, :].astype(np.float32)


def _filter_body(zt_ref, t_ref, w1_ref, b1_ref, w2_ref, b2_ref, w3_ref, b3_ref, fr_ref, w4_ref, dl_ref,
                 k_ref, s_ref, *, tr, seq_len, channels):
    i = pl.program_id(0)
    fr = fr_ref[...]

    def layer(ht, wt_ref, b_ref):
        w_hi, w_lo = _split(wt_ref[...])
        h_hi, h_lo = _split(ht)
        pre = _dot(w_hi, h_hi) + _dot(w_hi, h_lo) + _dot(w_lo, h_hi)
        return jnp.sin(fr * (pre + b_ref[...]))

    ht = layer(zt_ref[...], w1_ref, b1_ref)
    ht = layer(ht, w2_ref, b2_ref)
    ht = layer(ht, w3_ref, b3_ref)
    h = ht.T
    w_hi, w_lo = _split(w4_ref[...])
    h_hi, h_lo = _split(h)
    k = _dot(h_hi, w_hi) + _dot(h_lo, w_hi) + _dot(h_hi, w_lo)
    decay = jnp.exp(-t_ref[...] * dl_ref[...]) + DECAY_SHIFT
    row = i * tr + lax.broadcasted_iota(jnp.int32, (tr, 1), 0)
    decay = jnp.where(row == seq_len, 0.0, decay)
    k = k * jnp.concatenate([decay, decay], axis=1)
    k_ref[...] = k.astype(k_ref.dtype)

    @pl.when(i == 0)
    def _():
        s_ref[...] = jnp.zeros_like(s_ref)

    s_ref[...] += jnp.sum(jnp.abs(k).reshape(tr // 8, 8, 2 * channels), axis=0)


def filter_gen(seq_len, channels, w1p, b1, w2, b2, w3, b3, freq, w4d):
    n = 2 * seq_len
    tr = min(512, seq_len)
    z, deltas = _filter_tables(seq_len, channels)
    zt = np.ascontiguousarray(z.T)
    tcol = np.ascontiguousarray(z[:, 0:1])
    hid = FILTER_HIDDEN
    full = lambda shape: pl.BlockSpec(shape, lambda i: (0,) * len(shape))
    body = functools.partial(_filter_body, tr=tr, seq_len=seq_len, channels=channels)
    return pl.pallas_call(
        body,
        out_shape=(jax.ShapeDtypeStruct((n, 2 * channels), BF16), jax.ShapeDtypeStruct((8, 2 * channels), F32)),
        grid=(n // tr,),
        in_specs=[
            pl.BlockSpec((hid, tr), lambda i: (0, i)),
            pl.BlockSpec((tr, 1), lambda i: (i, 0)),
            full((hid, hid)), full((hid, 1)), full((hid, hid)), full((hid, 1)), full((hid, hid)), full((hid, 1)),
            full((hid, 1)),
            pl.BlockSpec((None, hid, 2 * channels), lambda i: ((i * tr) // seq_len, 0, 0)),
            full((1, channels)),
        ],
        out_specs=(pl.BlockSpec((tr, 2 * channels), lambda i: (i, 0)), full((8, 2 * channels))),
        compiler_params=_cparams(("arbitrary",)),
        name="filter_gen",
    )(zt, tcol, w1p.T, b1.T, w2.T, b2.T, w3.T, b3.T, freq.T, w4d, deltas)


def _major_fwd_body(u_ref, a_ref, y_ref, *, groups, sub):
    a_in, _, ct = u_ref.shape
    kh = y_ref.shape[0]
    ub = u_ref[...].astype(F32)
    ys = []
    for q in range(groups):
        u = ub[:, q * sub:(q + 1) * sub, :].reshape(a_in * sub, ct)
        ys.append(_dot(a_ref[...], u.astype(BF16)).reshape(kh, 2, sub, ct))
    y_ref[...] = jnp.concatenate(ys, axis=2).astype(y_ref.dtype)


def major_fwd(u, col, channels, a_fwd, kh, ct, groups, sub, out_dtype):
    bsz, a_in, n2, _ = u.shape
    nc = channels // ct
    rows = groups * sub
    body = functools.partial(_major_fwd_body, groups=groups, sub=sub)
    return pl.pallas_call(
        body,
        out_shape=jax.ShapeDtypeStruct((bsz, kh, 2, n2, channels), out_dtype),
        grid=(bsz, n2 // rows, nc),
        in_specs=[
            pl.BlockSpec((None, a_in, rows, ct), lambda b, r, j: (b, 0, r, col * nc + j)),
            pl.BlockSpec(a_fwd.shape, lambda b, r, j: (0, 0)),
        ],
        out_specs=pl.BlockSpec((None, kh, 2, rows, ct), lambda b, r, j: (b, 0, 0, r, j)),
        compiler_params=_cparams(("parallel", "parallel", "parallel")),
        name="major_fwd",
    )(u, a_fwd)


def _long_conv_body(u_ref, af_ref, ai_ref, wf_ref, k_ref, m_ref, o_ref, yg_ref, *,
                    nb, nk, kg, groups, sub, n2):
    s = pl.program_id(2)
    bb, a_len, rows, ct = u_ref.shape
    kh = yg_ref.shape[1]
    sdt = yg_ref.dtype
    wide = F32 if sub == SUB else u_ref.dtype

    @pl.when(s < nb)
    def _():
        for i in range(bb):
            ub = u_ref[i].astype(wide)
            for q in range(groups):
                u = ub[:, q * sub:(q + 1) * sub, :].reshape(a_len * sub, ct)
                y = _dot(af_ref[...], u.astype(BF16)).reshape(kh, 2, sub, ct)
                dst = pl.ds(pl.multiple_of(s * rows + q * sub, sub), sub)
                yg_ref[i, :, :, dst, :] = y.astype(sdt)

    @pl.when(jnp.logical_and(s >= nb, s < nb + nk))
    def _():
        for kk in range(kg):
            k1 = (s - nb) * kg + kk
            kr = k_ref[kk, 0].astype(F32)
            ki = k_ref[kk, 1].astype(F32)
            for i in range(bb):
                yv = jnp.concatenate([yg_ref[i, k1, 0].astype(BF16), yg_ref[i, k1, 1].astype(BF16)], axis=0)
                z = _dot(wf_ref[kk], yv)
                zr, zi = z[:n2], z[n2:]
                pr = (zr * kr - zi * ki).astype(BF16)
                pi = (zr * ki + zi * kr).astype(BF16)
                g = lax.dot_general(wf_ref[kk], jnp.concatenate([pr, pi], axis=0), (((0,), (0,)), ((), ())),
                                    preferred_element_type=F32)
                yg_ref[i, k1, 0] = g[:n2].astype(sdt)
                yg_ref[i, k1, 1] = g[n2:].astype(sdt)

    @pl.when(s >= nb + nk)
    def _():
        r = s - nb - nk
        for i in range(bb):
            ys = []
            for q in range(groups):
                src = pl.ds(pl.multiple_of(r * rows + q * sub, sub), sub)
                g = yg_ref[i, :, :, src, :].reshape(kh * 2 * sub, ct)
                ys.append(_dot(ai_ref[...], g.astype(BF16)).reshape(a_len, sub, ct))
            y = jnp.concatenate(ys, axis=1)
            o_ref[i] = (y * m_ref[i].astype(F32)).astype(o_ref.dtype)


def _k1_group(kh):
    return max(g for g in range(1, 18) if kh % g == 0)


def long_conv_gated(u, u_col, mult, mult_col, spectra, order, seq_len, channels, out_dtype):
    cst = _fft_consts(seq_len)
    kh, n2 = cst["kh"], cst["n2"]
    bsz, a_len = u.shape[0], u.shape[1]
    sub = cst["sub_sig"]
    ct = LONG_CONV_CT
    nc = channels // ct
    groups = _major_groups(a_len, sub, n2)
    rows = groups * sub
    nb = n2 // rows
    kg = _k1_group(kh)
    nk = kh // kg
    bb = max(g for g in (1, 2) if bsz % g == 0)
    a_fwd, a_inv = cst["a_fwd_half"], cst["a_inv"]

    def row_blk(s, first):
        return jnp.clip(s - first, 0, nb - 1)

    def k_blk(s):
        return jnp.clip(s - nb, 0, nk - 1)

    const = lambda shape: pl.BlockSpec(shape, lambda b, j, s: (0,) * len(shape), pipeline_mode=pl.Buffered(1))
    body = functools.partial(_long_conv_body, nb=nb, nk=nk, kg=kg, groups=groups, sub=sub, n2=n2)
    return pl.pallas_call(
        body,
        out_shape=jax.ShapeDtypeStruct((bsz, a_len, n2, channels), out_dtype),
        grid=(bsz // bb, nc, nb + nk + nb),
        in_specs=[
            pl.BlockSpec((bb, a_len, rows, ct), lambda b, j, s: (b, 0, row_blk(s, 0), u_col * nc + j)),
            const(a_fwd.shape),
            const(a_inv.shape),
            pl.BlockSpec((kg, 2 * n2, 2 * n2), lambda b, j, s: (k_blk(s), 0, 0)),
            pl.BlockSpec((kg, 2, n2, ct), lambda b, j, s: (k_blk(s), 0, 0, order * nc + j)),
            pl.BlockSpec((bb, a_len, rows, ct), lambda b, j, s: (b, 0, row_blk(s, nb + nk), mult_col * nc + j)),
        ],
        out_specs=pl.BlockSpec((bb, a_len, rows, ct), lambda b, j, s: (b, 0, row_blk(s, nb + nk), j)),
        scratch_shapes=[pltpu.VMEM((bb, kh, 2, n2, ct), F32 if sub == SUB else BF16)],
        compiler_params=_cparams(("parallel", "parallel", "arbitrary")),
        name="long_conv",
    )(u, a_fwd, a_inv, cst["w_fwd"], spectra, mult)


def _minor_filter_body(y_ref, wf_ref, inv_ref, bias_ref, k_ref, *, n2):
    z = _dot(wf_ref[...], jnp.concatenate([y_ref[0].astype(BF16), y_ref[1].astype(BF16)], axis=0))
    inv = inv_ref[...]
    k_ref[0] = (z[:n2] * inv + bias_ref[...]).astype(k_ref.dtype)
    k_ref[1] = (z[n2:] * inv).astype(k_ref.dtype)


def minor_filter(y, w_fwd, inv_norm, bias, ct):
    kh, _, n2, c2 = y.shape
    y_spec = pl.BlockSpec((None, 2, n2, ct), lambda k, j: (k, 0, 0, j))
    v_spec = pl.BlockSpec((1, ct), lambda k, j: (0, j))
    body = functools.partial(_minor_filter_body, n2=n2)
    return pl.pallas_call(
        body,
        out_shape=jax.ShapeDtypeStruct(y.shape, BF16),
        grid=(kh, c2 // ct),
        in_specs=[y_spec, pl.BlockSpec((None, 2 * n2, 2 * n2), lambda k, j: (k, 0, 0)), v_spec, v_spec],
        out_specs=y_spec,
        compiler_params=_cparams(("parallel", "parallel")),
        name="minor_filter",
    )(y, w_fwd, inv_norm, bias)


def _major_groups(a_in, sub, n2):
    return min(n2 // sub, max(1, 4096 // (a_in * sub)))


def hyena_filter_spectra(seq_len, channels, fw, f_bias):
    cst = _fft_consts(seq_len)
    n1, n2, kh = cst["n1"], cst["n2"], cst["kh"]
    kern, sums = filter_gen(seq_len, channels, *fw)
    inv_norm = 1.0 / jnp.sum(sums, axis=0, keepdims=True)
    u = kern.reshape(1, n1, n2, 2 * channels)
    y = major_fwd(u, 0, 2 * channels, cst["a_fwd"], kh, ct=512, groups=_major_groups(n1, SUB, n2), sub=SUB,
                  out_dtype=BF16)
    return minor_filter(y[0], cst["w_fwd"], inv_norm, f_bias.reshape(1, 2 * channels), ct=1024)


def _row_tile(seq_len, want):
    return min(want, seq_len)


def _layer(x, mem, lw, ws, layer, batch, seq_len, dims):
    d, dh, da, dm = dims
    n_heads = da // HEAD_DIM
    n_mem = mem.shape[0] // batch

    proj = norm_matmul_conv(x, lw["g_pre_mix"], ws["w_in"], layer, lw["conv_w"], lw["conv_b"], seq_len,
                            tm=_row_tile(seq_len, 1024), tn=512)

    cst = _fft_consts(seq_len)
    p4 = proj.reshape(batch, cst["a_out"], cst["n2"], proj.shape[1])
    spectra = lw["spectra"][seq_len]
    z = long_conv_gated(p4, 0, p4, 1, spectra, 0, seq_len, dh, BF16)
    y_h = long_conv_gated(z, 0, p4, 2, spectra, 1, seq_len, dh, BF16).reshape(batch * seq_len, dh)

    qcol = 3 * dh // HEAD_DIM
    y_a = dil_attn(proj, batch, seq_len, qcol, qcol + n_heads, qcol + 2 * n_heads, n_heads)
    kv = norm_matmul(mem, lw["g_mem"], ws["w_mem_kv"], layer, tm=n_mem, tn=512)
    y_m = mem_attn(proj, kv, batch, seq_len, n_mem, qcol + 3 * n_heads, dm // HEAD_DIM, tq=_row_tile(seq_len, 1024))

    x = mix_out(y_h, y_a, y_m, lw["g_grp"], ws["w_out"], layer, lw["g_post_mix"], x, _row_tile(seq_len, 512))
    return ffn(x, lw["g_pre_ffn"], ws["w_up"], lw["ffn_conv_w"], lw["ffn_conv_b"], ws["w_down"],
               lw["g_post_ffn"], layer, seq_len, _row_tile(seq_len, 1024), tf=512)


def kernel(x_prompt, x_sample, mem_prompt, mem_sample, g_pre_mix, w_in, conv_w, conv_b, f_w1, f_b1, f_w2, f_b2, f_w3, f_b3, f_w4, f_freq, f_bias, g_mem, w_mem_kv, g_grp, w_out, g_post_mix, g_pre_ffn, w_up, ffn_conv_w, ffn_conv_b, w_down, g_post_ffn):
    depth, d, d_in = w_in.shape
    dh = conv_w.shape[2] // 3
    dm = w_mem_kv.shape[2] // 2
    da = (d_in - 3 * dh - dm) // 3
    dims = (d, dh, da, dm)
    groups = [(x_prompt, mem_prompt), (x_sample, mem_sample)]
    seq_lens = sorted({g[0].shape[1] for g in groups})

    xs = [g[0].reshape(-1, d) for g in groups]
    mems = [g[1].reshape(-1, d) for g in groups]
    hid = FILTER_HIDDEN
    ws = dict(w_in=w_in.astype(BF16), w_mem_kv=w_mem_kv.astype(BF16), w_out=w_out.astype(BF16),
              w_up=w_up.astype(BF16), w_down=w_down.astype(BF16))
    for i in range(depth):
        row = lambda v: v[i][None, :]
        w1p = jnp.zeros((hid, hid), F32).at[:FILTER_EMB].set(f_w1[i])
        w4d = f_w4[i].reshape(hid, 2, 2, dh).transpose(2, 0, 1, 3).reshape(2, hid, 2 * dh)
        fw = (w1p, row(f_b1), f_w2[i], row(f_b2), f_w3[i], row(f_b3), row(f_freq), w4d)
        lw = dict(
            g_pre_mix=row(g_pre_mix), conv_w=conv_w[i], conv_b=row(conv_b),
            g_mem=row(g_mem), g_grp=row(g_grp), g_post_mix=row(g_post_mix), g_pre_ffn=row(g_pre_ffn),
            ffn_conv_w=ffn_conv_w[i], ffn_conv_b=row(ffn_conv_b), g_post_ffn=row(g_post_ffn),
            spectra={sl: hyena_filter_spectra(sl, dh, fw, f_bias[i]) for sl in seq_lens},
        )
        for gi, (xg, _) in enumerate(groups):
            xs[gi] = _layer(xs[gi], mems[gi], lw, ws, i, xg.shape[0], xg.shape[1], dims)
    return tuple(x.reshape(g[0].shape) for x, g in zip(xs, groups))
```
